```python
import jax, jax.numpy as jnp
from jax import lax
import numpy as np

D_MODEL = 2048
BATCH = 8
SEQ = 8192
DEPTH = 4

N_MIXERS = 2
N_ATTN_LAYERS = (DEPTH + N_MIXERS - 1) // N_MIXERS
N_RNN_LAYERS = DEPTH // N_MIXERS

GRID_W = 64
ROPE_THETA = 10000.0

HEAD_DIM = 128
N_Q_HEADS = D_MODEL // HEAD_DIM
N_KV_HEADS = 4
GQA_GROUP = N_Q_HEADS // N_KV_HEADS
Q_BLOCK = 128
ROPE_AXIS_DIM = HEAD_DIM // 2
ROPE_FREQS = ROPE_AXIS_DIM // 2
QKV_DIM = (N_Q_HEADS + 2 * N_KV_HEADS) * HEAD_DIM

D_RNN = D_MODEL
RNN_BLOCK_W = 256
RNN_BLOCKS = D_RNN // RNN_BLOCK_W
CONV_W = 4
CONV_LEFT = 2
CONV_RIGHT = CONV_W - 1 - CONV_LEFT
LRU_C = 8.0

D_FF = -(-8 * D_MODEL // (3 * 256)) * 256

EPS = 1e-6

kernel_name = "hybrid_axial_gqa_rglru_encoder"


def rms_norm(x, g):
    xf = x.astype(jnp.float32)
    y = xf * lax.rsqrt(jnp.mean(xf * xf, axis=-1, keepdims=True) + EPS)
    return (y * g.astype(jnp.float32)).astype(x.dtype)


def axial_rope_tables(seq_len):
    rows_n = seq_len // GRID_W
    freqs = ROPE_THETA ** (-jnp.arange(ROPE_FREQS, dtype=jnp.float32) / ROPE_FREQS)
    row_ang = jnp.arange(rows_n, dtype=jnp.float32)[:, None, None] * freqs
    col_ang = jnp.arange(GRID_W, dtype=jnp.float32)[None, :, None] * freqs
    shape = (rows_n, GRID_W, ROPE_FREQS)
    ang = jnp.stack([jnp.broadcast_to(row_ang, shape), jnp.broadcast_to(col_ang, shape)], axis=-2)
    ang = ang.reshape(seq_len, 2, ROPE_FREQS)
    return jnp.cos(ang), jnp.sin(ang)


def apply_axial_rope(x, cos, sin):
    xr = x.astype(jnp.float32).reshape(*x.shape[:-1], 2, 2, ROPE_FREQS)
    x1, x2 = xr[..., 0, :], xr[..., 1, :]
    c = cos[None, :, None]
    s = sin[None, :, None]
    out = jnp.stack([x1 * c - x2 * s, x2 * c + x1 * s], axis=-2)
    return out.reshape(x.shape).astype(x.dtype)


def attention_mixer(h, w_qkv, q_gain, k_gain, w_o, cos, sin):
    b, s, _ = h.shape
    qkv = h @ w_qkv
    q, k, v = jnp.split(qkv, [N_Q_HEADS * HEAD_DIM, (N_Q_HEADS + N_KV_HEADS) * HEAD_DIM], axis=-1)
    q = q.reshape(b, s, N_Q_HEADS, HEAD_DIM)
    k = k.reshape(b, s, N_KV_HEADS, HEAD_DIM)
    v = v.reshape(b, s, N_KV_HEADS, HEAD_DIM)
    q = apply_axial_rope(rms_norm(q, q_gain), cos, sin) * (HEAD_DIM ** -0.5)
    k = apply_axial_rope(rms_norm(k, k_gain), cos, sin)
    n_blocks = s // Q_BLOCK
    qb = q.reshape(b, n_blocks, Q_BLOCK, N_KV_HEADS, GQA_GROUP, HEAD_DIM).transpose(1, 0, 2, 3, 4, 5)

    def one_block(q_blk):
        sc = jnp.einsum('bqkgd,bskd->bkgqs', q_blk, k).astype(jnp.float32)
        p = jax.nn.softmax(sc, axis=-1).astype(v.dtype)
        return jnp.einsum('bkgqs,bskd->bqkgd', p, v)

    o = lax.map(one_block, qb)
    o = o.transpose(1, 0, 2, 3, 4, 5).reshape(b, s, N_Q_HEADS * HEAD_DIM)
    return o @ w_o


def depthwise_conv_centred(x, w, bias):
    y = lax.conv_general_dilated(
        x, w[:, None, :], window_strides=(1,), padding=[(CONV_LEFT, CONV_RIGHT)],
        dimension_numbers=('NWC', 'WIO', 'NWC'), feature_group_count=x.shape[-1])
    return y + bias


def block_diag_linear(x, w, bias):
    b, s, _ = x.shape
    xb = x.reshape(b, s, RNN_BLOCKS, RNN_BLOCK_W)
    return jnp.einsum('bsni,nij->bsnj', xb, w).reshape(b, s, D_RNN) + bias


def rg_lru(x, w_a, b_a, w_i, b_i, lam, reverse):
    xf = x.astype(jnp.float32)
    r = jax.nn.sigmoid(block_diag_linear(xf, w_a, b_a).astype(jnp.float32))
    i = jax.nn.sigmoid(block_diag_linear(xf, w_i, b_i).astype(jnp.float32))
    log_a = -LRU_C * r * jax.nn.softplus(-lam.astype(jnp.float32))
    a = jnp.exp(log_a)
    u = jnp.sqrt(-jnp.expm1(2.0 * log_a)) * (i * xf)

    def combine(e1, e2):
        a1, b1 = e1
        a2, b2 = e2
        return a1 * a2, a2 * b1 + b2

    _, hs = lax.associative_scan(combine, (a, u), axis=1, reverse=reverse)
    return hs


def recurrent_mixer(h, w_in, conv_w, conv_b, w_a, b_a, w_i, b_i, lam, w_out):
    xy = h @ w_in
    xb, yb = jnp.split(xy, 2, axis=-1)
    yb = jax.nn.gelu(yb, approximate=True)
    xb = depthwise_conv_centred(xb, conv_w, conv_b)
    h_fwd = rg_lru(xb, w_a[0], b_a[0], w_i[0], b_i[0], lam[0], reverse=False)
    h_bwd = rg_lru(xb, w_a[1], b_a[1], w_i[1], b_i[1], lam[1], reverse=True)
    out = (h_fwd + h_bwd).astype(h.dtype) * yb
    return out @ w_out


def swiglu(h, w_gate, w_up, w_down):
    return (jax.nn.silu(h @ w_gate) * (h @ w_up)) @ w_down


def _fwd_setup_inputs(seed: int = 0) -> dict:
    key = jax.random.key(seed)
    ks = jax.random.split(key, 24)
    f32 = jnp.float32

    def nrm(k, shape, fan_in):
        return jax.random.normal(k, shape, f32) * (fan_in ** -0.5)

    def small(k, shape, scale=0.01):
        return jax.random.normal(k, shape, f32) * scale

    u = jax.random.uniform(ks[15], (N_RNN_LAYERS, 2, D_RNN), f32, minval=0.9, maxval=0.999)
    s_lam = u ** (1.0 / LRU_C)
    rnn_lambda = jnp.log(s_lam) - jnp.log1p(-s_lam)

    return {
        "x": jax.random.normal(ks[0], (BATCH, SEQ, D_MODEL), f32),
        "norm_mix": 1.0 + small(ks[1], (DEPTH, D_MODEL), 0.02),
        "norm_ffn": 1.0 + small(ks[2], (DEPTH, D_MODEL), 0.02),
        "attn_w_qkv": nrm(ks[3], (N_ATTN_LAYERS, D_MODEL, QKV_DIM), D_MODEL),
        "attn_q_gain": 1.0 + small(ks[4], (N_ATTN_LAYERS, HEAD_DIM), 0.02),
        "attn_k_gain": 1.0 + small(ks[5], (N_ATTN_LAYERS, HEAD_DIM), 0.02),
        "attn_w_o": nrm(ks[6], (N_ATTN_LAYERS, N_Q_HEADS * HEAD_DIM, D_MODEL), N_Q_HEADS * HEAD_DIM),
        "rnn_w_in": nrm(ks[7], (N_RNN_LAYERS, D_MODEL, 2 * D_RNN), D_MODEL),
        "rnn_conv_w": nrm(ks[8], (N_RNN_LAYERS, CONV_W, D_RNN), CONV_W),
        "rnn_conv_b": small(ks[9], (N_RNN_LAYERS, D_RNN)),
        "rnn_w_a": nrm(ks[10], (N_RNN_LAYERS, 2, RNN_BLOCKS, RNN_BLOCK_W, RNN_BLOCK_W), RNN_BLOCK_W),
        "rnn_b_a": small(ks[11], (N_RNN_LAYERS, 2, D_RNN)),
        "rnn_w_i": nrm(ks[12], (N_RNN_LAYERS, 2, RNN_BLOCKS, RNN_BLOCK_W, RNN_BLOCK_W), RNN_BLOCK_W),
        "rnn_b_i": small(ks[13], (N_RNN_LAYERS, 2, D_RNN)),
        "rnn_lambda": rnn_lambda,
        "rnn_w_out": nrm(ks[14], (N_RNN_LAYERS, D_RNN, D_MODEL), D_RNN),
        "ffn_w_gate": nrm(ks[16], (DEPTH, D_MODEL, D_FF), D_MODEL),
        "ffn_w_up": nrm(ks[17], (DEPTH, D_MODEL, D_FF), D_MODEL),
        "ffn_w_down": nrm(ks[18], (DEPTH, D_FF, D_MODEL), D_FF),
    }


def _fwd_reference(x, norm_mix, norm_ffn, attn_w_qkv, attn_q_gain, attn_k_gain, attn_w_o,
              rnn_w_in, rnn_conv_w, rnn_conv_b, rnn_w_a, rnn_b_a, rnn_w_i, rnn_b_i,
              rnn_lambda, rnn_w_out, ffn_w_gate, ffn_w_up, ffn_w_down):
    seq_len = x.shape[1]
    cos, sin = axial_rope_tables(seq_len)
    for i in range(DEPTH):
        j = i // N_MIXERS
        h = rms_norm(x, norm_mix[i])
        if i % N_MIXERS == 0:
            mix = attention_mixer(h, attn_w_qkv[j], attn_q_gain[j], attn_k_gain[j], attn_w_o[j], cos, sin)
        else:
            mix = recurrent_mixer(h, rnn_w_in[j], rnn_conv_w[j], rnn_conv_b[j], rnn_w_a[j], rnn_b_a[j],
                                  rnn_w_i[j], rnn_b_i[j], rnn_lambda[j], rnn_w_out[j])
        x = x + mix
        x = x + swiglu(rms_norm(x, norm_ffn[i]), ffn_w_gate[i], ffn_w_up[i], ffn_w_down[i])
    return x


import jax as _jax
import jax.numpy as _jnp

TWIN_FORMAT = 'train_step'
FWD_PARAMS = ['x', 'norm_mix', 'norm_ffn', 'attn_w_qkv', 'attn_q_gain', 'attn_k_gain', 'attn_w_o', 'rnn_w_in', 'rnn_conv_w', 'rnn_conv_b', 'rnn_w_a', 'rnn_b_a', 'rnn_w_i', 'rnn_b_i', 'rnn_lambda', 'rnn_w_out', 'ffn_w_gate', 'ffn_w_up', 'ffn_w_down']
TWIN_WEIGHTS = ['norm_mix', 'norm_ffn', 'attn_w_qkv', 'attn_q_gain', 'attn_k_gain', 'attn_w_o', 'rnn_w_in', 'rnn_conv_w', 'rnn_conv_b', 'rnn_w_a', 'rnn_b_a', 'rnn_w_i', 'rnn_b_i', 'rnn_lambda', 'rnn_w_out', 'ffn_w_gate', 'ffn_w_up', 'ffn_w_down']
TWIN_DIFF_INPUT = 'x'
TWIN_INPUTS = ['x', 'norm_mix', 'norm_ffn', 'attn_w_qkv', 'attn_q_gain', 'attn_k_gain', 'attn_w_o', 'rnn_w_in', 'rnn_conv_w', 'rnn_conv_b', 'rnn_w_a', 'rnn_b_a', 'rnn_w_i', 'rnn_b_i', 'rnn_lambda', 'rnn_w_out', 'ffn_w_gate', 'ffn_w_up', 'ffn_w_down', 'loss_target', 'm_norm_mix', 'm_norm_ffn', 'm_attn_w_qkv', 'm_attn_q_gain', 'm_attn_k_gain', 'm_attn_w_o', 'm_rnn_w_in', 'm_rnn_conv_w', 'm_rnn_conv_b', 'm_rnn_w_a', 'm_rnn_b_a', 'm_rnn_w_i', 'm_rnn_b_i', 'm_rnn_lambda', 'm_rnn_w_out', 'm_ffn_w_gate', 'm_ffn_w_up', 'm_ffn_w_down', 'v_norm_mix', 'v_norm_ffn', 'v_attn_w_qkv', 'v_attn_q_gain', 'v_attn_k_gain', 'v_attn_w_o', 'v_rnn_w_in', 'v_rnn_conv_w', 'v_rnn_conv_b', 'v_rnn_w_a', 'v_rnn_b_a', 'v_rnn_w_i', 'v_rnn_b_i', 'v_rnn_lambda', 'v_rnn_w_out', 'v_ffn_w_gate', 'v_ffn_w_up', 'v_ffn_w_down']
TWIN_OUTPUTS = ['loss', 'grad_x', 'grad_norm_mix', 'grad_norm_ffn', 'grad_attn_w_qkv', 'grad_attn_q_gain', 'grad_attn_k_gain', 'grad_attn_w_o', 'grad_rnn_w_in', 'grad_rnn_conv_w', 'grad_rnn_conv_b', 'grad_rnn_w_a', 'grad_rnn_b_a', 'grad_rnn_w_i', 'grad_rnn_b_i', 'grad_rnn_lambda', 'grad_rnn_w_out', 'grad_ffn_w_gate', 'grad_ffn_w_up', 'grad_ffn_w_down', 'delta_norm_mix', 'delta_norm_ffn', 'delta_attn_w_qkv', 'delta_attn_q_gain', 'delta_attn_k_gain', 'delta_attn_w_o', 'delta_rnn_w_in', 'delta_rnn_conv_w', 'delta_rnn_conv_b', 'delta_rnn_w_a', 'delta_rnn_b_a', 'delta_rnn_w_i', 'delta_rnn_b_i', 'delta_rnn_lambda', 'delta_rnn_w_out', 'delta_ffn_w_gate', 'delta_ffn_w_up', 'delta_ffn_w_down', 'new_m_norm_mix', 'new_m_norm_ffn', 'new_m_attn_w_qkv', 'new_m_attn_q_gain', 'new_m_attn_k_gain', 'new_m_attn_w_o', 'new_m_rnn_w_in', 'new_m_rnn_conv_w', 'new_m_rnn_conv_b', 'new_m_rnn_w_a', 'new_m_rnn_b_a', 'new_m_rnn_w_i', 'new_m_rnn_b_i', 'new_m_rnn_lambda', 'new_m_rnn_w_out', 'new_m_ffn_w_gate', 'new_m_ffn_w_up', 'new_m_ffn_w_down', 'new_v_norm_mix', 'new_v_norm_ffn', 'new_v_attn_w_qkv', 'new_v_attn_q_gain', 'new_v_attn_k_gain', 'new_v_attn_w_o', 'new_v_rnn_w_in', 'new_v_rnn_conv_w', 'new_v_rnn_conv_b', 'new_v_rnn_w_a', 'new_v_rnn_b_a', 'new_v_rnn_w_i', 'new_v_rnn_b_i', 'new_v_rnn_lambda', 'new_v_rnn_w_out', 'new_v_ffn_w_gate', 'new_v_ffn_w_up', 'new_v_ffn_w_down']
TWIN_LEAF_KINDS = {'loss': 'loss', 'grad_x': 'grad_x', 'grad_norm_mix': 'grad_w', 'grad_norm_ffn': 'grad_w', 'grad_attn_w_qkv': 'grad_w', 'grad_attn_q_gain': 'grad_w', 'grad_attn_k_gain': 'grad_w', 'grad_attn_w_o': 'grad_w', 'grad_rnn_w_in': 'grad_w', 'grad_rnn_conv_w': 'grad_w', 'grad_rnn_conv_b': 'grad_w', 'grad_rnn_w_a': 'grad_w', 'grad_rnn_b_a': 'grad_w', 'grad_rnn_w_i': 'grad_w', 'grad_rnn_b_i': 'grad_w', 'grad_rnn_lambda': 'grad_w', 'grad_rnn_w_out': 'grad_w', 'grad_ffn_w_gate': 'grad_w', 'grad_ffn_w_up': 'grad_w', 'grad_ffn_w_down': 'grad_w', 'delta_norm_mix': 'delta_w', 'delta_norm_ffn': 'delta_w', 'delta_attn_w_qkv': 'delta_w', 'delta_attn_q_gain': 'delta_w', 'delta_attn_k_gain': 'delta_w', 'delta_attn_w_o': 'delta_w', 'delta_rnn_w_in': 'delta_w', 'delta_rnn_conv_w': 'delta_w', 'delta_rnn_conv_b': 'delta_w', 'delta_rnn_w_a': 'delta_w', 'delta_rnn_b_a': 'delta_w', 'delta_rnn_w_i': 'delta_w', 'delta_rnn_b_i': 'delta_w', 'delta_rnn_lambda': 'delta_w', 'delta_rnn_w_out': 'delta_w', 'delta_ffn_w_gate': 'delta_w', 'delta_ffn_w_up': 'delta_w', 'delta_ffn_w_down': 'delta_w', 'new_m_norm_mix': 'new_m', 'new_m_norm_ffn': 'new_m', 'new_m_attn_w_qkv': 'new_m', 'new_m_attn_q_gain': 'new_m', 'new_m_attn_k_gain': 'new_m', 'new_m_attn_w_o': 'new_m', 'new_m_rnn_w_in': 'new_m', 'new_m_rnn_conv_w': 'new_m', 'new_m_rnn_conv_b': 'new_m', 'new_m_rnn_w_a': 'new_m', 'new_m_rnn_b_a': 'new_m', 'new_m_rnn_w_i': 'new_m', 'new_m_rnn_b_i': 'new_m', 'new_m_rnn_lambda': 'new_m', 'new_m_rnn_w_out': 'new_m', 'new_m_ffn_w_gate': 'new_m', 'new_m_ffn_w_up': 'new_m', 'new_m_ffn_w_down': 'new_m', 'new_v_norm_mix': 'new_v', 'new_v_norm_ffn': 'new_v', 'new_v_attn_w_qkv': 'new_v', 'new_v_attn_q_gain': 'new_v', 'new_v_attn_k_gain': 'new_v', 'new_v_attn_w_o': 'new_v', 'new_v_rnn_w_in': 'new_v', 'new_v_rnn_conv_w': 'new_v', 'new_v_rnn_conv_b': 'new_v', 'new_v_rnn_w_a': 'new_v', 'new_v_rnn_b_a': 'new_v', 'new_v_rnn_w_i': 'new_v', 'new_v_rnn_b_i': 'new_v', 'new_v_rnn_lambda': 'new_v', 'new_v_rnn_w_out': 'new_v', 'new_v_ffn_w_gate': 'new_v', 'new_v_ffn_w_up': 'new_v', 'new_v_ffn_w_down': 'new_v'}


def _forward(args):
    return _fwd_reference(*[args[k] for k in FWD_PARAMS])


def _output_shape():
    def fwd():
        inp = _fwd_setup_inputs(0)
        return _fwd_reference(*[inp[k] for k in FWD_PARAMS])
    out = _jax.eval_shape(fwd)
    return out.shape, out.dtype

N_MICROBATCH = 1
ADAM_LR = 0.001
ADAM_B1 = 0.9
ADAM_B2 = 0.999
ADAM_EPS = 1e-08
ADAM_WD = 0.01
ADAM_STEP = 10
PER_EXAMPLE_BATCH_AXIS = {'x': 0, 'loss_target': 0}
SHARED_INPUTS = []
_WEIGHT_DTYPES = {'norm_mix': _jnp.float32, 'norm_ffn': _jnp.float32, 'attn_w_qkv': _jnp.float32, 'attn_q_gain': _jnp.float32, 'attn_k_gain': _jnp.float32, 'attn_w_o': _jnp.float32, 'rnn_w_in': _jnp.float32, 'rnn_conv_w': _jnp.float32, 'rnn_conv_b': _jnp.float32, 'rnn_w_a': _jnp.float32, 'rnn_b_a': _jnp.float32, 'rnn_w_i': _jnp.float32, 'rnn_b_i': _jnp.float32, 'rnn_lambda': _jnp.float32, 'rnn_w_out': _jnp.float32, 'ffn_w_gate': _jnp.float32, 'ffn_w_up': _jnp.float32, 'ffn_w_down': _jnp.float32}
MOMENT_SCALE = {'norm_mix': 4.607442e+01, 'norm_ffn': 2.165660e+01, 'attn_w_qkv': 1.174824e+01, 'attn_q_gain': 2.575153e+00, 'attn_k_gain': 2.929695e+00, 'attn_w_o': 1.279420e+01, 'rnn_w_in': 1.395900e+01, 'rnn_conv_w': 3.094397e+01, 'rnn_conv_b': 1.877244e+02, 'rnn_w_a': 2.254308e+00, 'rnn_b_a': 2.393073e+00, 'rnn_w_i': 4.214710e+00, 'rnn_b_i': 9.972206e+00, 'rnn_lambda': 5.624548e+00, 'rnn_w_out': 1.277167e+01, 'ffn_w_gate': 3.973400e+00, 'ffn_w_up': 4.118686e+00, 'ffn_w_down': 6.816704e+00}


def _to_microbatches(a, axis):
    t = _jnp.moveaxis(a, axis, 0)
    t = t.reshape((N_MICROBATCH, t.shape[0] // N_MICROBATCH) + t.shape[1:])
    return _jnp.moveaxis(t, 1, axis + 1)


def setup_inputs(seed: int = 0) -> dict:
    inp = _fwd_setup_inputs(seed)
    key = _jax.random.fold_in(_jax.random.key(seed), 7919)
    shape, _ = _output_shape()
    out = dict(inp)
    out["loss_target"] = _jax.random.normal(_jax.random.fold_in(key, 0), shape, _jnp.float32)
    for i, name in enumerate(TWIN_WEIGHTS):
        w = inp[name].astype(_jnp.float32)
        if MOMENT_SCALE is None:
            s = _jnp.sqrt(_jnp.mean(_jnp.square(w)) + 1e-30)
        else:
            s = MOMENT_SCALE[name]
        km, kv = _jax.random.split(_jax.random.fold_in(key, i + 1))
        out[name] = w
        out["m_" + name] = s * _jax.random.normal(km, w.shape, _jnp.float32)
        out["v_" + name] = (s * s) * _jax.random.uniform(kv, w.shape, _jnp.float32, 0.5, 1.5)
    if N_MICROBATCH > 1:
        for name, axis in PER_EXAMPLE_BATCH_AXIS.items():
            out[name] = _to_microbatches(out[name], axis)
    return {'x': out['x'], 'norm_mix': out['norm_mix'], 'norm_ffn': out['norm_ffn'], 'attn_w_qkv': out['attn_w_qkv'], 'attn_q_gain': out['attn_q_gain'], 'attn_k_gain': out['attn_k_gain'], 'attn_w_o': out['attn_w_o'], 'rnn_w_in': out['rnn_w_in'], 'rnn_conv_w': out['rnn_conv_w'], 'rnn_conv_b': out['rnn_conv_b'], 'rnn_w_a': out['rnn_w_a'], 'rnn_b_a': out['rnn_b_a'], 'rnn_w_i': out['rnn_w_i'], 'rnn_b_i': out['rnn_b_i'], 'rnn_lambda': out['rnn_lambda'], 'rnn_w_out': out['rnn_w_out'], 'ffn_w_gate': out['ffn_w_gate'], 'ffn_w_up': out['ffn_w_up'], 'ffn_w_down': out['ffn_w_down'], 'loss_target': out['loss_target'], 'm_norm_mix': out['m_norm_mix'], 'm_norm_ffn': out['m_norm_ffn'], 'm_attn_w_qkv': out['m_attn_w_qkv'], 'm_attn_q_gain': out['m_attn_q_gain'], 'm_attn_k_gain': out['m_attn_k_gain'], 'm_attn_w_o': out['m_attn_w_o'], 'm_rnn_w_in': out['m_rnn_w_in'], 'm_rnn_conv_w': out['m_rnn_conv_w'], 'm_rnn_conv_b': out['m_rnn_conv_b'], 'm_rnn_w_a': out['m_rnn_w_a'], 'm_rnn_b_a': out['m_rnn_b_a'], 'm_rnn_w_i': out['m_rnn_w_i'], 'm_rnn_b_i': out['m_rnn_b_i'], 'm_rnn_lambda': out['m_rnn_lambda'], 'm_rnn_w_out': out['m_rnn_w_out'], 'm_ffn_w_gate': out['m_ffn_w_gate'], 'm_ffn_w_up': out['m_ffn_w_up'], 'm_ffn_w_down': out['m_ffn_w_down'], 'v_norm_mix': out['v_norm_mix'], 'v_norm_ffn': out['v_norm_ffn'], 'v_attn_w_qkv': out['v_attn_w_qkv'], 'v_attn_q_gain': out['v_attn_q_gain'], 'v_attn_k_gain': out['v_attn_k_gain'], 'v_attn_w_o': out['v_attn_w_o'], 'v_rnn_w_in': out['v_rnn_w_in'], 'v_rnn_conv_w': out['v_rnn_conv_w'], 'v_rnn_conv_b': out['v_rnn_conv_b'], 'v_rnn_w_a': out['v_rnn_w_a'], 'v_rnn_b_a': out['v_rnn_b_a'], 'v_rnn_w_i': out['v_rnn_w_i'], 'v_rnn_b_i': out['v_rnn_b_i'], 'v_rnn_lambda': out['v_rnn_lambda'], 'v_rnn_w_out': out['v_rnn_w_out'], 'v_ffn_w_gate': out['v_ffn_w_gate'], 'v_ffn_w_up': out['v_ffn_w_up'], 'v_ffn_w_down': out['v_ffn_w_down']}


def _loss(weights, diff, rest, loss_target):
    with _jax.named_scope("forward"):
        args = {**rest, TWIN_DIFF_INPUT: diff, **{k: w.astype(_WEIGHT_DTYPES[k]) for k, w in weights.items()}}
        y = _forward(args)
    with _jax.named_scope("loss_head"):
        err = _jnp.square(y.astype(_jnp.float32) - loss_target)
        return 0.5 * _jnp.sum(_jnp.mean(err, axis=-1)) if err.ndim else 0.5 * err


def _adamw(w, g, m, v):
    m = ADAM_B1 * m + (1.0 - ADAM_B1) * g
    v = ADAM_B2 * v + (1.0 - ADAM_B2) * _jnp.square(g)
    m_hat = m / (1.0 - ADAM_B1 ** ADAM_STEP)
    v_hat = v / (1.0 - ADAM_B2 ** ADAM_STEP)
    delta = -ADAM_LR * (m_hat / (_jnp.sqrt(v_hat) + ADAM_EPS) + ADAM_WD * w)
    return delta, m, v


def reference(x, norm_mix, norm_ffn, attn_w_qkv, attn_q_gain, attn_k_gain, attn_w_o, rnn_w_in, rnn_conv_w, rnn_conv_b, rnn_w_a, rnn_b_a, rnn_w_i, rnn_b_i, rnn_lambda, rnn_w_out, ffn_w_gate, ffn_w_up, ffn_w_down, loss_target, m_norm_mix, m_norm_ffn, m_attn_w_qkv, m_attn_q_gain, m_attn_k_gain, m_attn_w_o, m_rnn_w_in, m_rnn_conv_w, m_rnn_conv_b, m_rnn_w_a, m_rnn_b_a, m_rnn_w_i, m_rnn_b_i, m_rnn_lambda, m_rnn_w_out, m_ffn_w_gate, m_ffn_w_up, m_ffn_w_down, v_norm_mix, v_norm_ffn, v_attn_w_qkv, v_attn_q_gain, v_attn_k_gain, v_attn_w_o, v_rnn_w_in, v_rnn_conv_w, v_rnn_conv_b, v_rnn_w_a, v_rnn_b_a, v_rnn_w_i, v_rnn_b_i, v_rnn_lambda, v_rnn_w_out, v_ffn_w_gate, v_ffn_w_up, v_ffn_w_down):
    given = dict(x=x, norm_mix=norm_mix, norm_ffn=norm_ffn, attn_w_qkv=attn_w_qkv, attn_q_gain=attn_q_gain, attn_k_gain=attn_k_gain, attn_w_o=attn_w_o, rnn_w_in=rnn_w_in, rnn_conv_w=rnn_conv_w, rnn_conv_b=rnn_conv_b, rnn_w_a=rnn_w_a, rnn_b_a=rnn_b_a, rnn_w_i=rnn_w_i, rnn_b_i=rnn_b_i, rnn_lambda=rnn_lambda, rnn_w_out=rnn_w_out, ffn_w_gate=ffn_w_gate, ffn_w_up=ffn_w_up, ffn_w_down=ffn_w_down, loss_target=loss_target, m_norm_mix=m_norm_mix, m_norm_ffn=m_norm_ffn, m_attn_w_qkv=m_attn_w_qkv, m_attn_q_gain=m_attn_q_gain, m_attn_k_gain=m_attn_k_gain, m_attn_w_o=m_attn_w_o, m_rnn_w_in=m_rnn_w_in, m_rnn_conv_w=m_rnn_conv_w, m_rnn_conv_b=m_rnn_conv_b, m_rnn_w_a=m_rnn_w_a, m_rnn_b_a=m_rnn_b_a, m_rnn_w_i=m_rnn_w_i, m_rnn_b_i=m_rnn_b_i, m_rnn_lambda=m_rnn_lambda, m_rnn_w_out=m_rnn_w_out, m_ffn_w_gate=m_ffn_w_gate, m_ffn_w_up=m_ffn_w_up, m_ffn_w_down=m_ffn_w_down, v_norm_mix=v_norm_mix, v_norm_ffn=v_norm_ffn, v_attn_w_qkv=v_attn_w_qkv, v_attn_q_gain=v_attn_q_gain, v_attn_k_gain=v_attn_k_gain, v_attn_w_o=v_attn_w_o, v_rnn_w_in=v_rnn_w_in, v_rnn_conv_w=v_rnn_conv_w, v_rnn_conv_b=v_rnn_conv_b, v_rnn_w_a=v_rnn_w_a, v_rnn_b_a=v_rnn_b_a, v_rnn_w_i=v_rnn_w_i, v_rnn_b_i=v_rnn_b_i, v_rnn_lambda=v_rnn_lambda, v_rnn_w_out=v_rnn_w_out, v_ffn_w_gate=v_ffn_w_gate, v_ffn_w_up=v_ffn_w_up, v_ffn_w_down=v_ffn_w_down)
    weights = {n: given[n] for n in TWIN_WEIGHTS}
    shared = {n: given[n] for n in SHARED_INPUTS}
    per_example = {n: given[n] for n in ['x']}
    grad_fn = _jax.value_and_grad(_loss, argnums=(0, 1))

    def one_microbatch(ex, loss_target):
        ex = dict(ex)
        diff = ex.pop(TWIN_DIFF_INPUT)
        return grad_fn(weights, diff, {**shared, **ex}, loss_target)

    if N_MICROBATCH == 1:
        loss, (grad_w, grad_x) = one_microbatch(per_example, given["loss_target"])
    else:
        def body(carry, xs):
            loss_sum, grad_sum = carry
            l_k, (gw_k, gx_k) = one_microbatch(xs[0], xs[1])
            with _jax.named_scope("update"):
                return (loss_sum + l_k, _jax.tree.map(_jnp.add, grad_sum, gw_k)), gx_k

        init = (_jnp.zeros((), _jnp.float32), _jax.tree.map(_jnp.zeros_like, weights))
        (loss, grad_w), grad_x = _jax.lax.scan(body, init, (per_example, given["loss_target"]))
    with _jax.named_scope("update"):
        delta_w, new_m, new_v = {}, {}, {}
        for n in TWIN_WEIGHTS:
            delta_w[n], new_m[n], new_v[n] = _adamw(weights[n], grad_w[n], given["m_" + n], given["v_" + n])
    return (loss, grad_x, *[grad_w[n] for n in TWIN_WEIGHTS], *[delta_w[n] for n in TWIN_WEIGHTS],
            *[new_m[n] for n in TWIN_WEIGHTS], *[new_v[n] for n in TWIN_WEIGHTS])
```

```python
import functools
import math

import jax
import jax.numpy as jnp
from jax import lax
from jax.experimental import pallas as pl
from jax.experimental.pallas import tpu as pltpu

F32 = jnp.float32
CD = jnp.bfloat16
NDEV = 8
HEAD_DIM = 128
GRID_W = 64
ROPE_THETA = 10000.0
ROPE_FREQS = 32
EPS = 1e-6
LRU_C = 8.0
SMALL_ROWS = 16
ADAM_LR, ADAM_B1, ADAM_B2, ADAM_EPS, ADAM_WD, ADAM_STEP = 0.001, 0.9, 0.999, 1e-08, 0.01, 10
V7X_VMEM_LIMIT = 56 * 1024 * 1024
BS = pl.BlockSpec
MESH = pl.DeviceIdType.MESH
NN = (((1,), (0,)), ((), ()))
NT = (((1,), (1,)), ((), ()))
TN = (((0,), (0,)), ((), ()))


def _params(*sem):
    return pltpu.CompilerParams(dimension_semantics=sem, vmem_limit_bytes=V7X_VMEM_LIMIT)


def _tile(n, want):
    return want if n % want == 0 else n


def _exchange(name, arrs, scatter):
    n = len(arrs)
    out_shape = [jax.ShapeDtypeStruct(a.shape if scatter else (NDEV,) + a.shape, a.dtype) for a in arrs]

    def body(*refs):
        ins, outs = refs[:n], refs[n:2 * n]
        send_sems, recv_sems, local_sems = refs[2 * n:]
        x, y, c = lax.axis_index("x"), lax.axis_index("y"), lax.axis_index("c")
        me = 4 * x + 2 * y + c
        local, sends, recvs = [], [], []
        for a in range(n):
            cp = pltpu.make_async_copy(ins[a].at[me] if scatter else ins[a], outs[a].at[me], local_sems.at[a])
            cp.start()
            local.append(cp)
        for k in range(1, NDEV):
            px = 1 - x if k & 4 else x
            py = 1 - y if k & 2 else y
            pc = 1 - c if k & 1 else c
            peer = 4 * px + 2 * py + pc
            for a in range(n):
                src = ins[a].at[peer] if scatter else ins[a]
                kw = dict(send_sem=send_sems.at[a, k - 1], recv_sem=recv_sems.at[a, k - 1],
                          device_id=(px, py, pc), device_id_type=MESH)
                cp = pltpu.make_async_remote_copy(src_ref=src, dst_ref=outs[a].at[me], **kw)
                cp.start()
                sends.append(cp)
                recvs.append(pltpu.make_async_remote_copy(src_ref=src, dst_ref=outs[a].at[peer], **kw))
        for cp in local:
            cp.wait()
        for cp in sends:
            cp.wait_send()
        for cp in recvs:
            cp.wait_recv()

    return pl.pallas_call(
        body, name=name, out_shape=out_shape,
        in_specs=[BS(memory_space=pl.ANY)] * n, out_specs=[BS(memory_space=pl.ANY)] * n,
        scratch_shapes=[pltpu.SemaphoreType.DMA((n, NDEV - 1)), pltpu.SemaphoreType.DMA((n, NDEV - 1)),
                        pltpu.SemaphoreType.DMA((n,))],
    )(*arrs)


def _mm(name, dims, a, b, a_spec, b_spec, o_spec, out_shape, out_dtype, grid, add=None):
    nr = grid[2]
    o_blk = tuple(d for d in o_spec.block_shape if d is not None)

    def body(*refs):
        a_ref, b_ref = refs[0], refs[1]
        add_ref = refs[2] if add is not None else None
        o_ref = refs[3] if add is not None else refs[2]
        part = lax.dot_general(a_ref[...].astype(CD), b_ref[...].astype(CD), dims, preferred_element_type=F32)

        def finish(v):
            if add_ref is not None:
                v = v + add_ref[...].astype(F32)
            o_ref[...] = v.astype(o_ref.dtype)

        if nr == 1:
            finish(part)
        else:
            acc = refs[-1]
            r = pl.program_id(2)

            @pl.when(r == 0)
            def _():
                acc[...] = part

            @pl.when(r > 0)
            def _():
                acc[...] += part

            @pl.when(r == nr - 1)
            def _():
                finish(acc[...])

    return pl.pallas_call(
        body, name=name, grid=grid, out_shape=jax.ShapeDtypeStruct(out_shape, out_dtype),
        in_specs=[a_spec, b_spec] + ([o_spec] if add is not None else []), out_specs=o_spec,
        scratch_shapes=[pltpu.VMEM(o_blk, F32)] if nr > 1 else [],
        compiler_params=_params("parallel", "parallel", "arbitrary"),
    )(*([a, b] + ([add] if add is not None else [])))


def _rms_fwd(name, x, g):
    S, D = x.shape
    tr = _tile(S, 512)

    def body(x_ref, g_ref, o_ref):
        xv = x_ref[...]
        r = lax.rsqrt(jnp.mean(xv * xv, axis=-1, keepdims=True) + EPS)
        o_ref[...] = (xv * r * g_ref[...]).astype(o_ref.dtype)

    return pl.pallas_call(
        body, name=name, grid=(S // tr,), out_shape=jax.ShapeDtypeStruct((S, D), CD),
        in_specs=[BS((tr, D), lambda i: (i, 0)), BS((1, D), lambda i: (0, 0))],
        out_specs=BS((tr, D), lambda i: (i, 0)), compiler_params=_params("parallel"),
    )(x, g)


def _rms_bwd(name, dh, x, g, dx_in):
    S, D = x.shape
    tr = _tile(S, 256)

    def body(dh_ref, x_ref, g_ref, dxin_ref, dx_ref, dxb_ref, dg_ref):
        xv = x_ref[...]
        dhv = dh_ref[...].astype(F32)
        r = lax.rsqrt(jnp.mean(xv * xv, axis=-1, keepdims=True) + EPS)
        yv = xv * r
        dy = dhv * g_ref[...]
        dx = dxin_ref[...] + r * (dy - yv * jnp.mean(dy * yv, axis=-1, keepdims=True))
        dx_ref[...] = dx
        dxb_ref[...] = dx.astype(CD)
        part = jnp.sum(dhv * yv, axis=0, keepdims=True)

        @pl.when(pl.program_id(0) == 0)
        def _():
            dg_ref[...] = part

        @pl.when(pl.program_id(0) > 0)
        def _():
            dg_ref[...] += part

    row = BS((tr, D), lambda i: (i, 0))
    vec = BS((1, D), lambda i: (0, 0))
    return pl.pallas_call(
        body, name=name, grid=(S // tr,),
        out_shape=[jax.ShapeDtypeStruct((S, D), F32), jax.ShapeDtypeStruct((S, D), CD),
                   jax.ShapeDtypeStruct((1, D), F32)],
        in_specs=[row, row, vec, row], out_specs=[row, row, vec], compiler_params=_params("arbitrary"),
    )(dh, x, g, dx_in)


def _loss_head(y, target):
    S, D = y.shape
    tr = _tile(S, 512)

    def body(y_ref, t_ref, dy_ref, dyb_ref, l_ref):
        err = y_ref[...] - t_ref[...]
        dy = err * (1.0 / D)
        dy_ref[...] = dy
        dyb_ref[...] = dy.astype(CD)
        part = jnp.full((8, 128), jnp.sum(err * err), F32)

        @pl.when(pl.program_id(0) == 0)
        def _():
            l_ref[...] = part

        @pl.when(pl.program_id(0) > 0)
        def _():
            l_ref[...] += part

    row = BS((tr, D), lambda i: (i, 0))
    return pl.pallas_call(
        body, name="loss_head", grid=(S // tr,),
        out_shape=[jax.ShapeDtypeStruct((S, D), F32), jax.ShapeDtypeStruct((S, D), CD),
                   jax.ShapeDtypeStruct((8, 128), F32)],
        in_specs=[row, row], out_specs=[row, row, BS((8, 128), lambda i: (0, 0))],
        compiler_params=_params("arbitrary"),
    )(y, target)


def _sigmoid(v):
    return 1.0 / (1.0 + jnp.exp(-v))


def _swiglu_fwd(gu):
    _, J, S, F8 = gu.shape
    tr = _tile(S, 1024)

    def body(gu_ref, o_ref):
        gv = gu_ref[0].astype(F32)
        uv = gu_ref[1].astype(F32)
        o_ref[...] = (gv * _sigmoid(gv) * uv).astype(o_ref.dtype)

    return pl.pallas_call(
        body, name="swiglu_fwd", grid=(J, S // tr), out_shape=jax.ShapeDtypeStruct((J, S, F8), CD),
        in_specs=[BS((2, None, tr, F8), lambda j, i: (0, j, i, 0))],
        out_specs=BS((None, tr, F8), lambda j, i: (j, i, 0)), compiler_params=_params("parallel", "parallel"),
    )(gu)


def _swiglu_bwd(da, gu):
    _, J, S, F8 = gu.shape
    tr = _tile(S, 1024)

    def body(da_ref, gu_ref, o_ref):
        gv = gu_ref[0].astype(F32)
        uv = gu_ref[1].astype(F32)
        dav = da_ref[...].astype(F32)
        sg = _sigmoid(gv)
        o_ref[0] = (dav * uv * sg * (1.0 + gv * (1.0 - sg))).astype(o_ref.dtype)
        o_ref[1] = (dav * gv * sg).astype(o_ref.dtype)

    pair = BS((2, None, tr, F8), lambda j, i: (0, j, i, 0))
    return pl.pallas_call(
        body, name="swiglu_bwd", grid=(J, S // tr), out_shape=jax.ShapeDtypeStruct(gu.shape, CD),
        in_specs=[BS((None, tr, F8), lambda j, i: (j, i, 0)), pair], out_specs=pair,
        compiler_params=_params("parallel", "parallel"),
    )(da, gu)


def _rope_tables(S):
    freqs = ROPE_THETA ** (-jnp.arange(ROPE_FREQS, dtype=F32) / ROPE_FREQS)
    pos = jnp.arange(S, dtype=jnp.int32)
    row = (pos // GRID_W).astype(F32)[:, None] * freqs
    col = (pos % GRID_W).astype(F32)[:, None] * freqs
    cr, sr, cc, sc = jnp.cos(row), jnp.sin(row), jnp.cos(col), jnp.sin(col)
    return jnp.concatenate([cr, cr, cc, cc], axis=1), jnp.concatenate([-sr, sr, -sc, sc], axis=1)


def _swap_halves(v):
    lane = lax.broadcasted_iota(jnp.int32, v.shape, 1)
    return jnp.where((lane & 32) == 0, pltpu.roll(v, 96, 1), pltpu.roll(v, 32, 1))


def _qk_prep_fwd(qkv, qg, kg, cos_t, sin_t, NQ, NKV):
    S, W = qkv.shape
    tr = _tile(S, 256)
    scale = HEAD_DIM ** -0.5

    def body(qkv_ref, qg_ref, kg_ref, c_ref, s_ref, q_out, k_out):
        C, Sg = c_ref[...], s_ref[...]

        def one(xh, gain, sc):
            xh = xh.astype(F32)
            r = lax.rsqrt(jnp.mean(xh * xh, axis=-1, keepdims=True) + EPS)
            nv = xh * r * gain
            return (nv * C + _swap_halves(nv) * Sg) * sc

        for h in range(NQ):
            sl = slice(h * HEAD_DIM, (h + 1) * HEAD_DIM)
            q_out[:, sl] = one(qkv_ref[:, sl], qg_ref[...], scale).astype(CD)
        for h in range(NKV):
            sl = slice((NQ + h) * HEAD_DIM, (NQ + h + 1) * HEAD_DIM)
            k_out[:, h * HEAD_DIM:(h + 1) * HEAD_DIM] = one(qkv_ref[:, sl], kg_ref[...], 1.0).astype(CD)

    vec = BS((1, HEAD_DIM), lambda i: (0, 0))
    tab = BS((tr, HEAD_DIM), lambda i: (i, 0))
    return pl.pallas_call(
        body, name="qk_prep_fwd", grid=(S // tr,),
        out_shape=[jax.ShapeDtypeStruct((S, NQ * HEAD_DIM), CD), jax.ShapeDtypeStruct((S, NKV * HEAD_DIM), CD)],
        in_specs=[BS((tr, W), lambda i: (i, 0)), vec, vec, tab, tab],
        out_specs=[BS((tr, NQ * HEAD_DIM), lambda i: (i, 0)), BS((tr, NKV * HEAD_DIM), lambda i: (i, 0))],
        compiler_params=_params("parallel"),
    )(qkv, qg, kg, cos_t, sin_t)


def _qk_prep_bwd(qkv, dq, dk, dv, qg, kg, cos_t, sin_t, NQ, NKV):
    S, W = qkv.shape
    tr = _tile(S, 256)
    scale = HEAD_DIM ** -0.5

    def body(qkv_ref, dq_ref, dk_ref, dv_ref, qg_ref, kg_ref, c_ref, s_ref, o_ref, dqg_ref, dkg_ref):
        C, Sg = c_ref[...], s_ref[...]

        def one(xh, dout, gain, sc):
            xh = xh.astype(F32)
            r = lax.rsqrt(jnp.mean(xh * xh, axis=-1, keepdims=True) + EPS)
            yv = xh * r
            dout = dout.astype(F32) * sc
            dn = dout * C + _swap_halves(dout * Sg)
            dy = dn * gain
            dx = r * (dy - yv * jnp.mean(dy * yv, axis=-1, keepdims=True))
            return dx, jnp.sum(dn * yv, axis=0, keepdims=True)

        dqg = jnp.zeros((1, HEAD_DIM), F32)
        dkg = jnp.zeros((1, HEAD_DIM), F32)
        for h in range(NQ):
            sl = slice(h * HEAD_DIM, (h + 1) * HEAD_DIM)
            dx, dg = one(qkv_ref[:, sl], dq_ref[:, sl], qg_ref[...], scale)
            o_ref[:, sl] = dx.astype(CD)
            dqg = dqg + dg
        for h in range(NKV):
            sl = slice((NQ + h) * HEAD_DIM, (NQ + h + 1) * HEAD_DIM)
            dx, dg = one(qkv_ref[:, sl], dk_ref[:, h * HEAD_DIM:(h + 1) * HEAD_DIM], kg_ref[...], 1.0)
            o_ref[:, sl] = dx.astype(CD)
            dkg = dkg + dg
        o_ref[:, (NQ + NKV) * HEAD_DIM:] = dv_ref[...].astype(CD)

        @pl.when(pl.program_id(0) == 0)
        def _():
            dqg_ref[...] = dqg
            dkg_ref[...] = dkg

        @pl.when(pl.program_id(0) > 0)
        def _():
            dqg_ref[...] += dqg
            dkg_ref[...] += dkg

    vec = BS((1, HEAD_DIM), lambda i: (0, 0))
    tab = BS((tr, HEAD_DIM), lambda i: (i, 0))
    full = BS((tr, W), lambda i: (i, 0))
    return pl.pallas_call(
        body, name="qk_prep_bwd", grid=(S // tr,),
        out_shape=[jax.ShapeDtypeStruct((S, W), CD), jax.ShapeDtypeStruct((1, HEAD_DIM), F32),
                   jax.ShapeDtypeStruct((1, HEAD_DIM), F32)],
        in_specs=[full, BS((tr, NQ * HEAD_DIM), lambda i: (i, 0)), BS((tr, NKV * HEAD_DIM), lambda i: (i, 0)),
                  BS((tr, NKV * HEAD_DIM), lambda i: (i, 0)), vec, vec, tab, tab],
        out_specs=[full, vec, vec], compiler_params=_params("arbitrary"),
    )(qkv, dq, dk, dv, qg, kg, cos_t, sin_t)


def _lane_pack(cols):
    rows = cols[0].shape[0]
    lane = lax.broadcasted_iota(jnp.int32, (rows, HEAD_DIM), 1)
    out = jnp.zeros((rows, HEAD_DIM), F32)
    for g, col in enumerate(cols):
        out = jnp.where(lane == g, col, out)
    return out


def _flash_fwd(qr, kr, qkv, NQ, NKV):
    S = qr.shape[0]
    G = NQ // NKV
    tq, tk = _tile(S, 512), _tile(S, 512)
    nk = S // tk
    GW = G * HEAD_DIM

    def body(q_ref, k_ref, v_ref, o_ref, lse_ref, m_s, l_s, acc_s):
        ki = pl.program_id(2)

        @pl.when(ki == 0)
        def _():
            m_s[...] = jnp.full(m_s.shape, -1e30, F32)
            l_s[...] = jnp.zeros(l_s.shape, F32)
            acc_s[...] = jnp.zeros(acc_s.shape, F32)

        kv, vv = k_ref[...], v_ref[...]
        for g in range(G):
            s = lax.dot_general(q_ref[:, g * HEAD_DIM:(g + 1) * HEAD_DIM], kv, NT, preferred_element_type=F32)
            m_prev = m_s[g]
            m_new = jnp.maximum(m_prev, jnp.max(s, axis=-1, keepdims=True))
            alpha = jnp.exp(m_prev - m_new)
            p = jnp.exp(s - m_new)
            l_s[g] = alpha * l_s[g] + jnp.sum(p, axis=-1, keepdims=True)
            acc_s[g] = alpha * acc_s[g] + jnp.dot(p.astype(CD), vv, preferred_element_type=F32)
            m_s[g] = m_new

        @pl.when(ki == nk - 1)
        def _():
            for g in range(G):
                o_ref[:, g * HEAD_DIM:(g + 1) * HEAD_DIM] = (acc_s[g] / l_s[g]).astype(o_ref.dtype)
            lse_ref[...] = _lane_pack([m_s[g] + jnp.log(l_s[g]) for g in range(G)])

    return pl.pallas_call(
        body, name="flash_fwd", grid=(NKV, S // tq, nk),
        out_shape=[jax.ShapeDtypeStruct((S, NQ * HEAD_DIM), CD), jax.ShapeDtypeStruct((NKV, S, HEAD_DIM), F32)],
        in_specs=[BS((tq, GW), lambda h, i, k: (i, h)), BS((tk, HEAD_DIM), lambda h, i, k: (k, h)),
                  BS((tk, HEAD_DIM), lambda h, i, k: (k, NQ + NKV + h))],
        out_specs=[BS((tq, GW), lambda h, i, k: (i, h)), BS((None, tq, HEAD_DIM), lambda h, i, k: (h, i, 0))],
        scratch_shapes=[pltpu.VMEM((G, tq, 1), F32), pltpu.VMEM((G, tq, 1), F32), pltpu.VMEM((G, tq, HEAD_DIM), F32)],
        compiler_params=_params("parallel", "parallel", "arbitrary"),
    )(qr, kr, qkv)


def _attn_delta(do, o, NQ, NKV):
    S = o.shape[0]
    G = NQ // NKV
    tr = _tile(S, 512)

    def body(do_ref, o_ref, d_ref):
        for kv in range(NKV):
            cols = []
            for g in range(G):
                sl = slice((kv * G + g) * HEAD_DIM, (kv * G + g + 1) * HEAD_DIM)
                cols.append(jnp.sum(do_ref[:, sl].astype(F32) * o_ref[:, sl].astype(F32), axis=-1, keepdims=True))
            d_ref[kv] = _lane_pack(cols)

    row = BS((tr, NQ * HEAD_DIM), lambda i: (i, 0))
    return pl.pallas_call(
        body, name="attn_delta", grid=(S // tr,), out_shape=jax.ShapeDtypeStruct((NKV, S, HEAD_DIM), F32),
        in_specs=[row, row], out_specs=BS((NKV, tr, HEAD_DIM), lambda i: (0, i, 0)),
        compiler_params=_params("parallel"),
    )(do, o)


def _flash_dq(qr, kr, qkv, do, lse, delta, NQ, NKV):
    S = qr.shape[0]
    G = NQ // NKV
    tq, tk = _tile(S, 512), _tile(S, 512)
    nk = S // tk
    GW = G * HEAD_DIM

    def body(q_ref, k_ref, v_ref, do_ref, lse_ref, dl_ref, dq_ref, acc_s):
        ki = pl.program_id(2)

        @pl.when(ki == 0)
        def _():
            acc_s[...] = jnp.zeros(acc_s.shape, F32)

        kv, vv = k_ref[...], v_ref[...]
        for g in range(G):
            sl = slice(g * HEAD_DIM, (g + 1) * HEAD_DIM)
            s = lax.dot_general(q_ref[:, sl], kv, NT, preferred_element_type=F32)
            p = jnp.exp(s - lse_ref[:, g:g + 1])
            dp = lax.dot_general(do_ref[:, sl], vv, NT, preferred_element_type=F32)
            ds = p * (dp - dl_ref[:, g:g + 1])
            acc_s[g] += jnp.dot(ds.astype(CD), kv, preferred_element_type=F32)

        @pl.when(ki == nk - 1)
        def _():
            for g in range(G):
                dq_ref[:, g * HEAD_DIM:(g + 1) * HEAD_DIM] = acc_s[g].astype(dq_ref.dtype)

    qb = BS((tq, GW), lambda h, i, k: (i, h))
    st = BS((None, tq, HEAD_DIM), lambda h, i, k: (h, i, 0))
    return pl.pallas_call(
        body, name="flash_dq", grid=(NKV, S // tq, nk), out_shape=jax.ShapeDtypeStruct((S, NQ * HEAD_DIM), CD),
        in_specs=[qb, BS((tk, HEAD_DIM), lambda h, i, k: (k, h)), BS((tk, HEAD_DIM), lambda h, i, k: (k, NQ + NKV + h)),
                  qb, st, st],
        out_specs=qb, scratch_shapes=[pltpu.VMEM((G, tq, HEAD_DIM), F32)],
        compiler_params=_params("parallel", "parallel", "arbitrary"),
    )(qr, kr, qkv, do, lse, delta)


def _flash_dkv(qr, kr, qkv, do, lse_rows, delta_rows, NQ, NKV):
    S = qr.shape[0]
    G = NQ // NKV
    tq, tk = _tile(S, 512), _tile(S, 512)
    nq = S // tq
    GW = G * HEAD_DIM

    def body(q_ref, k_ref, v_ref, do_ref, lse_ref, dl_ref, dk_ref, dv_ref, dk_s, dv_s):
        qi = pl.program_id(2)

        @pl.when(qi == 0)
        def _():
            dk_s[...] = jnp.zeros(dk_s.shape, F32)
            dv_s[...] = jnp.zeros(dv_s.shape, F32)

        kv, vv = k_ref[...], v_ref[...]
        for g in range(G):
            sl = slice(g * HEAD_DIM, (g + 1) * HEAD_DIM)
            qg, dog = q_ref[:, sl], do_ref[:, sl]
            st = lax.dot_general(kv, qg, NT, preferred_element_type=F32)
            pt = jnp.exp(st - lse_ref[g:g + 1, :])
            dv_s[...] += jnp.dot(pt.astype(CD), dog, preferred_element_type=F32)
            dpt = lax.dot_general(vv, dog, NT, preferred_element_type=F32)
            dst = pt * (dpt - dl_ref[g:g + 1, :])
            dk_s[...] += jnp.dot(dst.astype(CD), qg, preferred_element_type=F32)

        @pl.when(qi == nq - 1)
        def _():
            dk_ref[...] = dk_s[...].astype(dk_ref.dtype)
            dv_ref[...] = dv_s[...].astype(dv_ref.dtype)

    qb = BS((tq, GW), lambda h, k, i: (i, h))
    st = BS((None, G, tq), lambda h, k, i: (h, 0, i))
    kb = BS((tk, HEAD_DIM), lambda h, k, i: (k, h))
    return pl.pallas_call(
        body, name="flash_dkv", grid=(NKV, S // tk, nq),
        out_shape=[jax.ShapeDtypeStruct((S, NKV * HEAD_DIM), CD), jax.ShapeDtypeStruct((S, NKV * HEAD_DIM), CD)],
        in_specs=[qb, kb, BS((tk, HEAD_DIM), lambda h, k, i: (k, NQ + NKV + h)), qb, st, st],
        out_specs=[kb, kb], scratch_shapes=[pltpu.VMEM((tk, HEAD_DIM), F32), pltpu.VMEM((tk, HEAD_DIM), F32)],
        compiler_params=_params("parallel", "parallel", "arbitrary"),
    )(qr, kr, qkv, do, lse_rows, delta_rows)


HALO8 = 8


def _round(v):
    return v.astype(CD).astype(F32)


def _shift_down(v, k, above):
    T = v.shape[0]
    row = lax.broadcasted_iota(jnp.int32, v.shape, 0)
    out = pltpu.roll(v, k, 0)
    for j in range(k):
        out = jnp.where(row == j, above[above.shape[0] - k + j:above.shape[0] - k + j + 1, :], out)
    return out


def _shift_up(v, k, below):
    T = v.shape[0]
    row = lax.broadcasted_iota(jnp.int32, v.shape, 0)
    out = pltpu.roll(v, T - k, 0)
    for j in range(k):
        out = jnp.where(row == T - k + j, below[j:j + 1, :], out)
    return out


def _halo_specs(T, C, halo, n_chunks, S, col=0, rev=False):
    per = T // halo
    last = S // halo - 1
    ch = (lambda i: n_chunks - 1 - i) if rev else (lambda i: i)
    return (BS((T, C), lambda i: (ch(i), col)),
            BS((halo, C), lambda i: (jnp.maximum(ch(i) * per - 1, 0), col)),
            BS((halo, C), lambda i: (jnp.minimum((ch(i) + 1) * per, last), col)))


def _conv_fwd(xy, w8, bias):
    S, C2 = xy.shape
    C = C2 // 2
    T = _tile(S, 256)
    n = S // T

    def body(x_ref, p_ref, q_ref, w_ref, b_ref, o_ref):
        i = pl.program_id(0)
        xv = _round(x_ref[...])
        above = jnp.where(i > 0, _round(p_ref[...]), 0.0)
        below = jnp.where(i < n - 1, _round(q_ref[...]), 0.0)
        wv = _round(w_ref[...])
        o_ref[...] = (wv[0:1, :] * _shift_down(xv, 2, above) + wv[1:2, :] * _shift_down(xv, 1, above)
                      + wv[2:3, :] * xv + wv[3:4, :] * _shift_up(xv, 1, below) + b_ref[...])

    main, prev, nxt = _halo_specs(T, C, HALO8, n, S)
    return pl.pallas_call(
        body, name="conv_fwd", grid=(n,), out_shape=jax.ShapeDtypeStruct((S, C), F32),
        in_specs=[main, prev, nxt, BS((8, C), lambda i: (0, 0)), BS((1, C), lambda i: (0, 0))],
        out_specs=BS((T, C), lambda i: (i, 0)), compiler_params=_params("parallel"),
    )(xy, xy, xy, w8, bias)


def _tile_scan(a, b, reverse):
    row = lax.broadcasted_iota(jnp.int32, a.shape, 0)
    for sh in (1, 2, 4):
        if reverse:
            a_s, b_s, ok = pltpu.roll(a, 8 - sh, 0), pltpu.roll(b, 8 - sh, 0), row < 8 - sh
        else:
            a_s, b_s, ok = pltpu.roll(a, sh, 0), pltpu.roll(b, sh, 0), row >= sh
        b = jnp.where(ok, a * b_s + b, b)
        a = jnp.where(ok, a * a_s, a)
    return a, b


def _scan_chunk(a_ref, h_ref, carry_ref, reverse):
    T = h_ref.shape[0]
    nt = T // 8

    def step(j, carry):
        t = (nt - 1 - j) if reverse else j
        rows = pl.ds(pl.multiple_of(t * 8, 8), 8)
        ca, hb = _tile_scan(a_ref[rows, :], h_ref[rows, :], reverse)
        hv = hb + ca * carry
        h_ref[rows, :] = hv
        edge = hv[0:1, :] if reverse else hv[7:8, :]
        return jnp.broadcast_to(edge, hv.shape)

    carry_ref[...] = lax.fori_loop(0, nt, step, carry_ref[...])


def _softplus(z):
    return jnp.maximum(z, 0.0) + jnp.log(1.0 + jnp.exp(-jnp.abs(z)))


def _neg_expm1(z):
    return jnp.where(z > -0.01, -z * (1.0 + z * (0.5 + z * (1.0 / 6.0 + z * (1.0 / 24.0)))), 1.0 - jnp.exp(z))


def _gates(xs, wa, wi, ba, bi, sp):
    r = _sigmoid(jnp.dot(xs, wa, preferred_element_type=F32) + ba)
    ig = _sigmoid(jnp.dot(xs, wi, preferred_element_type=F32) + bi)
    la = -LRU_C * r * sp
    return r, ig, jnp.exp(la), jnp.sqrt(_neg_expm1(2.0 * la))


def _lru_fwd(name, xc, wai, d, ba, bi, lam, reverse):
    S, C = xc.shape
    NB, BW = wai.shape[2], wai.shape[3]
    T = _tile(S, 256)
    n = S // T
    ch = (lambda i: n - 1 - i) if reverse else (lambda i: i)

    def body(x_ref, wa_ref, wi_ref, ba_ref, bi_ref, lam_ref, h_ref, a_ref, carry):
        @pl.when(pl.program_id(0) == 0)
        def _():
            carry[...] = jnp.zeros(carry.shape, F32)

        for b in range(NB):
            sl = slice(b * BW, (b + 1) * BW)
            xf = x_ref[:, sl]
            _, ig, av, sq = _gates(xf.astype(CD), wa_ref[b], wi_ref[b], ba_ref[:, sl], bi_ref[:, sl],
                                   _softplus(-lam_ref[:, sl]))
            a_ref[:, sl] = av
            h_ref[:, sl] = sq * ig * xf
        _scan_chunk(a_ref, h_ref, carry, reverse)

    vec = BS((1, C), lambda i: (0, 0))
    blk = BS((T, C), lambda i: (ch(i), 0))
    return pl.pallas_call(
        body, name=name, grid=(n,),
        out_shape=[jax.ShapeDtypeStruct((S, C), F32), jax.ShapeDtypeStruct((S, C), F32)],
        in_specs=[blk, BS((None, None, NB, BW, BW), lambda i: (0, d, 0, 0, 0)),
                  BS((None, None, NB, BW, BW), lambda i: (1, d, 0, 0, 0)), vec, vec, vec],
        out_specs=[blk, blk], scratch_shapes=[pltpu.VMEM((8, C), F32)], compiler_params=_params("arbitrary"),
    )(xc, wai, wai, ba, bi, lam)


_GELU_C = math.sqrt(2.0 / math.pi)


def _rnn_out_fwd(h0, h1, xy):
    S, C = h0.shape
    tr = _tile(S, 256)

    def body(h0_ref, h1_ref, y_ref, o_ref):
        yv = y_ref[...].astype(F32)
        gl = 0.5 * yv * (1.0 + jnp.tanh(_GELU_C * (yv + 0.044715 * yv * yv * yv)))
        o_ref[...] = ((h0_ref[...] + h1_ref[...]) * gl).astype(o_ref.dtype)

    row = BS((tr, C), lambda i: (i, 0))
    return pl.pallas_call(
        body, name="rnn_out_fwd", grid=(S // tr,), out_shape=jax.ShapeDtypeStruct((S, C), CD),
        in_specs=[row, row, BS((tr, C), lambda i: (i, 1))], out_specs=row, compiler_params=_params("parallel"),
    )(h0, h1, xy)


def _rnn_out_bwd(dout, h0, h1, xy):
    S, C = h0.shape
    tr = _tile(S, 256)

    def body(d_ref, h0_ref, h1_ref, y_ref, dh_ref, dy_ref):
        yv = y_ref[...].astype(F32)
        dv = d_ref[...].astype(F32)
        t = jnp.tanh(_GELU_C * (yv + 0.044715 * yv * yv * yv))
        gl = 0.5 * yv * (1.0 + t)
        dgl = 0.5 * (1.0 + t) + 0.5 * yv * (1.0 - t * t) * _GELU_C * (1.0 + 3.0 * 0.044715 * yv * yv)
        dh_ref[...] = dv * gl
        dy_ref[...] = (dv * (h0_ref[...] + h1_ref[...]) * dgl).astype(dy_ref.dtype)

    row = BS((tr, C), lambda i: (i, 0))
    return pl.pallas_call(
        body, name="rnn_out_bwd", grid=(S // tr,),
        out_shape=[jax.ShapeDtypeStruct((S, C), F32), jax.ShapeDtypeStruct((S, C), CD)],
        in_specs=[row, row, row, BS((tr, C), lambda i: (i, 1))], out_specs=[row, row],
        compiler_params=_params("parallel"),
    )(dout, h0, h1, xy)


def _lru_bwd(name, dhs, a, h, xc, wai, d, ba, bi, lam, reverse, dxc_in):
    S, C = xc.shape
    NB, BW = wai.shape[2], wai.shape[3]
    T = _tile(S, 256)
    n = S // T
    back = not reverse

    def body(*refs):
        (dh_ref, a_ref, a_prev, a_next, h_ref, h_prev, h_next, x_ref, wa_ref, wi_ref, ba_ref, bi_ref,
         lam_ref) = refs[:13]
        k = 13
        add_ref = None
        if dxc_in is not None:
            add_ref = refs[k]
            k += 1
        dx_ref, dwa_ref, dwi_ref, dba_ref, dbi_ref, dlam_ref, coef_s, lam_s, carry, dsp_s = refs[k:]
        i = pl.program_id(0)
        first_chunk = i == (n - 1 if back else 0)
        last_chunk = i == (0 if back else n - 1)

        @pl.when(i == 0)
        def _():
            carry[...] = jnp.zeros(carry.shape, F32)
            dsp_s[...] = jnp.zeros(dsp_s.shape, F32)
            dwa_ref[...] = jnp.zeros(dwa_ref.shape, F32)
            dwi_ref[...] = jnp.zeros(dwi_ref.shape, F32)
            dba_ref[...] = jnp.zeros(dba_ref.shape, F32)
            dbi_ref[...] = jnp.zeros(dbi_ref.shape, F32)

        av, hv = a_ref[...], h_ref[...]
        if reverse:
            coef_s[...] = _shift_down(av, 1, a_prev[...])
            h_nb = _shift_up(hv, 1, jnp.where(last_chunk, 0.0, h_next[...]))
        else:
            coef_s[...] = _shift_up(av, 1, a_next[...])
            h_nb = _shift_down(hv, 1, jnp.where(first_chunk, 0.0, h_prev[...]))
        lam_s[...] = dh_ref[...]
        _scan_chunk(coef_s, lam_s, carry, back)

        for b in range(NB):
            sl = slice(b * BW, (b + 1) * BW)
            xf = x_ref[:, sl]
            xs = xf.astype(CD)
            sp = _softplus(-lam_ref[:, sl])
            r, ig, ab, sq = _gates(xs, wa_ref[b], wi_ref[b], ba_ref[:, sl], bi_ref[:, sl], sp)
            du = lam_s[:, sl]
            da = du * h_nb[:, sl]
            d_ig = du * sq * xf
            d_sq = du * ig * xf
            d_la = da * ab - d_sq * (ab * ab) / sq
            d_pa = d_la * (-LRU_C * sp) * r * (1.0 - r)
            d_pi = d_ig * ig * (1.0 - ig)
            dsp_s[:, sl] += jnp.sum(d_la * (-LRU_C * r), axis=0, keepdims=True)
            dba_ref[:, sl] += jnp.sum(d_pa, axis=0, keepdims=True)
            dbi_ref[:, sl] += jnp.sum(d_pi, axis=0, keepdims=True)
            d_pa_c, d_pi_c = d_pa.astype(CD), d_pi.astype(CD)
            dwa_ref[b] += lax.dot_general(xs, d_pa_c, TN, preferred_element_type=F32)
            dwi_ref[b] += lax.dot_general(xs, d_pi_c, TN, preferred_element_type=F32)
            dx = (du * sq * ig + lax.dot_general(d_pa_c, wa_ref[b], NT, preferred_element_type=F32)
                  + lax.dot_general(d_pi_c, wi_ref[b], NT, preferred_element_type=F32))
            if add_ref is not None:
                dx = dx + add_ref[:, sl]
            dx_ref[:, sl] = dx

        @pl.when(i == n - 1)
        def _():
            dlam_ref[...] = -dsp_s[...] * _sigmoid(-lam_ref[...])

    vec = BS((1, C), lambda i: (0, 0))
    main, prev, nxt = _halo_specs(T, C, HALO8, n, S, rev=back)
    wspec = lambda which: BS((None, None, NB, BW, BW), lambda i: (which, d, 0, 0, 0))
    acc = BS((NB, BW, BW), lambda i: (0, 0, 0))
    ins = [dhs, a, a, a, h, h, h, xc, wai, wai, ba, bi, lam]
    in_specs = [main, main, prev, nxt, main, prev, nxt, main, wspec(0), wspec(1), vec, vec, vec]
    if dxc_in is not None:
        ins.append(dxc_in)
        in_specs.append(main)
    return pl.pallas_call(
        body, name=name, grid=(n,),
        out_shape=[jax.ShapeDtypeStruct((S, C), F32), jax.ShapeDtypeStruct((NB, BW, BW), F32),
                   jax.ShapeDtypeStruct((NB, BW, BW), F32), jax.ShapeDtypeStruct((1, C), F32),
                   jax.ShapeDtypeStruct((1, C), F32), jax.ShapeDtypeStruct((1, C), F32)],
        in_specs=in_specs, out_specs=[main, acc, acc, vec, vec, vec],
        scratch_shapes=[pltpu.VMEM((T, C), F32), pltpu.VMEM((T, C), F32), pltpu.VMEM((8, C), F32),
                        pltpu.VMEM((1, C), F32)],
        compiler_params=_params("arbitrary"),
    )(*ins)


def _conv_bwd(dxc, xy, w8):
    S, C = dxc.shape
    T = _tile(S, 256)
    n = S // T

    def body(d_ref, dp_ref, dn_ref, x_ref, xp_ref, xn_ref, w_ref, dx_ref, dw_ref):
        i = pl.program_id(0)
        dv = _round(d_ref[...])
        d_above = jnp.where(i > 0, _round(dp_ref[...]), 0.0)
        d_below = jnp.where(i < n - 1, _round(dn_ref[...]), 0.0)
        wv = _round(w_ref[...])
        dx = (wv[0:1, :] * _shift_up(dv, 2, d_below) + wv[1:2, :] * _shift_up(dv, 1, d_below)
              + wv[2:3, :] * dv + wv[3:4, :] * _shift_down(dv, 1, d_above))
        dx_ref[...] = dx.astype(dx_ref.dtype)
        xv = _round(x_ref[...])
        above = jnp.where(i > 0, _round(xp_ref[...]), 0.0)
        below = jnp.where(i < n - 1, _round(xn_ref[...]), 0.0)
        rows = [jnp.sum(dv * _shift_down(xv, 2, above), axis=0, keepdims=True),
                jnp.sum(dv * _shift_down(xv, 1, above), axis=0, keepdims=True),
                jnp.sum(dv * xv, axis=0, keepdims=True),
                jnp.sum(dv * _shift_up(xv, 1, below), axis=0, keepdims=True),
                jnp.sum(dv, axis=0, keepdims=True)]
        rid = lax.broadcasted_iota(jnp.int32, (8, C), 0)
        part = jnp.zeros((8, C), F32)
        for j, rv in enumerate(rows):
            part = jnp.where(rid == j, rv, part)

        @pl.when(i == 0)
        def _():
            dw_ref[...] = part

        @pl.when(i > 0)
        def _():
            dw_ref[...] += part

    dmain, dprev, dnext = _halo_specs(T, C, HALO8, n, S)
    xmain, xprev, xnext = _halo_specs(T, C, HALO8, n, S)
    w = BS((8, C), lambda i: (0, 0))
    return pl.pallas_call(
        body, name="conv_bwd", grid=(n,),
        out_shape=[jax.ShapeDtypeStruct((S, C), CD), jax.ShapeDtypeStruct((8, C), F32)],
        in_specs=[dmain, dprev, dnext, xmain, xprev, xnext, w], out_specs=[dmain, w],
        compiler_params=_params("arbitrary"),
    )(dxc, dxc, dxc, xy, xy, xy, w8)


def _adamw(name, land, land_spec, w, m, v, spec, grid):
    c1 = 1.0 - ADAM_B1 ** ADAM_STEP
    c2 = 1.0 - ADAM_B2 ** ADAM_STEP

    def body(l_ref, w_ref, m_ref, v_ref, g_out, d_out, m_out, v_out):
        g = l_ref[0].astype(F32)
        for s in range(1, NDEV):
            g = g + l_ref[s].astype(F32)
        mn = ADAM_B1 * m_ref[...] + (1.0 - ADAM_B1) * g
        vn = ADAM_B2 * v_ref[...] + (1.0 - ADAM_B2) * (g * g)
        g_out[...] = g
        m_out[...] = mn
        v_out[...] = vn
        d_out[...] = -ADAM_LR * ((mn / c1) / (jnp.sqrt(vn / c2) + ADAM_EPS) + ADAM_WD * w_ref[...])

    sds = jax.ShapeDtypeStruct(w.shape, F32)
    return pl.pallas_call(
        body, name=name, grid=grid, out_shape=[sds] * 4, in_specs=[land_spec, spec, spec, spec],
        out_specs=[spec] * 4, compiler_params=_params(*(["parallel"] * len(grid))),
    )(land, w, m, v)


def _adamw_rows(name, land, w, m, v):
    C = w.shape[-1]
    w2, m2, v2 = (t.reshape(-1, C) for t in (w, m, v))
    R = w2.shape[0]
    tr = R
    for cand in (1024, 512, 256, 128, 64, 32, 16, 8):
        if R % cand == 0 and cand * C * 4 <= (1 << 20):
            tr = cand
            break
    outs = _adamw(name, land.reshape(NDEV, R, C), BS((NDEV, tr, C), lambda i: (0, i, 0)), w2, m2, v2,
                  BS((tr, C), lambda i: (i, 0)), (R // tr,))
    return [o.reshape(w.shape) for o in outs]


def kernel(x, norm_mix, norm_ffn, attn_w_qkv, attn_q_gain, attn_k_gain, attn_w_o, rnn_w_in, rnn_conv_w, rnn_conv_b, rnn_w_a, rnn_b_a, rnn_w_i, rnn_b_i, rnn_lambda, rnn_w_out, ffn_w_gate, ffn_w_up, ffn_w_down, loss_target, m_norm_mix, m_norm_ffn, m_attn_w_qkv, m_attn_q_gain, m_attn_k_gain, m_attn_w_o, m_rnn_w_in, m_rnn_conv_w, m_rnn_conv_b, m_rnn_w_a, m_rnn_b_a, m_rnn_w_i, m_rnn_b_i, m_rnn_lambda, m_rnn_w_out, m_ffn_w_gate, m_ffn_w_up, m_ffn_w_down, v_norm_mix, v_norm_ffn, v_attn_w_qkv, v_attn_q_gain, v_attn_k_gain, v_attn_w_o, v_rnn_w_in, v_rnn_conv_w, v_rnn_conv_b, v_rnn_w_a, v_rnn_b_a, v_rnn_w_i, v_rnn_b_i, v_rnn_lambda, v_rnn_w_out, v_ffn_w_gate, v_ffn_w_up, v_ffn_w_down):
    S, D = x.shape[1], x.shape[2]
    depth = norm_mix.shape[0]
    n_attn, n_rnn = attn_w_qkv.shape[0], rnn_w_in.shape[0]
    QS = attn_w_qkv.shape[2]
    NQ = D // HEAD_DIM
    NKV = (QS * NDEV // HEAD_DIM - NQ) // 2
    G = NQ // NKV
    OS = attn_w_o.shape[1]
    IS = rnn_w_in.shape[2]
    C = rnn_w_out.shape[1] * NDEV
    CS = C // NDEV
    NB, BWS, BW = rnn_w_a.shape[2], rnn_w_a.shape[3], rnn_w_a.shape[4]
    F8 = ffn_w_gate.shape[2]
    tm = _tile(S, 1024)
    tn = _tile(D, 1024)
    tk = _tile(S, 1024)

    def pack_small(cw, cb, b_a, b_i, lam):
        L = cw.shape[0]
        parts = [cw, cb[:, None, :], b_a, b_i, lam]
        used = sum(p.shape[1] for p in parts)
        return jnp.concatenate(parts + [jnp.zeros((L, SMALL_ROWS - used, cw.shape[2]), F32)], axis=1)

    def unpack_small(p):
        return p[:, 0:4], p[:, 4], p[:, 5:7], p[:, 7:9], p[:, 9:11]

    gu_sh = jnp.stack([ffn_w_gate, ffn_w_up], axis=1).astype(CD)
    ai_sh = jnp.stack([rnn_w_a, rnn_w_i], axis=1).astype(CD)
    small_sh = pack_small(rnn_conv_w, rnn_conv_b, rnn_b_a, rnn_b_i, rnn_lambda)
    Wqkv, Wo, Win, Wout, Wai, Wgu, Wd, small = _exchange(
        "gather_weights",
        [attn_w_qkv.astype(CD), attn_w_o.astype(CD), rnn_w_in.astype(CD), rnn_w_out.astype(CD), ai_sh, gu_sh,
         ffn_w_down.astype(CD), small_sh], scatter=False)
    Wai = jnp.transpose(Wai, (1, 2, 3, 4, 0, 5, 6)).reshape(n_rnn, 2, 2, NB, BW, BW)
    small = jnp.transpose(small, (1, 2, 0, 3)).reshape(n_rnn, SMALL_ROWS, C)
    conv_w_f, conv_b_f, b_a_f, b_i_f, lam_f = unpack_small(small)
    cos_t, sin_t = _rope_tables(S)

    def proj_cols(name, a, W, l, width, dtype):
        return _mm(name, NN, a, W, BS((tm, D), lambda i, j, r: (i, 0)),
                   BS((None, None, D, width), lambda i, j, r: (j, l, 0, 0)),
                   BS((tm, width), lambda i, j, r: (i, j)), (S, NDEV * width), dtype, (S // tm, NDEV, 1))

    def proj_cols_dx(name, dy, W, l, width):
        return _mm(name, NT, dy, W, BS((tm, width), lambda i, j, r: (i, r)),
                   BS((None, None, tn, width), lambda i, j, r: (r, l, j, 0)),
                   BS((tm, tn), lambda i, j, r: (i, j)), (S, D), F32, (S // tm, D // tn, NDEV))

    def proj_cols_dw(name, a, dy, width):
        return _mm(name, TN, a, dy, BS((tk, tn), lambda i, j, r: (r, i)),
                   BS((tk, width), lambda i, j, r: (r, j)),
                   BS((None, tn, width), lambda i, j, r: (j, i, 0)), (NDEV, D, width), CD, (D // tn, NDEV, S // tk))

    def proj_rows(name, a, W, l, rows, resid):
        return _mm(name, NN, a, W, BS((tm, rows), lambda i, j, r: (i, r)),
                   BS((None, None, rows, tn), lambda i, j, r: (r, l, 0, j)),
                   BS((tm, tn), lambda i, j, r: (i, j)), (S, D), F32, (S // tm, D // tn, NDEV), add=resid)

    def proj_rows_dx(name, dy, W, l, rows, dtype):
        return _mm(name, NT, dy, W, BS((tm, D), lambda i, j, r: (i, 0)),
                   BS((None, None, rows, D), lambda i, j, r: (j, l, 0, 0)),
                   BS((tm, rows), lambda i, j, r: (i, j)), (S, NDEV * rows), dtype, (S // tm, NDEV, 1))

    def proj_rows_dw(name, a, dy, rows):
        return _mm(name, TN, a, dy, BS((tk, rows), lambda i, j, r: (r, i)),
                   BS((tk, tn), lambda i, j, r: (r, j)),
                   BS((None, rows, tn), lambda i, j, r: (i, 0, j)), (NDEV, rows, D), CD, (NDEV, D // tn, S // tk))

    saved = []
    xs = x[0]
    for i in range(depth):
        l = i // 2
        h1 = _rms_fwd("norm_mix_fwd", xs, norm_mix[i:i + 1])
        if i % 2 == 0:
            qkv = proj_cols("attn_qkv", h1, Wqkv, l, QS, CD)
            qr, kr = _qk_prep_fwd(qkv, attn_q_gain[l:l + 1], attn_k_gain[l:l + 1], cos_t, sin_t, NQ, NKV)
            o, lse = _flash_fwd(qr, kr, qkv, NQ, NKV)
            x1 = proj_rows("attn_out", o, Wo, l, OS, xs)
            mix = (qkv, qr, kr, o, lse)
        else:
            xy = proj_cols("rnn_in", h1, Win, l, IS, F32)
            w8 = jnp.concatenate([conv_w_f[l], jnp.zeros((4, C), F32)], axis=0)
            xc = _conv_fwd(xy, w8, conv_b_f[l][None])
            hs, avs = [], []
            for d in range(2):
                hd, ad = _lru_fwd("lru_fwd_rev" if d else "lru_fwd", xc, Wai[l], d, b_a_f[l, d][None], b_i_f[l, d][None],
                                  lam_f[l, d][None], bool(d))
                hs.append(hd)
                avs.append(ad)
            ro = _rnn_out_fwd(hs[0], hs[1], xy)
            x1 = proj_rows("rnn_out", ro, Wout, l, CS, xs)
            mix = (xy, w8, xc, hs, avs, ro)
        h2 = _rms_fwd("norm_ffn_fwd", x1, norm_ffn[i:i + 1])
        gu = _mm("ffn_gate_up", NN, h2, Wgu, BS((tm, D), lambda i_, j, r: (i_, 0)),
                 BS((None, None, None, D, F8), lambda i_, j, r, i=i: (j % NDEV, i, j // NDEV, 0, 0)),
                 BS((None, None, tm, F8), lambda i_, j, r: (j // NDEV, j % NDEV, i_, 0)),
                 (2, NDEV, S, F8), CD, (S // tm, 2 * NDEV, 1))
        act = _swiglu_fwd(gu)
        x2 = _mm("ffn_down", NN, act, Wd, BS((None, tm, F8), lambda i_, j, r: (r, i_, 0)),
                 BS((None, None, F8, tn), lambda i_, j, r, i=i: (r, i, 0, j)),
                 BS((tm, tn), lambda i_, j, r: (i_, j)), (S, D), F32, (S // tm, D // tn, NDEV), add=x1)
        saved.append((xs, h1, mix, x1, h2, gu, act))
        xs = x2

    dx, dxb, sq_err = _loss_head(xs, loss_target[0])
    loss = lax.psum(0.5 * sq_err[0, 0] / D, ("x", "y", "c"))

    g_norm_mix, g_norm_ffn = [None] * depth, [None] * depth
    g_qkv, g_wo, g_qg, g_kg = [None] * n_attn, [None] * n_attn, [None] * n_attn, [None] * n_attn
    g_win, g_wout, g_ai, g_small = [None] * n_rnn, [None] * n_rnn, [None] * n_rnn, [None] * n_rnn
    g_gu, g_down = [None] * depth, [None] * depth
    for i in reversed(range(depth)):
        l = i // 2
        xs, h1, mix, x1, h2, gu, act = saved[i]
        dact = _mm("ffn_down_dx", NT, dxb, Wd, BS((tm, D), lambda i_, j, r: (i_, 0)),
                   BS((None, None, F8, D), lambda i_, j, r, i=i: (j, i, 0, 0)),
                   BS((None, tm, F8), lambda i_, j, r: (j, i_, 0)), (NDEV, S, F8), CD, (S // tm, NDEV, 1))
        g_down[i] = _mm("ffn_down_dw", TN, act, dxb, BS((None, tk, F8), lambda i_, j, r: (i_, r, 0)),
                        BS((tk, tn), lambda i_, j, r: (r, j)),
                        BS((None, F8, tn), lambda i_, j, r: (i_, 0, j)), (NDEV, F8, D), CD, (NDEV, D // tn, S // tk))
        dgu = _swiglu_bwd(dact, gu)
        dh2 = _mm("ffn_gate_up_dx", NT, dgu, Wgu, BS((None, None, tm, F8), lambda i_, j, r: (r // NDEV, r % NDEV, i_, 0)),
                  BS((None, None, None, tn, F8), lambda i_, j, r, i=i: (r % NDEV, i, r // NDEV, j, 0)),
                  BS((tm, tn), lambda i_, j, r: (i_, j)), (S, D), F32, (S // tm, D // tn, 2 * NDEV))
        g_gu[i] = _mm("ffn_gate_up_dw", TN, h2, dgu, BS((tk, tn), lambda i_, j, r: (r, i_)),
                      BS((None, None, tk, F8), lambda i_, j, r: (j // NDEV, j % NDEV, r, 0)),
                      BS((None, None, tn, F8), lambda i_, j, r: (j % NDEV, j // NDEV, i_, 0)),
                      (NDEV, 2, D, F8), CD, (D // tn, 2 * NDEV, S // tk))
        dx, dxb, g_norm_ffn[i] = _rms_bwd("norm_ffn_bwd", dh2, x1, norm_ffn[i:i + 1], dx)
        if i % 2 == 0:
            qkv, qr, kr, o, lse = mix
            do = proj_rows_dx("attn_out_dx", dxb, Wo, l, OS, CD)
            g_wo[l] = proj_rows_dw("attn_out_dw", o, dxb, OS)
            delta = _attn_delta(do, o, NQ, NKV)
            dq = _flash_dq(qr, kr, qkv, do, lse, delta, NQ, NKV)
            dk, dv = _flash_dkv(qr, kr, qkv, do, jnp.transpose(lse[:, :, :G], (0, 2, 1)),
                                jnp.transpose(delta[:, :, :G], (0, 2, 1)), NQ, NKV)
            dqkv, g_qg[l], g_kg[l] = _qk_prep_bwd(qkv, dq, dk, dv, attn_q_gain[l:l + 1], attn_k_gain[l:l + 1],
                                                  cos_t, sin_t, NQ, NKV)
            g_qkv[l] = proj_cols_dw("attn_qkv_dw", h1, dqkv, QS)
            dh1 = proj_cols_dx("attn_qkv_dx", dqkv, Wqkv, l, QS)
        else:
            xy, w8, xc, hs, avs, ro = mix
            dro = proj_rows_dx("rnn_out_dx", dxb, Wout, l, CS, F32)
            g_wout[l] = proj_rows_dw("rnn_out_dw", ro, dxb, CS)
            dhs, dyb = _rnn_out_bwd(dro, hs[0], hs[1], xy)
            dxc, dwa, dwi, dba, dbi, dlam = None, [], [], [], [], []
            for d in range(2):
                res = _lru_bwd("lru_bwd_rev" if d else "lru_bwd", dhs, avs[d], hs[d], xc, Wai[l], d, b_a_f[l, d][None],
                               b_i_f[l, d][None], lam_f[l, d][None], bool(d), dxc)
                dxc = res[0]
                for lst, val in zip((dwa, dwi, dba, dbi, dlam), res[1:]):
                    lst.append(val)
            dxb_branch, dw8 = _conv_bwd(dxc, xy, w8)
            dxy = jnp.concatenate([dxb_branch, dyb], axis=1)
            g_win[l] = proj_cols_dw("rnn_in_dw", h1, dxy, IS)
            dh1 = proj_cols_dx("rnn_in_dx", dxy, Win, l, IS)
            dai = jnp.stack([jnp.stack(dwa), jnp.stack(dwi)])
            g_ai[l] = jnp.transpose(dai.reshape(2, 2, NB, NDEV, BWS, BW), (3, 0, 1, 2, 4, 5))
            sm = jnp.concatenate([dw8[0:5]] + dba + dbi + dlam + [jnp.zeros((SMALL_ROWS - 11, C), F32)], axis=0)
            g_small[l] = jnp.transpose(sm.reshape(SMALL_ROWS, NDEV, CS), (1, 0, 2))
        dx, dxb, g_norm_mix[i] = _rms_bwd("norm_mix_bwd", dh1, xs, norm_mix[i:i + 1], dx)

    st = lambda parts: jnp.stack(parts, axis=1)
    L_qkv, L_wo, L_win, L_wout, L_ai, L_small, L_gu, L_down = _exchange(
        "scatter_grads", [st(g_qkv), st(g_wo), st(g_win), st(g_wout), st(g_ai), st(g_small), st(g_gu), st(g_down)],
        scatter=True)
    norm_part = jnp.concatenate(g_norm_mix + g_norm_ffn, axis=0)
    gain_part = jnp.concatenate(g_qg + g_kg + [jnp.zeros((8 - 2 * n_attn, HEAD_DIM), F32)], axis=0)
    L_norm, L_gain = _exchange("gather_replicated", [norm_part, gain_part], scatter=False)

    out = {}
    out["attn_w_qkv"] = _adamw_rows("adamw_qkv", L_qkv, attn_w_qkv, m_attn_w_qkv, v_attn_w_qkv)
    out["attn_w_o"] = _adamw_rows("adamw_wo", L_wo, attn_w_o, m_attn_w_o, v_attn_w_o)
    out["rnn_w_in"] = _adamw_rows("adamw_win", L_win, rnn_w_in, m_rnn_w_in, v_rnn_w_in)
    out["rnn_w_out"] = _adamw_rows("adamw_wout", L_wout, rnn_w_out, m_rnn_w_out, v_rnn_w_out)
    out["ffn_w_down"] = _adamw_rows("adamw_down", L_down, ffn_w_down, m_ffn_w_down, v_ffn_w_down)
    trg = _tile(D, 256)
    for which, (name, w, m, v) in enumerate((("ffn_w_gate", ffn_w_gate, m_ffn_w_gate, v_ffn_w_gate),
                                             ("ffn_w_up", ffn_w_up, m_ffn_w_up, v_ffn_w_up))):
        spec = BS((None, trg, F8), lambda li, ri: (li, ri, 0))
        out[name] = _adamw("adamw_" + name, L_gu, BS((NDEV, None, None, trg, F8), lambda li, ri, which=which: (0, li, which, ri, 0)),
                           w, m, v, spec, (depth, D // trg))
    RA = 2 * NB * BWS
    for which, (name, w, m, v) in enumerate((("rnn_w_a", rnn_w_a, m_rnn_w_a, v_rnn_w_a),
                                             ("rnn_w_i", rnn_w_i, m_rnn_w_i, v_rnn_w_i))):
        spec = BS((None, RA, BW), lambda li: (li, 0, 0))
        res = _adamw("adamw_" + name, L_ai.reshape(NDEV, n_rnn, 2, RA, BW),
                     BS((NDEV, None, None, RA, BW), lambda li, which=which: (0, li, which, 0, 0)),
                     w.reshape(n_rnn, RA, BW), m.reshape(n_rnn, RA, BW), v.reshape(n_rnn, RA, BW), spec, (n_rnn,))
        out[name] = [r.reshape(w.shape) for r in res]
    small_res = _adamw_rows("adamw_small", L_small, small_sh,
                            pack_small(m_rnn_conv_w, m_rnn_conv_b, m_rnn_b_a, m_rnn_b_i, m_rnn_lambda),
                            pack_small(v_rnn_conv_w, v_rnn_conv_b, v_rnn_b_a, v_rnn_b_i, v_rnn_lambda))
    for name, vals in zip(("rnn_conv_w", "rnn_conv_b", "rnn_b_a", "rnn_b_i", "rnn_lambda"),
                          zip(*[unpack_small(r) for r in small_res])):
        out[name] = list(vals)
    norm_res = _adamw_rows("adamw_norm", L_norm, jnp.concatenate([norm_mix, norm_ffn], axis=0),
                           jnp.concatenate([m_norm_mix, m_norm_ffn], axis=0),
                           jnp.concatenate([v_norm_mix, v_norm_ffn], axis=0))
    out["norm_mix"] = [r[:depth] for r in norm_res]
    out["norm_ffn"] = [r[depth:] for r in norm_res]
    pad = jnp.zeros((8 - 2 * n_attn, HEAD_DIM), F32)
    gain_res = _adamw_rows("adamw_gain", L_gain, jnp.concatenate([attn_q_gain, attn_k_gain, pad], axis=0),
                           jnp.concatenate([m_attn_q_gain, m_attn_k_gain, pad], axis=0),
                           jnp.concatenate([v_attn_q_gain, v_attn_k_gain, pad + 1.0], axis=0))
    out["attn_q_gain"] = [r[:n_attn] for r in gain_res]
    out["attn_k_gain"] = [r[n_attn:2 * n_attn] for r in gain_res]

    order = ["norm_mix", "norm_ffn", "attn_w_qkv", "attn_q_gain", "attn_k_gain", "attn_w_o", "rnn_w_in", "rnn_conv_w",
             "rnn_conv_b", "rnn_w_a", "rnn_b_a", "rnn_w_i", "rnn_b_i", "rnn_lambda", "rnn_w_out", "ffn_w_gate",
             "ffn_w_up", "ffn_w_down"]
    return (loss, dx[None], *[out[k][0] for k in order], *[out[k][1] for k in order], *[out[k][2] for k in order],
            *[out[k][3] for k in order])
```

```python
import functools
import math

import jax
import jax.numpy as jnp
from jax import lax
from jax.experimental import pallas as pl
from jax.experimental.pallas import tpu as pltpu

F32 = jnp.float32
CD = jnp.bfloat16
NDEV = 8
HEAD_DIM = 128
GRID_W = 64
ROPE_THETA = 10000.0
ROPE_FREQS = 32
EPS = 1e-6
LRU_C = 8.0
SMALL_ROWS = 40
ADAM_LR, ADAM_B1, ADAM_B2, ADAM_EPS, ADAM_WD, ADAM_STEP = 0.001, 0.9, 0.999, 1e-08, 0.01, 10
V7X_VMEM_LIMIT = 56 * 1024 * 1024
BS = pl.BlockSpec
MESH = pl.DeviceIdType.MESH
NN = (((1,), (0,)), ((), ()))
NT = (((1,), (1,)), ((), ()))
TN = (((0,), (0,)), ((), ()))


def _params(*sem):
    return pltpu.CompilerParams(dimension_semantics=sem, vmem_limit_bytes=V7X_VMEM_LIMIT)


def _tile(n, want):
    return want if n % want == 0 else n


def _exchange(name, arrs, scatter):
    n = len(arrs)
    out_shape = [jax.ShapeDtypeStruct(a.shape if scatter else (NDEV,) + a.shape, a.dtype) for a in arrs]

    def body(*refs):
        ins, outs = refs[:n], refs[n:2 * n]
        send_sems, recv_sems, local_sems = refs[2 * n:]
        x, y, c = lax.axis_index("x"), lax.axis_index("y"), lax.axis_index("c")
        me = 4 * x + 2 * y + c
        local, sends, recvs = [], [], []
        for a in range(n):
            cp = pltpu.make_async_copy(ins[a].at[me] if scatter else ins[a], outs[a].at[me], local_sems.at[a])
            cp.start()
            local.append(cp)
        for k in range(1, NDEV):
            px = 1 - x if k & 4 else x
            py = 1 - y if k & 2 else y
            pc = 1 - c if k & 1 else c
            peer = 4 * px + 2 * py + pc
            for a in range(n):
                src = ins[a].at[peer] if scatter else ins[a]
                kw = dict(send_sem=send_sems.at[a, k - 1], recv_sem=recv_sems.at[a, k - 1],
                          device_id=(px, py, pc), device_id_type=MESH)
                cp = pltpu.make_async_remote_copy(src_ref=src, dst_ref=outs[a].at[me], **kw)
                cp.start()
                sends.append(cp)
                recvs.append(pltpu.make_async_remote_copy(src_ref=src, dst_ref=outs[a].at[peer], **kw))
        for cp in local:
            cp.wait()
        for cp in sends:
            cp.wait_send()
        for cp in recvs:
            cp.wait_recv()

    return pl.pallas_call(
        body, name=name, out_shape=out_shape,
        in_specs=[BS(memory_space=pl.ANY)] * n, out_specs=[BS(memory_space=pl.ANY)] * n,
        scratch_shapes=[pltpu.SemaphoreType.DMA((n, NDEV - 1)), pltpu.SemaphoreType.DMA((n, NDEV - 1)),
                        pltpu.SemaphoreType.DMA((n,))],
    )(*arrs)


HBM = BS(memory_space=pltpu.HBM)
SEM = BS(memory_space=pltpu.SEMAPHORE)
EFFECT = pltpu.SideEffectType.DATAFLOW_SIDE_EFFECTING


def _peers(x, y, c):
    out = []
    for k in range(1, NDEV):
        px = 1 - x if k & 4 else x
        py = 1 - y if k & 2 else y
        pc = 1 - c if k & 1 else c
        out.append(((px, py, pc), 4 * px + 2 * py + pc))
    return out


def _exchange_start(name, arrs, scatter):
    n = len(arrs)
    lands = [lax.empty(a.shape if scatter else (NDEV,) + a.shape, a.dtype) for a in arrs]

    def body(*refs):
        ins, outs = refs[:n], refs[n:2 * n]
        send_sems, recv_sems, token = refs[2 * n], refs[2 * n + 1], refs[-1]
        x, y, c = lax.axis_index("x"), lax.axis_index("y"), lax.axis_index("c")
        me = 4 * x + 2 * y + c
        for k, (dev, peer) in enumerate(_peers(x, y, c)):
            for a in range(n):
                pltpu.make_async_remote_copy(
                    src_ref=ins[a].at[peer] if scatter else ins[a], dst_ref=outs[a].at[me],
                    send_sem=send_sems.at[a * (NDEV - 1) + k], recv_sem=recv_sems.at[a * (NDEV - 1) + k],
                    device_id=dev, device_id_type=MESH).start()
        token[...] = jnp.zeros(token.shape, F32)

    sems = pltpu.SemaphoreType.DMA((n * (NDEV - 1),))
    res = pl.pallas_call(
        body, name=name,
        out_shape=(sems, sems, *[pltpu.HBM(a.shape, a.dtype) for a in arrs], *[pltpu.HBM(l.shape, l.dtype) for l in lands],
                   jax.ShapeDtypeStruct((8, 128), F32)),
        in_specs=[HBM] * (2 * n), out_specs=(SEM, SEM, *([HBM] * (2 * n)), BS(memory_space=pltpu.VMEM)),
        input_output_aliases={i: i + 2 for i in range(2 * n)},
        compiler_params=pltpu.CompilerParams(has_side_effects=EFFECT),
    )(*[pltpu.with_memory_space_constraint(t, pltpu.HBM) for t in list(arrs) + lands])
    return (res[0], res[1], res[2:2 + n], res[2 + n:2 + 2 * n], scatter), res[-1]


def _exchange_wait(name, handle, after):
    send_sems, recv_sems, srcs, lands, scatter = handle
    n = len(srcs)

    def body(*refs):
        ins, outs = refs[:n], refs[n:2 * n]
        s_sems, r_sems = refs[2 * n], refs[2 * n + 1]
        x, y, c = lax.axis_index("x"), lax.axis_index("y"), lax.axis_index("c")
        me = 4 * x + 2 * y + c
        for k, (dev, peer) in enumerate(_peers(x, y, c)):
            for a in range(n):
                src = ins[a].at[peer] if scatter else ins[a]
                kw = dict(send_sem=s_sems.at[a * (NDEV - 1) + k], recv_sem=r_sems.at[a * (NDEV - 1) + k], device_id=dev,
                          device_id_type=MESH)
                pltpu.make_async_remote_copy(src_ref=src, dst_ref=outs[a].at[me], **kw).wait_send()
                pltpu.make_async_remote_copy(src_ref=src, dst_ref=outs[a].at[peer], **kw).wait_recv()

    res = pl.pallas_call(
        body, name=name, out_shape=tuple(pltpu.HBM(t.shape, t.dtype) for t in list(srcs) + list(lands)),
        in_specs=[HBM] * (2 * n) + [SEM, SEM, BS(memory_space=pl.ANY)], out_specs=tuple([HBM] * (2 * n)),
        input_output_aliases={i: i for i in range(2 * n)},
        compiler_params=pltpu.CompilerParams(has_side_effects=EFFECT),
    )(*srcs, *lands, send_sems, recv_sems, after)
    me = 4 * lax.axis_index("x") + 2 * lax.axis_index("y") + lax.axis_index("c")
    out = []
    for src, land in zip(res[:n], res[n:]):
        own = lax.dynamic_index_in_dim(src, me, 0, keepdims=True) if scatter else src[None]
        out.append(lax.dynamic_update_slice_in_dim(land, own, me, 0))
    return out


def _mm(name, dims, a, b, a_spec, b_spec, o_spec, out_shape, out_dtype, grid, add=None):
    nr = grid[2]
    o_blk = tuple(d for d in o_spec.block_shape if d is not None)

    def body(*refs):
        a_ref, b_ref = refs[0], refs[1]
        add_ref = refs[2] if add is not None else None
        o_ref = refs[3] if add is not None else refs[2]
        part = lax.dot_general(a_ref[...].astype(CD), b_ref[...].astype(CD), dims, preferred_element_type=F32)

        def finish(v):
            if add_ref is not None:
                v = v + add_ref[...].astype(F32)
            o_ref[...] = v.astype(o_ref.dtype)

        if nr == 1:
            finish(part)
        else:
            acc = refs[-1]
            r = pl.program_id(2)

            @pl.when(r == 0)
            def _():
                acc[...] = part

            @pl.when(r > 0)
            def _():
                acc[...] += part

            @pl.when(r == nr - 1)
            def _():
                finish(acc[...])

    return pl.pallas_call(
        body, name=name, grid=grid, out_shape=jax.ShapeDtypeStruct(out_shape, out_dtype),
        in_specs=[a_spec, b_spec] + ([o_spec] if add is not None else []), out_specs=o_spec,
        scratch_shapes=[pltpu.VMEM(o_blk, F32)] if nr > 1 else [],
        compiler_params=_params("parallel", "parallel", "arbitrary"),
    )(*([a, b] + ([add] if add is not None else [])))


def _rms_fwd(name, x, g):
    S, D = x.shape
    tr = _tile(S, 512)

    def body(x_ref, g_ref, o_ref):
        xv = x_ref[...]
        r = lax.rsqrt(jnp.mean(xv * xv, axis=-1, keepdims=True) + EPS)
        o_ref[...] = (xv * r * g_ref[...]).astype(o_ref.dtype)

    return pl.pallas_call(
        body, name=name, grid=(S // tr,), out_shape=jax.ShapeDtypeStruct((S, D), CD),
        in_specs=[BS((tr, D), lambda i: (i, 0)), BS((1, D), lambda i: (0, 0))],
        out_specs=BS((tr, D), lambda i: (i, 0)), compiler_params=_params("parallel"),
    )(x, g)


def _rms_bwd(name, dh, x, g, dx_in):
    S, D = x.shape
    tr = _tile(S, 256)

    def body(dh_ref, x_ref, g_ref, dxin_ref, dx_ref, dxb_ref, dg_ref):
        xv = x_ref[...]
        dhv = dh_ref[...].astype(F32)
        r = lax.rsqrt(jnp.mean(xv * xv, axis=-1, keepdims=True) + EPS)
        yv = xv * r
        dy = dhv * g_ref[...]
        dx = dxin_ref[...] + r * (dy - yv * jnp.mean(dy * yv, axis=-1, keepdims=True))
        dx_ref[...] = dx
        dxb_ref[...] = dx.astype(CD)
        part = jnp.sum(dhv * yv, axis=0, keepdims=True)

        @pl.when(pl.program_id(0) == 0)
        def _():
            dg_ref[...] = part

        @pl.when(pl.program_id(0) > 0)
        def _():
            dg_ref[...] += part

    row = BS((tr, D), lambda i: (i, 0))
    vec = BS((1, D), lambda i: (0, 0))
    return pl.pallas_call(
        body, name=name, grid=(S // tr,),
        out_shape=[jax.ShapeDtypeStruct((S, D), F32), jax.ShapeDtypeStruct((S, D), CD),
                   jax.ShapeDtypeStruct((1, D), F32)],
        in_specs=[row, row, vec, row], out_specs=[row, row, vec], compiler_params=_params("arbitrary"),
    )(dh, x, g, dx_in)


def _loss_head(y, target):
    S, D = y.shape
    tr = _tile(S, 512)

    def body(y_ref, t_ref, dy_ref, dyb_ref, l_ref):
        err = y_ref[...] - t_ref[...]
        dy = err * (1.0 / D)
        dy_ref[...] = dy
        dyb_ref[...] = dy.astype(CD)
        part = jnp.full((8, 128), jnp.sum(err * err), F32)

        @pl.when(pl.program_id(0) == 0)
        def _():
            l_ref[...] = part

        @pl.when(pl.program_id(0) > 0)
        def _():
            l_ref[...] += part

    row = BS((tr, D), lambda i: (i, 0))
    return pl.pallas_call(
        body, name="loss_head", grid=(S // tr,),
        out_shape=[jax.ShapeDtypeStruct((S, D), F32), jax.ShapeDtypeStruct((S, D), CD),
                   jax.ShapeDtypeStruct((8, 128), F32)],
        in_specs=[row, row], out_specs=[row, row, BS((8, 128), lambda i: (0, 0))],
        compiler_params=_params("arbitrary"),
    )(y, target)


def _sigmoid(v):
    return 1.0 / (1.0 + jnp.exp(-v))


def _swiglu_fwd(gu):
    _, J, S, F8 = gu.shape
    tr = _tile(S, 1024)

    def body(gu_ref, o_ref):
        gv = gu_ref[0].astype(F32)
        uv = gu_ref[1].astype(F32)
        o_ref[...] = (gv * _sigmoid(gv) * uv).astype(o_ref.dtype)

    return pl.pallas_call(
        body, name="swiglu_fwd", grid=(J, S // tr), out_shape=jax.ShapeDtypeStruct((J, S, F8), CD),
        in_specs=[BS((2, None, tr, F8), lambda j, i: (0, j, i, 0))],
        out_specs=BS((None, tr, F8), lambda j, i: (j, i, 0)), compiler_params=_params("parallel", "parallel"),
    )(gu)


def _swiglu_bwd(da, gu):
    _, J, S, F8 = gu.shape
    tr = _tile(S, 1024)

    def body(da_ref, gu_ref, o_ref):
        gv = gu_ref[0].astype(F32)
        uv = gu_ref[1].astype(F32)
        dav = da_ref[...].astype(F32)
        sg = _sigmoid(gv)
        o_ref[0] = (dav * uv * sg * (1.0 + gv * (1.0 - sg))).astype(o_ref.dtype)
        o_ref[1] = (dav * gv * sg).astype(o_ref.dtype)

    pair = BS((2, None, tr, F8), lambda j, i: (0, j, i, 0))
    return pl.pallas_call(
        body, name="swiglu_bwd", grid=(J, S // tr), out_shape=jax.ShapeDtypeStruct(gu.shape, CD),
        in_specs=[BS((None, tr, F8), lambda j, i: (j, i, 0)), pair], out_specs=pair,
        compiler_params=_params("parallel", "parallel"),
    )(da, gu)


def _rope_tables(S):
    freqs = ROPE_THETA ** (-jnp.arange(ROPE_FREQS, dtype=F32) / ROPE_FREQS)
    pos = jnp.arange(S, dtype=jnp.int32)
    row = (pos // GRID_W).astype(F32)[:, None] * freqs
    col = (pos % GRID_W).astype(F32)[:, None] * freqs
    cr, sr, cc, sc = jnp.cos(row), jnp.sin(row), jnp.cos(col), jnp.sin(col)
    return jnp.concatenate([cr, cr, cc, cc], axis=1), jnp.concatenate([-sr, sr, -sc, sc], axis=1)


def _swap_halves(v):
    lane = lax.broadcasted_iota(jnp.int32, v.shape, 1)
    return jnp.where((lane & 32) == 0, pltpu.roll(v, 96, 1), pltpu.roll(v, 32, 1))


def _qk_prep_fwd(qkv, qg, kg, cos_t, sin_t, NQ, NKV):
    S, W = qkv.shape
    tr = _tile(S, 256)
    scale = HEAD_DIM ** -0.5

    def body(qkv_ref, qg_ref, kg_ref, c_ref, s_ref, q_out, k_out):
        C, Sg = c_ref[...], s_ref[...]

        def one(xh, gain, sc):
            xh = xh.astype(F32)
            r = lax.rsqrt(jnp.mean(xh * xh, axis=-1, keepdims=True) + EPS)
            nv = xh * r * gain
            return (nv * C + _swap_halves(nv) * Sg) * sc

        for h in range(NQ):
            sl = slice(h * HEAD_DIM, (h + 1) * HEAD_DIM)
            q_out[:, sl] = one(qkv_ref[:, sl], qg_ref[...], scale).astype(CD)
        for h in range(NKV):
            sl = slice((NQ + h) * HEAD_DIM, (NQ + h + 1) * HEAD_DIM)
            k_out[:, h * HEAD_DIM:(h + 1) * HEAD_DIM] = one(qkv_ref[:, sl], kg_ref[...], 1.0).astype(CD)

    vec = BS((1, HEAD_DIM), lambda i: (0, 0))
    tab = BS((tr, HEAD_DIM), lambda i: (i, 0))
    return pl.pallas_call(
        body, name="qk_prep_fwd", grid=(S // tr,),
        out_shape=[jax.ShapeDtypeStruct((S, NQ * HEAD_DIM), CD), jax.ShapeDtypeStruct((S, NKV * HEAD_DIM), CD)],
        in_specs=[BS((tr, W), lambda i: (i, 0)), vec, vec, tab, tab],
        out_specs=[BS((tr, NQ * HEAD_DIM), lambda i: (i, 0)), BS((tr, NKV * HEAD_DIM), lambda i: (i, 0))],
        compiler_params=_params("parallel"),
    )(qkv, qg, kg, cos_t, sin_t)


def _qk_prep_bwd(qkv, dq, dk, dv, qg, kg, cos_t, sin_t, NQ, NKV):
    S, W = qkv.shape
    tr = _tile(S, 256)
    scale = HEAD_DIM ** -0.5

    def body(qkv_ref, dq_ref, dk_ref, dv_ref, qg_ref, kg_ref, c_ref, s_ref, o_ref, dqg_ref, dkg_ref):
        C, Sg = c_ref[...], s_ref[...]

        def one(xh, dout, gain, sc):
            xh = xh.astype(F32)
            r = lax.rsqrt(jnp.mean(xh * xh, axis=-1, keepdims=True) + EPS)
            yv = xh * r
            dout = dout.astype(F32) * sc
            dn = dout * C + _swap_halves(dout * Sg)
            dy = dn * gain
            dx = r * (dy - yv * jnp.mean(dy * yv, axis=-1, keepdims=True))
            return dx, jnp.sum(dn * yv, axis=0, keepdims=True)

        dqg = jnp.zeros((1, HEAD_DIM), F32)
        dkg = jnp.zeros((1, HEAD_DIM), F32)
        for h in range(NQ):
            sl = slice(h * HEAD_DIM, (h + 1) * HEAD_DIM)
            dx, dg = one(qkv_ref[:, sl], dq_ref[:, sl], qg_ref[...], scale)
            o_ref[:, sl] = dx.astype(CD)
            dqg = dqg + dg
        for h in range(NKV):
            sl = slice((NQ + h) * HEAD_DIM, (NQ + h + 1) * HEAD_DIM)
            dx, dg = one(qkv_ref[:, sl], dk_ref[:, h * HEAD_DIM:(h + 1) * HEAD_DIM], kg_ref[...], 1.0)
            o_ref[:, sl] = dx.astype(CD)
            dkg = dkg + dg
        o_ref[:, (NQ + NKV) * HEAD_DIM:] = dv_ref[...].astype(CD)

        @pl.when(pl.program_id(0) == 0)
        def _():
            dqg_ref[...] = dqg
            dkg_ref[...] = dkg

        @pl.when(pl.program_id(0) > 0)
        def _():
            dqg_ref[...] += dqg
            dkg_ref[...] += dkg

    vec = BS((1, HEAD_DIM), lambda i: (0, 0))
    tab = BS((tr, HEAD_DIM), lambda i: (i, 0))
    full = BS((tr, W), lambda i: (i, 0))
    return pl.pallas_call(
        body, name="qk_prep_bwd", grid=(S // tr,),
        out_shape=[jax.ShapeDtypeStruct((S, W), CD), jax.ShapeDtypeStruct((1, HEAD_DIM), F32),
                   jax.ShapeDtypeStruct((1, HEAD_DIM), F32)],
        in_specs=[full, BS((tr, NQ * HEAD_DIM), lambda i: (i, 0)), BS((tr, NKV * HEAD_DIM), lambda i: (i, 0)),
                  BS((tr, NKV * HEAD_DIM), lambda i: (i, 0)), vec, vec, tab, tab],
        out_specs=[full, vec, vec], compiler_params=_params("arbitrary"),
    )(qkv, dq, dk, dv, qg, kg, cos_t, sin_t)


def _lane_pack(cols):
    rows = cols[0].shape[0]
    lane = lax.broadcasted_iota(jnp.int32, (rows, HEAD_DIM), 1)
    out = jnp.zeros((rows, HEAD_DIM), F32)
    for g, col in enumerate(cols):
        out = jnp.where(lane == g, col, out)
    return out


def _flash_fwd(qr, kr, qkv, NQ, NKV):
    S = qr.shape[0]
    G = NQ // NKV
    tq, tk = _tile(S, 512), _tile(S, 512)
    nk = S // tk
    GW = G * HEAD_DIM

    def body(q_ref, k_ref, v_ref, o_ref, lse_ref, m_s, acc_s):
        ki = pl.program_id(2)

        @pl.when(ki == 0)
        def _():
            m_s[...] = jnp.full(m_s.shape, -1e30, F32)
            acc_s[...] = jnp.zeros(acc_s.shape, F32)

        kv = k_ref[...]
        v_ones = jnp.concatenate([v_ref[...], jnp.ones((tk, HEAD_DIM), CD)], axis=1)
        for g in range(G):
            s = lax.dot_general(q_ref[:, g * HEAD_DIM:(g + 1) * HEAD_DIM], kv, NT, preferred_element_type=F32)
            m_prev = m_s[g]
            m_new = jnp.maximum(m_prev, jnp.max(s, axis=-1, keepdims=True))
            alpha = jnp.exp(m_prev - m_new)
            p = jnp.exp(s - jnp.tile(m_new, (1, tk // HEAD_DIM)))
            acc_s[g] = jnp.tile(alpha, (1, 2)) * acc_s[g] + jnp.dot(p.astype(CD), v_ones, preferred_element_type=F32)
            m_s[g] = m_new

        @pl.when(ki == nk - 1)
        def _():
            lane = lax.broadcasted_iota(jnp.int32, (tq, HEAD_DIM), 1)
            lse = jnp.zeros((tq, HEAD_DIM), F32)
            for g in range(G):
                l = acc_s[g, :, HEAD_DIM:]
                o_ref[:, g * HEAD_DIM:(g + 1) * HEAD_DIM] = (acc_s[g, :, :HEAD_DIM] / l).astype(o_ref.dtype)
                lse = jnp.where(lane == g, m_s[g] + jnp.log(l), lse)
            lse_ref[...] = lse

    return pl.pallas_call(
        body, name="flash_fwd", grid=(NKV, S // tq, nk),
        out_shape=[jax.ShapeDtypeStruct((S, NQ * HEAD_DIM), CD), jax.ShapeDtypeStruct((NKV, S, HEAD_DIM), F32)],
        in_specs=[BS((tq, GW), lambda h, i, k: (i, h)), BS((tk, HEAD_DIM), lambda h, i, k: (k, h)),
                  BS((tk, HEAD_DIM), lambda h, i, k: (k, NQ + NKV + h))],
        out_specs=[BS((tq, GW), lambda h, i, k: (i, h)), BS((None, tq, HEAD_DIM), lambda h, i, k: (h, i, 0))],
        scratch_shapes=[pltpu.VMEM((G, tq, HEAD_DIM), F32), pltpu.VMEM((G, tq, 2 * HEAD_DIM), F32)],
        compiler_params=_params("parallel", "parallel", "arbitrary"),
    )(qr, kr, qkv)


def _attn_delta(do, o, NQ, NKV):
    S = o.shape[0]
    G = NQ // NKV
    tr = _tile(S, 512)

    def body(do_ref, o_ref, d_ref):
        for kv in range(NKV):
            cols = []
            for g in range(G):
                sl = slice((kv * G + g) * HEAD_DIM, (kv * G + g + 1) * HEAD_DIM)
                cols.append(jnp.sum(do_ref[:, sl].astype(F32) * o_ref[:, sl].astype(F32), axis=-1, keepdims=True))
            d_ref[kv] = _lane_pack(cols)

    row = BS((tr, NQ * HEAD_DIM), lambda i: (i, 0))
    return pl.pallas_call(
        body, name="attn_delta", grid=(S // tr,), out_shape=jax.ShapeDtypeStruct((NKV, S, HEAD_DIM), F32),
        in_specs=[row, row], out_specs=BS((NKV, tr, HEAD_DIM), lambda i: (0, i, 0)),
        compiler_params=_params("parallel"),
    )(do, o)


def _flash_dq(qr, kr, qkv, do, lse, delta, NQ, NKV):
    S = qr.shape[0]
    G = NQ // NKV
    tq, tk = _tile(S, 512), _tile(S, 512)
    nk = S // tk
    GW = G * HEAD_DIM

    def body(q_ref, k_ref, v_ref, do_ref, lse_ref, dl_ref, dq_ref, acc_s):
        ki = pl.program_id(2)

        @pl.when(ki == 0)
        def _():
            acc_s[...] = jnp.zeros(acc_s.shape, F32)

        kv, vv = k_ref[...], v_ref[...]
        for g in range(G):
            sl = slice(g * HEAD_DIM, (g + 1) * HEAD_DIM)
            s = lax.dot_general(q_ref[:, sl], kv, NT, preferred_element_type=F32)
            p = jnp.exp(s - lse_ref[:, g:g + 1])
            dp = lax.dot_general(do_ref[:, sl], vv, NT, preferred_element_type=F32)
            ds = p * (dp - dl_ref[:, g:g + 1])
            acc_s[g] += jnp.dot(ds.astype(CD), kv, preferred_element_type=F32)

        @pl.when(ki == nk - 1)
        def _():
            for g in range(G):
                dq_ref[:, g * HEAD_DIM:(g + 1) * HEAD_DIM] = acc_s[g].astype(dq_ref.dtype)

    qb = BS((tq, GW), lambda h, i, k: (i, h))
    st = BS((None, tq, HEAD_DIM), lambda h, i, k: (h, i, 0))
    return pl.pallas_call(
        body, name="flash_dq", grid=(NKV, S // tq, nk), out_shape=jax.ShapeDtypeStruct((S, NQ * HEAD_DIM), CD),
        in_specs=[qb, BS((tk, HEAD_DIM), lambda h, i, k: (k, h)), BS((tk, HEAD_DIM), lambda h, i, k: (k, NQ + NKV + h)),
                  qb, st, st],
        out_specs=qb, scratch_shapes=[pltpu.VMEM((G, tq, HEAD_DIM), F32)],
        compiler_params=_params("parallel", "parallel", "arbitrary"),
    )(qr, kr, qkv, do, lse, delta)


def _flash_dkv(qr, kr, qkv, do, lse_rows, delta_rows, NQ, NKV):
    S = qr.shape[0]
    G = NQ // NKV
    tq, tk = _tile(S, 512), _tile(S, 512)
    nq = S // tq
    GW = G * HEAD_DIM

    def body(q_ref, k_ref, v_ref, do_ref, lse_ref, dl_ref, dk_ref, dv_ref, dk_s, dv_s):
        qi = pl.program_id(2)

        @pl.when(qi == 0)
        def _():
            dk_s[...] = jnp.zeros(dk_s.shape, F32)
            dv_s[...] = jnp.zeros(dv_s.shape, F32)

        kv, vv = k_ref[...], v_ref[...]
        for g in range(G):
            sl = slice(g * HEAD_DIM, (g + 1) * HEAD_DIM)
            qg, dog = q_ref[:, sl], do_ref[:, sl]
            st = lax.dot_general(kv, qg, NT, preferred_element_type=F32)
            pt = jnp.exp(st - lse_ref[g:g + 1, :])
            dv_s[...] += jnp.dot(pt.astype(CD), dog, preferred_element_type=F32)
            dpt = lax.dot_general(vv, dog, NT, preferred_element_type=F32)
            dst = pt * (dpt - dl_ref[g:g + 1, :])
            dk_s[...] += jnp.dot(dst.astype(CD), qg, preferred_element_type=F32)

        @pl.when(qi == nq - 1)
        def _():
            dk_ref[...] = dk_s[...].astype(dk_ref.dtype)
            dv_ref[...] = dv_s[...].astype(dv_ref.dtype)

    qb = BS((tq, GW), lambda h, k, i: (i, h))
    st = BS((None, G, tq), lambda h, k, i: (h, 0, i))
    kb = BS((tk, HEAD_DIM), lambda h, k, i: (k, h))
    return pl.pallas_call(
        body, name="flash_dkv", grid=(NKV, S // tk, nq),
        out_shape=[jax.ShapeDtypeStruct((S, NKV * HEAD_DIM), CD), jax.ShapeDtypeStruct((S, NKV * HEAD_DIM), CD)],
        in_specs=[qb, kb, BS((tk, HEAD_DIM), lambda h, k, i: (k, NQ + NKV + h)), qb, st, st],
        out_specs=[kb, kb], scratch_shapes=[pltpu.VMEM((tk, HEAD_DIM), F32), pltpu.VMEM((tk, HEAD_DIM), F32)],
        compiler_params=_params("parallel", "parallel", "arbitrary"),
    )(qr, kr, qkv, do, lse_rows, delta_rows)


HALO8 = 8


def _round(v):
    return v.astype(CD).astype(F32)


def _shift_down(v, k, above):
    T = v.shape[0]
    row = lax.broadcasted_iota(jnp.int32, v.shape, 0)
    out = pltpu.roll(v, k, 0)
    for j in range(k):
        out = jnp.where(row == j, above[above.shape[0] - k + j:above.shape[0] - k + j + 1, :], out)
    return out


def _shift_up(v, k, below):
    T = v.shape[0]
    row = lax.broadcasted_iota(jnp.int32, v.shape, 0)
    out = pltpu.roll(v, T - k, 0)
    for j in range(k):
        out = jnp.where(row == T - k + j, below[j:j + 1, :], out)
    return out


def _halo_specs(T, C, halo, n_chunks, S, col=0, rev=False):
    per = T // halo
    last = S // halo - 1
    ch = (lambda i: n_chunks - 1 - i) if rev else (lambda i: i)
    return (BS((T, C), lambda i: (ch(i), col)),
            BS((halo, C), lambda i: (jnp.maximum(ch(i) * per - 1, 0), col)),
            BS((halo, C), lambda i: (jnp.minimum((ch(i) + 1) * per, last), col)))


def _conv_fwd(xy, w8, bias):
    S, C2 = xy.shape
    C = C2 // 2
    T = _tile(S, 256)
    n = S // T

    def body(x_ref, p_ref, q_ref, w_ref, b_ref, o_ref):
        i = pl.program_id(0)
        xv = _round(x_ref[...])
        above = jnp.where(i > 0, _round(p_ref[...]), 0.0)
        below = jnp.where(i < n - 1, _round(q_ref[...]), 0.0)
        wv = _round(w_ref[...])
        o_ref[...] = (wv[0:1, :] * _shift_down(xv, 2, above) + wv[1:2, :] * _shift_down(xv, 1, above)
                      + wv[2:3, :] * xv + wv[3:4, :] * _shift_up(xv, 1, below) + b_ref[...])

    main, prev, nxt = _halo_specs(T, C, HALO8, n, S)
    return pl.pallas_call(
        body, name="conv_fwd", grid=(n,), out_shape=jax.ShapeDtypeStruct((S, C), F32),
        in_specs=[main, prev, nxt, BS((8, C), lambda i: (0, 0)), BS((1, C), lambda i: (0, 0))],
        out_specs=BS((T, C), lambda i: (i, 0)), compiler_params=_params("parallel"),
    )(xy, xy, xy, w8, bias)


def _tile_scan(a, b, reverse):
    row = lax.broadcasted_iota(jnp.int32, a.shape, 0)
    for sh in (1, 2, 4):
        if reverse:
            a_s, b_s, ok = pltpu.roll(a, 8 - sh, 0), pltpu.roll(b, 8 - sh, 0), row < 8 - sh
        else:
            a_s, b_s, ok = pltpu.roll(a, sh, 0), pltpu.roll(b, sh, 0), row >= sh
        b = jnp.where(ok, a * b_s + b, b)
        a = jnp.where(ok, a * a_s, a)
    return a, b


def _scan_chunk(a_ref, h_ref, carry_ref, reverse):
    T = h_ref.shape[0]
    nt = T // 8

    def step(j, carry):
        t = (nt - 1 - j) if reverse else j
        rows = pl.ds(pl.multiple_of(t * 8, 8), 8)
        ca, hb = _tile_scan(a_ref[rows, :], h_ref[rows, :], reverse)
        hv = hb + ca * carry
        h_ref[rows, :] = hv
        edge = hv[0:1, :] if reverse else hv[7:8, :]
        return jnp.broadcast_to(edge, hv.shape)

    carry_ref[...] = lax.fori_loop(0, nt, step, carry_ref[...])


def _softplus(z):
    return jnp.maximum(z, 0.0) + jnp.log(1.0 + jnp.exp(-jnp.abs(z)))


def _neg_expm1(z):
    return jnp.where(z > -0.01, -z * (1.0 + z * (0.5 + z * (1.0 / 6.0 + z * (1.0 / 24.0)))), 1.0 - jnp.exp(z))


def _gates(xs, wa, wi, ba, bi, sp):
    r = _sigmoid(jnp.dot(xs, wa, preferred_element_type=F32) + ba)
    ig = _sigmoid(jnp.dot(xs, wi, preferred_element_type=F32) + bi)
    la = -LRU_C * r * sp
    return r, ig, jnp.exp(la), jnp.sqrt(_neg_expm1(2.0 * la))


def _lru_fwd(name, xc, wai, d, ba, bi, lam, reverse):
    S, C = xc.shape
    NB, BW = wai.shape[2], wai.shape[3]
    T = _tile(S, 256)
    n = S // T
    ch = (lambda i: n - 1 - i) if reverse else (lambda i: i)

    def body(x_ref, wa_ref, wi_ref, ba_ref, bi_ref, lam_ref, h_ref, a_ref, carry):
        @pl.when(pl.program_id(0) == 0)
        def _():
            carry[...] = jnp.zeros(carry.shape, F32)

        for b in range(NB):
            sl = slice(b * BW, (b + 1) * BW)
            xf = x_ref[:, sl]
            _, ig, av, sq = _gates(xf.astype(CD), wa_ref[b], wi_ref[b], ba_ref[:, sl], bi_ref[:, sl],
                                   _softplus(-lam_ref[:, sl]))
            a_ref[:, sl] = av
            h_ref[:, sl] = sq * ig * xf
        _scan_chunk(a_ref, h_ref, carry, reverse)

    vec = BS((1, C), lambda i: (0, 0))
    blk = BS((T, C), lambda i: (ch(i), 0))
    return pl.pallas_call(
        body, name=name, grid=(n,),
        out_shape=[jax.ShapeDtypeStruct((S, C), F32), jax.ShapeDtypeStruct((S, C), F32)],
        in_specs=[blk, BS((None, None, NB, BW, BW), lambda i: (0, d, 0, 0, 0)),
                  BS((None, None, NB, BW, BW), lambda i: (1, d, 0, 0, 0)), vec, vec, vec],
        out_specs=[blk, blk], scratch_shapes=[pltpu.VMEM((8, C), F32)], compiler_params=_params("arbitrary"),
    )(xc, wai, wai, ba, bi, lam)


_GELU_C = math.sqrt(2.0 / math.pi)


def _rnn_out_fwd(h0, h1, xy):
    S, C = h0.shape
    tr = _tile(S, 256)

    def body(h0_ref, h1_ref, y_ref, o_ref):
        yv = y_ref[...].astype(F32)
        gl = 0.5 * yv * (1.0 + jnp.tanh(_GELU_C * (yv + 0.044715 * yv * yv * yv)))
        o_ref[...] = ((h0_ref[...] + h1_ref[...]) * gl).astype(o_ref.dtype)

    row = BS((tr, C), lambda i: (i, 0))
    return pl.pallas_call(
        body, name="rnn_out_fwd", grid=(S // tr,), out_shape=jax.ShapeDtypeStruct((S, C), CD),
        in_specs=[row, row, BS((tr, C), lambda i: (i, 1))], out_specs=row, compiler_params=_params("parallel"),
    )(h0, h1, xy)


def _rnn_out_bwd(dout, h0, h1, xy):
    S, C = h0.shape
    tr = _tile(S, 256)

    def body(d_ref, h0_ref, h1_ref, y_ref, dh_ref, dy_ref):
        yv = y_ref[...].astype(F32)
        dv = d_ref[...].astype(F32)
        t = jnp.tanh(_GELU_C * (yv + 0.044715 * yv * yv * yv))
        gl = 0.5 * yv * (1.0 + t)
        dgl = 0.5 * (1.0 + t) + 0.5 * yv * (1.0 - t * t) * _GELU_C * (1.0 + 3.0 * 0.044715 * yv * yv)
        dh_ref[...] = dv * gl
        dy_ref[...] = (dv * (h0_ref[...] + h1_ref[...]) * dgl).astype(dy_ref.dtype)

    row = BS((tr, C), lambda i: (i, 0))
    return pl.pallas_call(
        body, name="rnn_out_bwd", grid=(S // tr,),
        out_shape=[jax.ShapeDtypeStruct((S, C), F32), jax.ShapeDtypeStruct((S, C), CD)],
        in_specs=[row, row, row, BS((tr, C), lambda i: (i, 1))], out_specs=[row, row],
        compiler_params=_params("parallel"),
    )(dout, h0, h1, xy)


def _lru_bwd(name, dhs, a, h, xc, wai, d, ba, bi, lam, reverse, dxc_in):
    S, C = xc.shape
    NB, BW = wai.shape[2], wai.shape[3]
    T = _tile(S, 256)
    n = S // T
    back = not reverse

    def body(*refs):
        (dh_ref, a_ref, a_prev, a_next, h_ref, h_prev, h_next, x_ref, wa_ref, wi_ref, ba_ref, bi_ref,
         lam_ref) = refs[:13]
        k = 13
        add_ref = None
        if dxc_in is not None:
            add_ref = refs[k]
            k += 1
        dx_ref, dwa_ref, dwi_ref, dba_ref, dbi_ref, dlam_ref, coef_s, lam_s, carry, dsp_s = refs[k:]
        i = pl.program_id(0)
        first_chunk = i == (n - 1 if back else 0)
        last_chunk = i == (0 if back else n - 1)

        @pl.when(i == 0)
        def _():
            carry[...] = jnp.zeros(carry.shape, F32)
            dsp_s[...] = jnp.zeros(dsp_s.shape, F32)
            dwa_ref[...] = jnp.zeros(dwa_ref.shape, F32)
            dwi_ref[...] = jnp.zeros(dwi_ref.shape, F32)
            dba_ref[...] = jnp.zeros(dba_ref.shape, F32)
            dbi_ref[...] = jnp.zeros(dbi_ref.shape, F32)

        av, hv = a_ref[...], h_ref[...]
        if reverse:
            coef_s[...] = _shift_down(av, 1, a_prev[...])
            h_nb = _shift_up(hv, 1, jnp.where(last_chunk, 0.0, h_next[...]))
        else:
            coef_s[...] = _shift_up(av, 1, a_next[...])
            h_nb = _shift_down(hv, 1, jnp.where(first_chunk, 0.0, h_prev[...]))
        lam_s[...] = dh_ref[...]
        _scan_chunk(coef_s, lam_s, carry, back)

        for b in range(NB):
            sl = slice(b * BW, (b + 1) * BW)
            xf = x_ref[:, sl]
            xs = xf.astype(CD)
            sp = _softplus(-lam_ref[:, sl])
            r, ig, ab, sq = _gates(xs, wa_ref[b], wi_ref[b], ba_ref[:, sl], bi_ref[:, sl], sp)
            du = lam_s[:, sl]
            da = du * h_nb[:, sl]
            d_ig = du * sq * xf
            d_sq = du * ig * xf
            d_la = da * ab - d_sq * (ab * ab) / sq
            d_pa = d_la * (-LRU_C * sp) * r * (1.0 - r)
            d_pi = d_ig * ig * (1.0 - ig)
            dsp_s[:, sl] += jnp.sum(d_la * (-LRU_C * r), axis=0, keepdims=True)
            dba_ref[:, sl] += jnp.sum(d_pa, axis=0, keepdims=True)
            dbi_ref[:, sl] += jnp.sum(d_pi, axis=0, keepdims=True)
            d_pa_c, d_pi_c = d_pa.astype(CD), d_pi.astype(CD)
            dwa_ref[b] += lax.dot_general(xs, d_pa_c, TN, preferred_element_type=F32)
            dwi_ref[b] += lax.dot_general(xs, d_pi_c, TN, preferred_element_type=F32)
            dx = (du * sq * ig + lax.dot_general(d_pa_c, wa_ref[b], NT, preferred_element_type=F32)
                  + lax.dot_general(d_pi_c, wi_ref[b], NT, preferred_element_type=F32))
            if add_ref is not None:
                dx = dx + add_ref[:, sl]
            dx_ref[:, sl] = dx

        @pl.when(i == n - 1)
        def _():
            dlam_ref[...] = -dsp_s[...] * _sigmoid(-lam_ref[...])

    vec = BS((1, C), lambda i: (0, 0))
    main, prev, nxt = _halo_specs(T, C, HALO8, n, S, rev=back)
    wspec = lambda which: BS((None, None, NB, BW, BW), lambda i: (which, d, 0, 0, 0))
    acc = BS((NB, BW, BW), lambda i: (0, 0, 0))
    ins = [dhs, a, a, a, h, h, h, xc, wai, wai, ba, bi, lam]
    in_specs = [main, main, prev, nxt, main, prev, nxt, main, wspec(0), wspec(1), vec, vec, vec]
    if dxc_in is not None:
        ins.append(dxc_in)
        in_specs.append(main)
    return pl.pallas_call(
        body, name=name, grid=(n,),
        out_shape=[jax.ShapeDtypeStruct((S, C), F32), jax.ShapeDtypeStruct((NB, BW, BW), F32),
                   jax.ShapeDtypeStruct((NB, BW, BW), F32), jax.ShapeDtypeStruct((1, C), F32),
                   jax.ShapeDtypeStruct((1, C), F32), jax.ShapeDtypeStruct((1, C), F32)],
        in_specs=in_specs, out_specs=[main, acc, acc, vec, vec, vec],
        scratch_shapes=[pltpu.VMEM((T, C), F32), pltpu.VMEM((T, C), F32), pltpu.VMEM((8, C), F32),
                        pltpu.VMEM((1, C), F32)],
        compiler_params=_params("arbitrary"),
    )(*ins)


def _conv_bwd(dxc, xy, w8):
    S, C = dxc.shape
    T = _tile(S, 256)
    n = S // T

    def body(d_ref, dp_ref, dn_ref, x_ref, xp_ref, xn_ref, w_ref, dx_ref, dw_ref):
        i = pl.program_id(0)
        dv = _round(d_ref[...])
        d_above = jnp.where(i > 0, _round(dp_ref[...]), 0.0)
        d_below = jnp.where(i < n - 1, _round(dn_ref[...]), 0.0)
        wv = _round(w_ref[...])
        dx = (wv[0:1, :] * _shift_up(dv, 2, d_below) + wv[1:2, :] * _shift_up(dv, 1, d_below)
              + wv[2:3, :] * dv + wv[3:4, :] * _shift_down(dv, 1, d_above))
        dx_ref[...] = dx.astype(dx_ref.dtype)
        xv = _round(x_ref[...])
        above = jnp.where(i > 0, _round(xp_ref[...]), 0.0)
        below = jnp.where(i < n - 1, _round(xn_ref[...]), 0.0)
        rows = [jnp.sum(dv * _shift_down(xv, 2, above), axis=0, keepdims=True),
                jnp.sum(dv * _shift_down(xv, 1, above), axis=0, keepdims=True),
                jnp.sum(dv * xv, axis=0, keepdims=True),
                jnp.sum(dv * _shift_up(xv, 1, below), axis=0, keepdims=True),
                jnp.sum(dv, axis=0, keepdims=True)]
        rid = lax.broadcasted_iota(jnp.int32, (8, C), 0)
        part = jnp.zeros((8, C), F32)
        for j, rv in enumerate(rows):
            part = jnp.where(rid == j, rv, part)

        @pl.when(i == 0)
        def _():
            dw_ref[...] = part

        @pl.when(i > 0)
        def _():
            dw_ref[...] += part

    dmain, dprev, dnext = _halo_specs(T, C, HALO8, n, S)
    xmain, xprev, xnext = _halo_specs(T, C, HALO8, n, S)
    w = BS((8, C), lambda i: (0, 0))
    return pl.pallas_call(
        body, name="conv_bwd", grid=(n,),
        out_shape=[jax.ShapeDtypeStruct((S, C), CD), jax.ShapeDtypeStruct((8, C), F32)],
        in_specs=[dmain, dprev, dnext, xmain, xprev, xnext, w], out_specs=[dmain, w],
        compiler_params=_params("arbitrary"),
    )(dxc, dxc, dxc, xy, xy, xy, w8)


def _adamw(name, lands, w, m, v, pick=None):
    c1 = 1.0 - ADAM_B1 ** ADAM_STEP
    c2 = 1.0 - ADAM_B2 ** ADAM_STEP
    L, C = w.shape[0], w.shape[-1]
    w3, m3, v3 = (t.reshape(L, -1, C) for t in (w, m, v))
    R = w3.shape[1]
    tr = R
    for cand in (1024, 512, 256, 128, 64, 32, 16, 8):
        if R % cand == 0 and cand * C * 4 <= (1 << 20):
            tr = cand
            break

    def body(*refs):
        l_refs = refs[:L]
        w_ref, m_ref, v_ref, g_out, d_out, m_out, v_out = refs[L:]

        def update(l_ref):
            g = l_ref[0].astype(F32)
            for s in range(1, NDEV):
                g = g + l_ref[s].astype(F32)
            mn = ADAM_B1 * m_ref[...] + (1.0 - ADAM_B1) * g
            vn = ADAM_B2 * v_ref[...] + (1.0 - ADAM_B2) * (g * g)
            g_out[...] = g
            m_out[...] = mn
            v_out[...] = vn
            d_out[...] = -ADAM_LR * ((mn / c1) / (jnp.sqrt(vn / c2) + ADAM_EPS) + ADAM_WD * w_ref[...])

        for k in range(L):
            pl.when(pl.program_id(0) == k)(functools.partial(update, l_refs[k]))

    if pick is None:
        lands = [t.reshape(NDEV, R, C) for t in lands]
        land_specs = [BS((NDEV, tr, C), lambda l, i, k=k: (0, jnp.where(l == k, i, 0), 0)) for k in range(L)]
    else:
        lands = [t.reshape(NDEV, -1, R, C) for t in lands]
        land_specs = [BS((NDEV, None, tr, C), lambda l, i, k=k: (0, pick, jnp.where(l == k, i, 0), 0)) for k in range(L)]
    spec = BS((None, tr, C), lambda l, i: (l, i, 0))
    sds = jax.ShapeDtypeStruct(w3.shape, F32)
    outs = pl.pallas_call(
        body, name=name, grid=(L, R // tr), out_shape=[sds] * 4, in_specs=land_specs + [spec, spec, spec],
        out_specs=[spec] * 4, compiler_params=_params("arbitrary", "arbitrary"),
    )(*lands, w3, m3, v3)
    return [o.reshape(w.shape) for o in outs]


def kernel(x, norm_mix, norm_ffn, attn_w_qkv, attn_q_gain, attn_k_gain, attn_w_o, rnn_w_in, rnn_conv_w, rnn_conv_b, rnn_w_a, rnn_b_a, rnn_w_i, rnn_b_i, rnn_lambda, rnn_w_out, ffn_w_gate, ffn_w_up, ffn_w_down, loss_target, m_norm_mix, m_norm_ffn, m_attn_w_qkv, m_attn_q_gain, m_attn_k_gain, m_attn_w_o, m_rnn_w_in, m_rnn_conv_w, m_rnn_conv_b, m_rnn_w_a, m_rnn_b_a, m_rnn_w_i, m_rnn_b_i, m_rnn_lambda, m_rnn_w_out, m_ffn_w_gate, m_ffn_w_up, m_ffn_w_down, v_norm_mix, v_norm_ffn, v_attn_w_qkv, v_attn_q_gain, v_attn_k_gain, v_attn_w_o, v_rnn_w_in, v_rnn_conv_w, v_rnn_conv_b, v_rnn_w_a, v_rnn_b_a, v_rnn_w_i, v_rnn_b_i, v_rnn_lambda, v_rnn_w_out, v_ffn_w_gate, v_ffn_w_up, v_ffn_w_down):
    S, D = x.shape[1], x.shape[2]
    depth = norm_mix.shape[0]
    n_attn, n_rnn = attn_w_qkv.shape[0], rnn_w_in.shape[0]
    QS = attn_w_qkv.shape[2]
    NQ = D // HEAD_DIM
    NKV = (QS * NDEV // HEAD_DIM - NQ) // 2
    G = NQ // NKV
    OS = attn_w_o.shape[1]
    IS = rnn_w_in.shape[2]
    C = rnn_w_out.shape[1] * NDEV
    CS = C // NDEV
    NB, BWS, BW = rnn_w_a.shape[2], rnn_w_a.shape[3], rnn_w_a.shape[4]
    F8 = ffn_w_gate.shape[2]
    tm = _tile(S, 1024)
    tn = _tile(D, 1024)
    tk = _tile(S, 1024)

    def rows_at(parts, total):
        out = None
        for at, p in parts:
            cfg = [(0, 0)] * (p.ndim - 2) + [(at, total - at - p.shape[-2]), (0, 0)]
            out = jnp.pad(p, cfg) if out is None else out + jnp.pad(p, cfg)
        return out

    def pack_small(cw, cb, b_a, b_i, lam):
        return rows_at([(0, cw), (8, cb[:, None, :]), (16, b_a), (24, b_i), (32, lam)], SMALL_ROWS)

    def unpack_small(p):
        return p[:, 0:4], p[:, 8], p[:, 16:18], p[:, 24:26], p[:, 32:34]

    gu_sh = jnp.stack([ffn_w_gate, ffn_w_up], axis=1).astype(CD)
    ai_sh = jnp.stack([rnn_w_a, rnn_w_i], axis=1).astype(CD)
    small_sh = pack_small(rnn_conv_w, rnn_conv_b, rnn_b_a, rnn_b_i, rnn_lambda)
    Wqkv0, Wo0 = _exchange("gather_first", [attn_w_qkv[0:1].astype(CD), attn_w_o[0:1].astype(CD)], scatter=False)
    ffn0_handle, ffn0_token = _exchange_start("gather_ffn0_start", [gu_sh[0:1], ffn_w_down[0:1].astype(CD)], scatter=False)
    rest_handle, rest_token = _exchange_start(
        "gather_rest_start",
        [attn_w_qkv[1:].astype(CD), attn_w_o[1:].astype(CD), rnn_w_in.astype(CD), rnn_w_out.astype(CD), ai_sh, gu_sh[1:],
         ffn_w_down[1:].astype(CD), small_sh], scatter=False)
    started = ffn0_token[0, 0] + rest_token[0, 0]
    wqkv, wo = {0: (Wqkv0, 0)}, {0: (Wo0, 0)}
    win, wout, wgu, wd = {}, {}, {}, {}
    cos_t, sin_t = _rope_tables(S)

    def proj_cols(name, a, W, l, width, dtype):
        return _mm(name, NN, a, W, BS((tm, D), lambda i, j, r: (i, 0)),
                   BS((None, None, D, width), lambda i, j, r: (j, l, 0, 0)),
                   BS((tm, width), lambda i, j, r: (i, j)), (S, NDEV * width), dtype, (S // tm, NDEV, 1))

    def proj_cols_dx(name, dy, W, l, width):
        return _mm(name, NT, dy, W, BS((tm, width), lambda i, j, r: (i, r)),
                   BS((None, None, tn, width), lambda i, j, r: (r, l, j, 0)),
                   BS((tm, tn), lambda i, j, r: (i, j)), (S, D), F32, (S // tm, D // tn, NDEV))

    def proj_cols_dw(name, a, dy, width):
        return _mm(name, TN, a, dy, BS((tk, tn), lambda i, j, r: (r, i)),
                   BS((tk, width), lambda i, j, r: (r, j)),
                   BS((None, tn, width), lambda i, j, r: (j, i, 0)), (NDEV, D, width), CD, (D // tn, NDEV, S // tk))

    def proj_rows(name, a, W, l, rows, resid):
        return _mm(name, NN, a, W, BS((tm, rows), lambda i, j, r: (i, r)),
                   BS((None, None, rows, tn), lambda i, j, r: (r, l, 0, j)),
                   BS((tm, tn), lambda i, j, r: (i, j)), (S, D), F32, (S // tm, D // tn, NDEV), add=resid)

    def proj_rows_dx(name, dy, W, l, rows, dtype):
        return _mm(name, NT, dy, W, BS((tm, D), lambda i, j, r: (i, 0)),
                   BS((None, None, rows, D), lambda i, j, r: (j, l, 0, 0)),
                   BS((tm, rows), lambda i, j, r: (i, j)), (S, NDEV * rows), dtype, (S // tm, NDEV, 1))

    def proj_rows_dw(name, a, dy, rows):
        return _mm(name, TN, a, dy, BS((tk, rows), lambda i, j, r: (r, i)),
                   BS((tk, tn), lambda i, j, r: (r, j)),
                   BS((None, rows, tn), lambda i, j, r: (i, 0, j)), (NDEV, rows, D), CD, (NDEV, D // tn, S // tk))

    saved = []
    xs = x[0]
    for i in range(depth):
        l = i // 2
        h1 = _rms_fwd("norm_mix_fwd", xs, norm_mix[i:i + 1] + started if i == 0 else norm_mix[i:i + 1])
        if i % 2 == 0:
            qkv = proj_cols("attn_qkv", h1, *wqkv[l], QS, CD)
            qr, kr = _qk_prep_fwd(qkv, attn_q_gain[l:l + 1], attn_k_gain[l:l + 1], cos_t, sin_t, NQ, NKV)
            o, lse = _flash_fwd(qr, kr, qkv, NQ, NKV)
            x1 = proj_rows("attn_out", o, *wo[l], OS, xs)
            mix = (qkv, qr, kr, o, lse)
        else:
            xy = proj_cols("rnn_in", h1, *win[l], IS, F32)
            w8 = jnp.pad(conv_w_f[l], ((0, 4), (0, 0)))
            xc = _conv_fwd(xy, w8, conv_b_f[l][None])
            hs, avs = [], []
            for d in range(2):
                hd, ad = _lru_fwd("lru_fwd_rev" if d else "lru_fwd", xc, Wai[l], d, b_a_f[l, d][None], b_i_f[l, d][None],
                                  lam_f[l, d][None], bool(d))
                hs.append(hd)
                avs.append(ad)
            ro = _rnn_out_fwd(hs[0], hs[1], xy)
            x1 = proj_rows("rnn_out", ro, *wout[l], CS, xs)
            mix = (xy, w8, xc, hs, avs, ro)
        if i == 0:
            Wgu0, Wd0 = _exchange_wait("gather_ffn0_wait", ffn0_handle, x1)
            wgu[0], wd[0] = (Wgu0, 0), (Wd0, 0)
        h2 = _rms_fwd("norm_ffn_fwd", x1, norm_ffn[i:i + 1])
        Wgu, li = wgu[i]
        gu = _mm("ffn_gate_up", NN, h2, Wgu, BS((tm, D), lambda i_, j, r: (i_, 0)),
                 BS((None, None, None, D, F8), lambda i_, j, r, li=li: (j % NDEV, li, j // NDEV, 0, 0)),
                 BS((None, None, tm, F8), lambda i_, j, r: (j // NDEV, j % NDEV, i_, 0)),
                 (2, NDEV, S, F8), CD, (S // tm, 2 * NDEV, 1))
        act = _swiglu_fwd(gu)
        Wd, li = wd[i]
        x2 = _mm("ffn_down", NN, act, Wd, BS((None, tm, F8), lambda i_, j, r: (r, i_, 0)),
                 BS((None, None, F8, tn), lambda i_, j, r, li=li: (r, li, 0, j)),
                 BS((tm, tn), lambda i_, j, r: (i_, j)), (S, D), F32, (S // tm, D // tn, NDEV), add=x1)
        saved.append((xs, h1, mix, x1, h2, gu, act))
        xs = x2
        if i == 0:
            Wqkv1, Wo1, Win, Wout, Wai, Wgu1, Wd1, small = _exchange_wait("gather_rest_wait", rest_handle, x2)
            for k in range(1, n_attn):
                wqkv[k], wo[k] = (Wqkv1, k - 1), (Wo1, k - 1)
            for k in range(n_rnn):
                win[k], wout[k] = (Win, k), (Wout, k)
            for k in range(1, depth):
                wgu[k], wd[k] = (Wgu1, k - 1), (Wd1, k - 1)
            Wai = jnp.transpose(Wai, (1, 2, 3, 4, 0, 5, 6)).reshape(n_rnn, 2, 2, NB, BW, BW)
            small = jnp.transpose(small, (1, 2, 0, 3)).reshape(n_rnn, SMALL_ROWS, C)
            conv_w_f, conv_b_f, b_a_f, b_i_f, lam_f = unpack_small(small)

    dx, dxb, sq_err = _loss_head(xs, loss_target[0])
    loss = lax.psum(0.5 * sq_err[0, 0] / D, ("x", "y", "c"))

    g_norm_mix, g_norm_ffn = [None] * depth, [None] * depth
    g_qkv, g_wo, g_qg, g_kg = [None] * n_attn, [None] * n_attn, [None] * n_attn, [None] * n_attn
    g_win, g_wout, g_ai, g_small = [None] * n_rnn, [None] * n_rnn, [None] * n_rnn, [None] * n_rnn
    g_gu, g_down = [None] * depth, [None] * depth
    scat, sent = {}, None
    for i in reversed(range(depth)):
        l = i // 2
        xs, h1, mix, x1, h2, gu, act = saved[i]
        gain_ffn = norm_ffn[i:i + 1] if sent is None else norm_ffn[i:i + 1] + sent[0, 0]
        Wd, li = wd[i]
        dact = _mm("ffn_down_dx", NT, dxb, Wd, BS((tm, D), lambda i_, j, r: (i_, 0)),
                   BS((None, None, F8, D), lambda i_, j, r, li=li: (j, li, 0, 0)),
                   BS((None, tm, F8), lambda i_, j, r: (j, i_, 0)), (NDEV, S, F8), CD, (S // tm, NDEV, 1))
        g_down[i] = _mm("ffn_down_dw", TN, act, dxb, BS((None, tk, F8), lambda i_, j, r: (i_, r, 0)),
                        BS((tk, tn), lambda i_, j, r: (r, j)),
                        BS((None, F8, tn), lambda i_, j, r: (i_, 0, j)), (NDEV, F8, D), CD, (NDEV, D // tn, S // tk))
        dgu = _swiglu_bwd(dact, gu)
        Wgu, li = wgu[i]
        dh2 = _mm("ffn_gate_up_dx", NT, dgu, Wgu, BS((None, None, tm, F8), lambda i_, j, r: (r // NDEV, r % NDEV, i_, 0)),
                  BS((None, None, None, tn, F8), lambda i_, j, r, li=li: (r % NDEV, li, r // NDEV, j, 0)),
                  BS((tm, tn), lambda i_, j, r: (i_, j)), (S, D), F32, (S // tm, D // tn, 2 * NDEV))
        g_gu[i] = _mm("ffn_gate_up_dw", TN, h2, dgu, BS((tk, tn), lambda i_, j, r: (r, i_)),
                      BS((None, None, tk, F8), lambda i_, j, r: (j // NDEV, j % NDEV, r, 0)),
                      BS((None, None, tn, F8), lambda i_, j, r: (j % NDEV, j // NDEV, i_, 0)),
                      (NDEV, 2, D, F8), CD, (D // tn, 2 * NDEV, S // tk))
        dx, dxb, g_norm_ffn[i] = _rms_bwd("norm_ffn_bwd", dh2, x1, gain_ffn, dx)
        if i == 0:
            scat["ffn0"], sent = _exchange_start("scatter_ffn0_start", [g_gu[0], g_down[0]], scatter=True)
        gain_mix = norm_mix[i:i + 1] + sent[0, 0] if i == 0 else norm_mix[i:i + 1]
        if i % 2 == 0:
            qkv, qr, kr, o, lse = mix
            do = proj_rows_dx("attn_out_dx", dxb, *wo[l], OS, CD)
            g_wo[l] = proj_rows_dw("attn_out_dw", o, dxb, OS)
            delta = _attn_delta(do, o, NQ, NKV)
            dq = _flash_dq(qr, kr, qkv, do, lse, delta, NQ, NKV)
            dk, dv = _flash_dkv(qr, kr, qkv, do, jnp.transpose(lse[:, :, :G], (0, 2, 1)),
                                jnp.transpose(delta[:, :, :G], (0, 2, 1)), NQ, NKV)
            dqkv, g_qg[l], g_kg[l] = _qk_prep_bwd(qkv, dq, dk, dv, attn_q_gain[l:l + 1], attn_k_gain[l:l + 1],
                                                  cos_t, sin_t, NQ, NKV)
            g_qkv[l] = proj_cols_dw("attn_qkv_dw", h1, dqkv, QS)
            dh1 = proj_cols_dx("attn_qkv_dx", dqkv, *wqkv[l], QS)
        else:
            xy, w8, xc, hs, avs, ro = mix
            dro = proj_rows_dx("rnn_out_dx", dxb, *wout[l], CS, F32)
            g_wout[l] = proj_rows_dw("rnn_out_dw", ro, dxb, CS)
            dhs, dyb = _rnn_out_bwd(dro, hs[0], hs[1], xy)
            dxc, dwa, dwi, dba, dbi, dlam = None, [], [], [], [], []
            for d in range(2):
                res = _lru_bwd("lru_bwd_rev" if d else "lru_bwd", dhs, avs[d], hs[d], xc, Wai[l], d, b_a_f[l, d][None],
                               b_i_f[l, d][None], lam_f[l, d][None], bool(d), dxc)
                dxc = res[0]
                for lst, val in zip((dwa, dwi, dba, dbi, dlam), res[1:]):
                    lst.append(val)
            dxb_branch, dw8 = _conv_bwd(dxc, xy, w8)
            dxy = jnp.concatenate([dxb_branch, dyb], axis=1)
            g_win[l] = proj_cols_dw("rnn_in_dw", h1, dxy, IS)
            dh1 = proj_cols_dx("rnn_in_dx", dxy, *win[l], IS)
            dai = jnp.stack([jnp.stack(dwa), jnp.stack(dwi)])
            g_ai[l] = jnp.transpose(dai.reshape(2, 2, NB, NDEV, BWS, BW), (3, 0, 1, 2, 4, 5))
            sm = rows_at([(0, dw8[0:4]), (8, dw8[4:5]), (16, dba[0]), (17, dba[1]), (24, dbi[0]), (25, dbi[1]),
                          (32, dlam[0]), (33, dlam[1])], SMALL_ROWS)
            g_small[l] = jnp.transpose(sm.reshape(SMALL_ROWS, NDEV, CS), (1, 0, 2))
        dx, dxb, g_norm_mix[i] = _rms_bwd("norm_mix_bwd", dh1, xs, gain_mix, dx)
        if i % 2 == 1:
            scat[i], sent = _exchange_start(f"scatter_layer{i}_start", [g_gu[i], g_down[i], g_win[l], g_wout[l], g_ai[l], g_small[l]],
                                            scatter=True)
        elif i > 0:
            scat[i], sent = _exchange_start(f"scatter_layer{i}_start", [g_gu[i], g_down[i], g_qkv[l], g_wo[l]], scatter=True)

    L_gu, L_down = [None] * depth, [None] * depth
    L_qkv, L_wo = [None] * n_attn, [None] * n_attn
    L_win, L_wout, L_ai, L_small = [None] * n_rnn, [None] * n_rnn, [None] * n_rnn, [None] * n_rnn
    L_qkv[0], L_wo[0] = _exchange("scatter_last", [g_qkv[0], g_wo[0]], scatter=True)
    norm_part = rows_at(list(enumerate(g_norm_mix + g_norm_ffn)), 2 * depth)
    gain_part = rows_at(list(enumerate(g_qg + g_kg)), 8)
    L_norm, L_gain = _exchange("gather_replicated", [norm_part, gain_part], scatter=False)
    for i in reversed(range(1, depth)):
        l = i // 2
        got = _exchange_wait(f"scatter_layer{i}_wait", scat[i], L_norm)
        if i % 2 == 1:
            L_gu[i], L_down[i], L_win[l], L_wout[l], L_ai[l], L_small[l] = got
        else:
            L_gu[i], L_down[i], L_qkv[l], L_wo[l] = got
    L_gu[0], L_down[0] = _exchange_wait("scatter_ffn0_wait", scat["ffn0"], L_norm)

    out = {}
    out["attn_w_qkv"] = _adamw("adamw_qkv", L_qkv, attn_w_qkv, m_attn_w_qkv, v_attn_w_qkv)
    out["attn_w_o"] = _adamw("adamw_wo", L_wo, attn_w_o, m_attn_w_o, v_attn_w_o)
    out["rnn_w_in"] = _adamw("adamw_win", L_win, rnn_w_in, m_rnn_w_in, v_rnn_w_in)
    out["rnn_w_out"] = _adamw("adamw_wout", L_wout, rnn_w_out, m_rnn_w_out, v_rnn_w_out)
    out["ffn_w_down"] = _adamw("adamw_down", L_down, ffn_w_down, m_ffn_w_down, v_ffn_w_down)
    out["ffn_w_gate"] = _adamw("adamw_gate", L_gu, ffn_w_gate, m_ffn_w_gate, v_ffn_w_gate, pick=0)
    out["ffn_w_up"] = _adamw("adamw_up", L_gu, ffn_w_up, m_ffn_w_up, v_ffn_w_up, pick=1)
    out["rnn_w_a"] = _adamw("adamw_w_a", L_ai, rnn_w_a, m_rnn_w_a, v_rnn_w_a, pick=0)
    out["rnn_w_i"] = _adamw("adamw_w_i", L_ai, rnn_w_i, m_rnn_w_i, v_rnn_w_i, pick=1)
    small_res = _adamw("adamw_small", L_small, small_sh,
                       pack_small(m_rnn_conv_w, m_rnn_conv_b, m_rnn_b_a, m_rnn_b_i, m_rnn_lambda),
                       pack_small(v_rnn_conv_w, v_rnn_conv_b, v_rnn_b_a, v_rnn_b_i, v_rnn_lambda))
    for name, vals in zip(("rnn_conv_w", "rnn_conv_b", "rnn_b_a", "rnn_b_i", "rnn_lambda"),
                          zip(*[unpack_small(r) for r in small_res])):
        out[name] = list(vals)
    one = lambda t: t[None]
    for name, land, w, m, v in (("norm_mix", L_norm[:, :depth], norm_mix, m_norm_mix, v_norm_mix),
                                ("norm_ffn", L_norm[:, depth:], norm_ffn, m_norm_ffn, v_norm_ffn),
                                ("attn_q_gain", L_gain[:, :n_attn], attn_q_gain, m_attn_q_gain, v_attn_q_gain),
                                ("attn_k_gain", L_gain[:, n_attn:2 * n_attn], attn_k_gain, m_attn_k_gain, v_attn_k_gain)):
        out[name] = [r[0] for r in _adamw("adamw_" + name, [land], one(w), one(m), one(v))]

    order = ["norm_mix", "norm_ffn", "attn_w_qkv", "attn_q_gain", "attn_k_gain", "attn_w_o", "rnn_w_in", "rnn_conv_w",
             "rnn_conv_b", "rnn_w_a", "rnn_b_a", "rnn_w_i", "rnn_b_i", "rnn_lambda", "rnn_w_out", "ffn_w_gate",
             "ffn_w_up", "ffn_w_down"]
    return (loss, dx[None], *[out[k][0] for k in order], *[out[k][1] for k in order], *[out[k][2] for k in order],
            *[out[k][3] for k in order])
```

```python
import functools
import math

import jax
import jax.numpy as jnp
from jax import lax
from jax.experimental import pallas as pl
from jax.experimental.pallas import tpu as pltpu

F32 = jnp.float32
CD = jnp.bfloat16
NDEV = 8
HEAD_DIM = 128
GRID_W = 64
ROPE_THETA = 10000.0
ROPE_FREQS = 32
EPS = 1e-6
LRU_C = 8.0
SMALL_ROWS = 40
ADAM_LR, ADAM_B1, ADAM_B2, ADAM_EPS, ADAM_WD, ADAM_STEP = 0.001, 0.9, 0.999, 1e-08, 0.01, 10
V7X_VMEM_LIMIT = 56 * 1024 * 1024
BS = pl.BlockSpec
MESH = pl.DeviceIdType.MESH
NN = (((1,), (0,)), ((), ()))
NT = (((1,), (1,)), ((), ()))
TN = (((0,), (0,)), ((), ()))


def _params(*sem):
    return pltpu.CompilerParams(dimension_semantics=sem, vmem_limit_bytes=V7X_VMEM_LIMIT)


def _tile(n, want):
    return want if n % want == 0 else n


def _exchange(name, arrs, scatter):
    n = len(arrs)
    out_shape = [jax.ShapeDtypeStruct(a.shape if scatter else (NDEV,) + a.shape, a.dtype) for a in arrs]

    def body(*refs):
        ins, outs = refs[:n], refs[n:2 * n]
        send_sems, recv_sems, local_sems = refs[2 * n:]
        x, y, c = lax.axis_index("x"), lax.axis_index("y"), lax.axis_index("c")
        me = 4 * x + 2 * y + c
        local, sends, recvs = [], [], []
        for a in range(n):
            cp = pltpu.make_async_copy(ins[a].at[me] if scatter else ins[a], outs[a].at[me], local_sems.at[a])
            cp.start()
            local.append(cp)
        for k in range(1, NDEV):
            px = 1 - x if k & 4 else x
            py = 1 - y if k & 2 else y
            pc = 1 - c if k & 1 else c
            peer = 4 * px + 2 * py + pc
            for a in range(n):
                src = ins[a].at[peer] if scatter else ins[a]
                kw = dict(send_sem=send_sems.at[a, k - 1], recv_sem=recv_sems.at[a, k - 1],
                          device_id=(px, py, pc), device_id_type=MESH)
                cp = pltpu.make_async_remote_copy(src_ref=src, dst_ref=outs[a].at[me], **kw)
                cp.start()
                sends.append(cp)
                recvs.append(pltpu.make_async_remote_copy(src_ref=src, dst_ref=outs[a].at[peer], **kw))
        for cp in local:
            cp.wait()
        for cp in sends:
            cp.wait_send()
        for cp in recvs:
            cp.wait_recv()

    return pl.pallas_call(
        body, name=name, out_shape=out_shape,
        in_specs=[BS(memory_space=pl.ANY)] * n, out_specs=[BS(memory_space=pl.ANY)] * n,
        scratch_shapes=[pltpu.SemaphoreType.DMA((n, NDEV - 1)), pltpu.SemaphoreType.DMA((n, NDEV - 1)),
                        pltpu.SemaphoreType.DMA((n,))],
    )(*arrs)


HBM = BS(memory_space=pltpu.HBM)
SEM = BS(memory_space=pltpu.SEMAPHORE)
EFFECT = pltpu.SideEffectType.DATAFLOW_SIDE_EFFECTING


def _peers(x, y, c):
    out = []
    for k in range(1, NDEV):
        px = 1 - x if k & 4 else x
        py = 1 - y if k & 2 else y
        pc = 1 - c if k & 1 else c
        out.append(((px, py, pc), 4 * px + 2 * py + pc))
    return out


def _exchange_start(name, arrs, scatter, after=None):
    n = len(arrs)
    lands = [lax.empty(a.shape if scatter else (NDEV,) + a.shape, a.dtype) for a in arrs]
    extra = [] if after is None else [after]

    def body(*refs):
        ins, outs = refs[:n], refs[n:2 * n]
        send_sems, recv_sems, token = refs[2 * n + len(extra)], refs[2 * n + len(extra) + 1], refs[-1]
        x, y, c = lax.axis_index("x"), lax.axis_index("y"), lax.axis_index("c")
        me = 4 * x + 2 * y + c
        for k, (dev, peer) in enumerate(_peers(x, y, c)):
            for a in range(n):
                pltpu.make_async_remote_copy(
                    src_ref=ins[a].at[peer] if scatter else ins[a], dst_ref=outs[a].at[me],
                    send_sem=send_sems.at[a * (NDEV - 1) + k], recv_sem=recv_sems.at[a * (NDEV - 1) + k],
                    device_id=dev, device_id_type=MESH).start()
        token[...] = jnp.zeros(token.shape, F32)

    sems = pltpu.SemaphoreType.DMA((n * (NDEV - 1),))
    res = pl.pallas_call(
        body, name=name,
        out_shape=(sems, sems, *[pltpu.HBM(a.shape, a.dtype) for a in arrs], *[pltpu.HBM(l.shape, l.dtype) for l in lands],
                   jax.ShapeDtypeStruct((8, 128), F32)),
        in_specs=[HBM] * (2 * n) + [BS(memory_space=pl.ANY)] * len(extra),
        out_specs=(SEM, SEM, *([HBM] * (2 * n)), BS(memory_space=pltpu.VMEM)),
        input_output_aliases={i: i + 2 for i in range(2 * n)},
        compiler_params=pltpu.CompilerParams(has_side_effects=EFFECT),
    )(*[pltpu.with_memory_space_constraint(t, pltpu.HBM) for t in list(arrs) + lands], *extra)
    return (res[0], res[1], res[2:2 + n], res[2 + n:2 + 2 * n], scatter), res[-1]


def _exchange_wait(name, handle, after):
    send_sems, recv_sems, srcs, lands, scatter = handle
    n = len(srcs)

    def body(*refs):
        ins, outs = refs[:n], refs[n:2 * n]
        s_sems, r_sems = refs[2 * n], refs[2 * n + 1]
        x, y, c = lax.axis_index("x"), lax.axis_index("y"), lax.axis_index("c")
        me = 4 * x + 2 * y + c
        for k, (dev, peer) in enumerate(_peers(x, y, c)):
            for a in range(n):
                src = ins[a].at[peer] if scatter else ins[a]
                kw = dict(send_sem=s_sems.at[a * (NDEV - 1) + k], recv_sem=r_sems.at[a * (NDEV - 1) + k], device_id=dev,
                          device_id_type=MESH)
                pltpu.make_async_remote_copy(src_ref=src, dst_ref=outs[a].at[me], **kw).wait_send()
                pltpu.make_async_remote_copy(src_ref=src, dst_ref=outs[a].at[peer], **kw).wait_recv()

    res = pl.pallas_call(
        body, name=name, out_shape=tuple(pltpu.HBM(t.shape, t.dtype) for t in list(srcs) + list(lands)),
        in_specs=[HBM] * (2 * n) + [SEM, SEM, BS(memory_space=pl.ANY)], out_specs=tuple([HBM] * (2 * n)),
        input_output_aliases={i: i for i in range(2 * n)},
        compiler_params=pltpu.CompilerParams(has_side_effects=EFFECT),
    )(*srcs, *lands, send_sems, recv_sems, after)
    me = 4 * lax.axis_index("x") + 2 * lax.axis_index("y") + lax.axis_index("c")
    out = []
    for src, land in zip(res[:n], res[n:]):
        own = lax.dynamic_index_in_dim(src, me, 0, keepdims=True) if scatter else src[None]
        out.append(lax.dynamic_update_slice_in_dim(land, own, me, 0))
    return out


def _mm(name, dims, a, b, a_spec, b_spec, o_spec, out_shape, out_dtype, grid, add=None, pairs=None):
    nr = grid[2]
    o_blk = tuple(d for d in o_spec.block_shape if d is not None)

    def body(*refs):
        a_ref, b_ref = refs[0], refs[1]
        add_ref = refs[2] if add is not None else None
        o_ref = refs[3] if add is not None else refs[2]
        part = None
        for av, bv in ([(a_ref[...], b_ref[...])] if pairs is None else pairs(a_ref, b_ref)):
            prod = lax.dot_general(av.astype(CD), bv.astype(CD), dims, preferred_element_type=F32)
            part = prod if part is None else part + prod

        def finish(v):
            if add_ref is not None:
                v = v + add_ref[...].astype(F32)
            o_ref[...] = v.astype(o_ref.dtype)

        if nr == 1:
            finish(part)
        else:
            acc = refs[-1]
            r = pl.program_id(2)

            @pl.when(r == 0)
            def _():
                acc[...] = part

            @pl.when(r > 0)
            def _():
                acc[...] += part

            @pl.when(r == nr - 1)
            def _():
                finish(acc[...])

    return pl.pallas_call(
        body, name=name, grid=grid, out_shape=jax.ShapeDtypeStruct(out_shape, out_dtype),
        in_specs=[a_spec, b_spec] + ([o_spec] if add is not None else []), out_specs=o_spec,
        scratch_shapes=[pltpu.VMEM(o_blk, F32)] if nr > 1 else [],
        compiler_params=_params("parallel", "parallel", "arbitrary"),
    )(*([a, b] + ([add] if add is not None else [])))


def _rms_fwd(name, x, g):
    S, D = x.shape
    tr = _tile(S, 512)

    def body(x_ref, g_ref, o_ref):
        xv = x_ref[...]
        r = lax.rsqrt(jnp.mean(xv * xv, axis=-1, keepdims=True) + EPS)
        o_ref[...] = (xv * r * g_ref[...]).astype(o_ref.dtype)

    return pl.pallas_call(
        body, name=name, grid=(S // tr,), out_shape=jax.ShapeDtypeStruct((S, D), CD),
        in_specs=[BS((tr, D), lambda i: (i, 0)), BS((1, D), lambda i: (0, 0))],
        out_specs=BS((tr, D), lambda i: (i, 0)), compiler_params=_params("parallel"),
    )(x, g)


def _rms_bwd(name, dh, x, g, dx_in):
    S, D = x.shape
    tr = _tile(S, 256)

    def body(dh_ref, x_ref, g_ref, dxin_ref, dx_ref, dxb_ref, dg_ref):
        xv = x_ref[...]
        dhv = dh_ref[...].astype(F32)
        r = lax.rsqrt(jnp.mean(xv * xv, axis=-1, keepdims=True) + EPS)
        yv = xv * r
        dy = dhv * g_ref[...]
        dx = dxin_ref[...] + r * (dy - yv * jnp.mean(dy * yv, axis=-1, keepdims=True))
        dx_ref[...] = dx
        dxb_ref[...] = dx.astype(CD)
        part = jnp.sum(dhv * yv, axis=0, keepdims=True)

        @pl.when(pl.program_id(0) == 0)
        def _():
            dg_ref[...] = part

        @pl.when(pl.program_id(0) > 0)
        def _():
            dg_ref[...] += part

    row = BS((tr, D), lambda i: (i, 0))
    vec = BS((1, D), lambda i: (0, 0))
    return pl.pallas_call(
        body, name=name, grid=(S // tr,),
        out_shape=[jax.ShapeDtypeStruct((S, D), F32), jax.ShapeDtypeStruct((S, D), CD),
                   jax.ShapeDtypeStruct((1, D), F32)],
        in_specs=[row, row, vec, row], out_specs=[row, row, vec], compiler_params=_params("arbitrary"),
    )(dh, x, g, dx_in)


def _loss_head(y, target):
    S, D = y.shape
    tr = _tile(S, 512)

    def body(y_ref, t_ref, dy_ref, dyb_ref, l_ref):
        err = y_ref[...] - t_ref[...]
        dy = err * (1.0 / D)
        dy_ref[...] = dy
        dyb_ref[...] = dy.astype(CD)
        part = jnp.full((8, 128), jnp.sum(err * err), F32)

        @pl.when(pl.program_id(0) == 0)
        def _():
            l_ref[...] = part

        @pl.when(pl.program_id(0) > 0)
        def _():
            l_ref[...] += part

    row = BS((tr, D), lambda i: (i, 0))
    return pl.pallas_call(
        body, name="loss_head", grid=(S // tr,),
        out_shape=[jax.ShapeDtypeStruct((S, D), F32), jax.ShapeDtypeStruct((S, D), CD),
                   jax.ShapeDtypeStruct((8, 128), F32)],
        in_specs=[row, row], out_specs=[row, row, BS((8, 128), lambda i: (0, 0))],
        compiler_params=_params("arbitrary"),
    )(y, target)


def _sigmoid(v):
    return 1.0 / (1.0 + jnp.exp(-v))


def _swiglu_fwd(gu):
    _, J, S, F8 = gu.shape
    tr = _tile(S, 1024)

    def body(gu_ref, o_ref):
        gv = gu_ref[0].astype(F32)
        uv = gu_ref[1].astype(F32)
        o_ref[...] = (gv * _sigmoid(gv) * uv).astype(o_ref.dtype)

    return pl.pallas_call(
        body, name="swiglu_fwd", grid=(J, S // tr), out_shape=jax.ShapeDtypeStruct((J, S, F8), CD),
        in_specs=[BS((2, None, tr, F8), lambda j, i: (0, j, i, 0))],
        out_specs=BS((None, tr, F8), lambda j, i: (j, i, 0)), compiler_params=_params("parallel", "parallel"),
    )(gu)


def _swiglu_bwd(da, gu):
    _, J, S, F8 = gu.shape
    tr = _tile(S, 1024)

    def body(da_ref, gu_ref, o_ref):
        gv = gu_ref[0].astype(F32)
        uv = gu_ref[1].astype(F32)
        dav = da_ref[...].astype(F32)
        sg = _sigmoid(gv)
        o_ref[0] = (dav * uv * sg * (1.0 + gv * (1.0 - sg))).astype(o_ref.dtype)
        o_ref[1] = (dav * gv * sg).astype(o_ref.dtype)

    pair = BS((2, None, tr, F8), lambda j, i: (0, j, i, 0))
    return pl.pallas_call(
        body, name="swiglu_bwd", grid=(J, S // tr), out_shape=jax.ShapeDtypeStruct(gu.shape, CD),
        in_specs=[BS((None, tr, F8), lambda j, i: (j, i, 0)), pair], out_specs=pair,
        compiler_params=_params("parallel", "parallel"),
    )(da, gu)


def _rope_tables(S):
    freqs = ROPE_THETA ** (-jnp.arange(ROPE_FREQS, dtype=F32) / ROPE_FREQS)
    pos = jnp.arange(S, dtype=jnp.int32)
    row = (pos // GRID_W).astype(F32)[:, None] * freqs
    col = (pos % GRID_W).astype(F32)[:, None] * freqs
    cr, sr, cc, sc = jnp.cos(row), jnp.sin(row), jnp.cos(col), jnp.sin(col)
    return jnp.concatenate([cr, cr, cc, cc], axis=1), jnp.concatenate([-sr, sr, -sc, sc], axis=1)


def _swap_halves(v):
    lane = lax.broadcasted_iota(jnp.int32, v.shape, 1)
    return jnp.where((lane & 32) == 0, pltpu.roll(v, 96, 1), pltpu.roll(v, 32, 1))


def _qk_prep_fwd(qkv, qg, kg, cos_t, sin_t, NQ, NKV):
    S, W = qkv.shape
    tr = _tile(S, 256)
    scale = HEAD_DIM ** -0.5

    def body(qkv_ref, qg_ref, kg_ref, c_ref, s_ref, q_out, k_out):
        C, Sg = c_ref[...], s_ref[...]

        def one(xh, gain, sc):
            xh = xh.astype(F32)
            r = lax.rsqrt(jnp.mean(xh * xh, axis=-1, keepdims=True) + EPS)
            nv = xh * r * gain
            return (nv * C + _swap_halves(nv) * Sg) * sc

        for h in range(NQ):
            sl = slice(h * HEAD_DIM, (h + 1) * HEAD_DIM)
            q_out[:, sl] = one(qkv_ref[:, sl], qg_ref[...], scale).astype(CD)
        for h in range(NKV):
            sl = slice((NQ + h) * HEAD_DIM, (NQ + h + 1) * HEAD_DIM)
            k_out[:, h * HEAD_DIM:(h + 1) * HEAD_DIM] = one(qkv_ref[:, sl], kg_ref[...], 1.0).astype(CD)

    vec = BS((1, HEAD_DIM), lambda i: (0, 0))
    tab = BS((tr, HEAD_DIM), lambda i: (i, 0))
    return pl.pallas_call(
        body, name="qk_prep_fwd", grid=(S // tr,),
        out_shape=[jax.ShapeDtypeStruct((S, NQ * HEAD_DIM), CD), jax.ShapeDtypeStruct((S, NKV * HEAD_DIM), CD)],
        in_specs=[BS((tr, W), lambda i: (i, 0)), vec, vec, tab, tab],
        out_specs=[BS((tr, NQ * HEAD_DIM), lambda i: (i, 0)), BS((tr, NKV * HEAD_DIM), lambda i: (i, 0))],
        compiler_params=_params("parallel"),
    )(qkv, qg, kg, cos_t, sin_t)


def _qk_prep_bwd(qkv, dq, dk, dv, qg, kg, cos_t, sin_t, NQ, NKV):
    S, W = qkv.shape
    tr = _tile(S, 256)
    scale = HEAD_DIM ** -0.5

    def body(qkv_ref, dq_ref, dk_ref, dv_ref, qg_ref, kg_ref, c_ref, s_ref, o_ref, dqg_ref, dkg_ref):
        C, Sg = c_ref[...], s_ref[...]

        def one(xh, dout, gain, sc):
            xh = xh.astype(F32)
            r = lax.rsqrt(jnp.mean(xh * xh, axis=-1, keepdims=True) + EPS)
            yv = xh * r
            dout = dout.astype(F32) * sc
            dn = dout * C + _swap_halves(dout * Sg)
            dy = dn * gain
            dx = r * (dy - yv * jnp.mean(dy * yv, axis=-1, keepdims=True))
            return dx, jnp.sum(dn * yv, axis=0, keepdims=True)

        dqg = jnp.zeros((1, HEAD_DIM), F32)
        dkg = jnp.zeros((1, HEAD_DIM), F32)
        for h in range(NQ):
            sl = slice(h * HEAD_DIM, (h + 1) * HEAD_DIM)
            dx, dg = one(qkv_ref[:, sl], dq_ref[:, sl], qg_ref[...], scale)
            o_ref[:, sl] = dx.astype(CD)
            dqg = dqg + dg
        for h in range(NKV):
            sl = slice((NQ + h) * HEAD_DIM, (NQ + h + 1) * HEAD_DIM)
            dx, dg = one(qkv_ref[:, sl], dk_ref[:, h * HEAD_DIM:(h + 1) * HEAD_DIM], kg_ref[...], 1.0)
            o_ref[:, sl] = dx.astype(CD)
            dkg = dkg + dg
        o_ref[:, (NQ + NKV) * HEAD_DIM:] = dv_ref[...].astype(CD)

        @pl.when(pl.program_id(0) == 0)
        def _():
            dqg_ref[...] = dqg
            dkg_ref[...] = dkg

        @pl.when(pl.program_id(0) > 0)
        def _():
            dqg_ref[...] += dqg
            dkg_ref[...] += dkg

    vec = BS((1, HEAD_DIM), lambda i: (0, 0))
    tab = BS((tr, HEAD_DIM), lambda i: (i, 0))
    full = BS((tr, W), lambda i: (i, 0))
    return pl.pallas_call(
        body, name="qk_prep_bwd", grid=(S // tr,),
        out_shape=[jax.ShapeDtypeStruct((S, W), CD), jax.ShapeDtypeStruct((1, HEAD_DIM), F32),
                   jax.ShapeDtypeStruct((1, HEAD_DIM), F32)],
        in_specs=[full, BS((tr, NQ * HEAD_DIM), lambda i: (i, 0)), BS((tr, NKV * HEAD_DIM), lambda i: (i, 0)),
                  BS((tr, NKV * HEAD_DIM), lambda i: (i, 0)), vec, vec, tab, tab],
        out_specs=[full, vec, vec], compiler_params=_params("arbitrary"),
    )(qkv, dq, dk, dv, qg, kg, cos_t, sin_t)


def _lane_pack(cols):
    rows = cols[0].shape[0]
    lane = lax.broadcasted_iota(jnp.int32, (rows, HEAD_DIM), 1)
    out = jnp.zeros((rows, HEAD_DIM), F32)
    for g, col in enumerate(cols):
        out = jnp.where(lane == g, col, out)
    return out


def _flash_fwd(qr, kr, qkv, NQ, NKV):
    S = qr.shape[0]
    G = NQ // NKV
    tq, tk = _tile(S, 512), _tile(S, 512)
    nk = S // tk
    GW = G * HEAD_DIM

    def body(q_ref, k_ref, v_ref, o_ref, lse_ref, m_s, acc_s):
        ki = pl.program_id(2)

        @pl.when(ki == 0)
        def _():
            m_s[...] = jnp.full(m_s.shape, -1e30, F32)
            acc_s[...] = jnp.zeros(acc_s.shape, F32)

        kv = k_ref[...]
        v_ones = jnp.concatenate([v_ref[...], jnp.ones((tk, HEAD_DIM), CD)], axis=1)
        for g in range(G):
            s = lax.dot_general(q_ref[:, g * HEAD_DIM:(g + 1) * HEAD_DIM], kv, NT, preferred_element_type=F32)
            m_prev = m_s[g]
            m_new = jnp.maximum(m_prev, jnp.max(s, axis=-1, keepdims=True))
            alpha = jnp.exp(m_prev - m_new)
            p = jnp.exp(s - jnp.tile(m_new, (1, tk // HEAD_DIM)))
            acc_s[g] = jnp.tile(alpha, (1, 2)) * acc_s[g] + jnp.dot(p.astype(CD), v_ones, preferred_element_type=F32)
            m_s[g] = m_new

        @pl.when(ki == nk - 1)
        def _():
            lane = lax.broadcasted_iota(jnp.int32, (tq, HEAD_DIM), 1)
            lse = jnp.zeros((tq, HEAD_DIM), F32)
            for g in range(G):
                l = acc_s[g, :, HEAD_DIM:]
                o_ref[:, g * HEAD_DIM:(g + 1) * HEAD_DIM] = (acc_s[g, :, :HEAD_DIM] / l).astype(o_ref.dtype)
                lse = jnp.where(lane == g, m_s[g] + jnp.log(l), lse)
            lse_ref[...] = lse

    return pl.pallas_call(
        body, name="flash_fwd", grid=(NKV, S // tq, nk),
        out_shape=[jax.ShapeDtypeStruct((S, NQ * HEAD_DIM), CD), jax.ShapeDtypeStruct((NKV, S, HEAD_DIM), F32)],
        in_specs=[BS((tq, GW), lambda h, i, k: (i, h)), BS((tk, HEAD_DIM), lambda h, i, k: (k, h)),
                  BS((tk, HEAD_DIM), lambda h, i, k: (k, NQ + NKV + h))],
        out_specs=[BS((tq, GW), lambda h, i, k: (i, h)), BS((None, tq, HEAD_DIM), lambda h, i, k: (h, i, 0))],
        scratch_shapes=[pltpu.VMEM((G, tq, HEAD_DIM), F32), pltpu.VMEM((G, tq, 2 * HEAD_DIM), F32)],
        compiler_params=_params("parallel", "parallel", "arbitrary"),
    )(qr, kr, qkv)


def _attn_delta(do, o, NQ, NKV):
    S = o.shape[0]
    G = NQ // NKV
    tr = _tile(S, 512)

    def body(do_ref, o_ref, d_ref):
        for kv in range(NKV):
            cols = []
            for g in range(G):
                sl = slice((kv * G + g) * HEAD_DIM, (kv * G + g + 1) * HEAD_DIM)
                cols.append(jnp.sum(do_ref[:, sl].astype(F32) * o_ref[:, sl].astype(F32), axis=-1, keepdims=True))
            d_ref[kv] = _lane_pack(cols)

    row = BS((tr, NQ * HEAD_DIM), lambda i: (i, 0))
    return pl.pallas_call(
        body, name="attn_delta", grid=(S // tr,), out_shape=jax.ShapeDtypeStruct((NKV, S, HEAD_DIM), F32),
        in_specs=[row, row], out_specs=BS((NKV, tr, HEAD_DIM), lambda i: (0, i, 0)),
        compiler_params=_params("parallel"),
    )(do, o)


def _flash_dq(qr, kr, qkv, do, lse, delta, NQ, NKV):
    S = qr.shape[0]
    G = NQ // NKV
    tq, tk = _tile(S, 512), _tile(S, 512)
    nk = S // tk
    GW = G * HEAD_DIM

    def body(q_ref, k_ref, v_ref, do_ref, lse_ref, dl_ref, dq_ref, acc_s):
        ki = pl.program_id(2)

        @pl.when(ki == 0)
        def _():
            acc_s[...] = jnp.zeros(acc_s.shape, F32)

        kv, vv = k_ref[...], v_ref[...]
        for g in range(G):
            sl = slice(g * HEAD_DIM, (g + 1) * HEAD_DIM)
            s = lax.dot_general(q_ref[:, sl], kv, NT, preferred_element_type=F32)
            p = jnp.exp(s - lse_ref[:, g:g + 1])
            dp = lax.dot_general(do_ref[:, sl], vv, NT, preferred_element_type=F32)
            ds = p * (dp - dl_ref[:, g:g + 1])
            acc_s[g] += jnp.dot(ds.astype(CD), kv, preferred_element_type=F32)

        @pl.when(ki == nk - 1)
        def _():
            for g in range(G):
                dq_ref[:, g * HEAD_DIM:(g + 1) * HEAD_DIM] = acc_s[g].astype(dq_ref.dtype)

    qb = BS((tq, GW), lambda h, i, k: (i, h))
    st = BS((None, tq, HEAD_DIM), lambda h, i, k: (h, i, 0))
    return pl.pallas_call(
        body, name="flash_dq", grid=(NKV, S // tq, nk), out_shape=jax.ShapeDtypeStruct((S, NQ * HEAD_DIM), CD),
        in_specs=[qb, BS((tk, HEAD_DIM), lambda h, i, k: (k, h)), BS((tk, HEAD_DIM), lambda h, i, k: (k, NQ + NKV + h)),
                  qb, st, st],
        out_specs=qb, scratch_shapes=[pltpu.VMEM((G, tq, HEAD_DIM), F32)],
        compiler_params=_params("parallel", "parallel", "arbitrary"),
    )(qr, kr, qkv, do, lse, delta)


def _flash_dkv(qr, kr, qkv, do, lse_rows, delta_rows, NQ, NKV):
    S = qr.shape[0]
    G = NQ // NKV
    tq, tk = _tile(S, 512), _tile(S, 512)
    nq = S // tq
    GW = G * HEAD_DIM

    def body(q_ref, k_ref, v_ref, do_ref, lse_ref, dl_ref, dk_ref, dv_ref, dk_s, dv_s):
        qi = pl.program_id(2)

        @pl.when(qi == 0)
        def _():
            dk_s[...] = jnp.zeros(dk_s.shape, F32)
            dv_s[...] = jnp.zeros(dv_s.shape, F32)

        kv, vv = k_ref[...], v_ref[...]
        for g in range(G):
            sl = slice(g * HEAD_DIM, (g + 1) * HEAD_DIM)
            qg, dog = q_ref[:, sl], do_ref[:, sl]
            st = lax.dot_general(kv, qg, NT, preferred_element_type=F32)
            pt = jnp.exp(st - lse_ref[g:g + 1, :])
            dv_s[...] += jnp.dot(pt.astype(CD), dog, preferred_element_type=F32)
            dpt = lax.dot_general(vv, dog, NT, preferred_element_type=F32)
            dst = pt * (dpt - dl_ref[g:g + 1, :])
            dk_s[...] += jnp.dot(dst.astype(CD), qg, preferred_element_type=F32)

        @pl.when(qi == nq - 1)
        def _():
            dk_ref[...] = dk_s[...].astype(dk_ref.dtype)
            dv_ref[...] = dv_s[...].astype(dv_ref.dtype)

    qb = BS((tq, GW), lambda h, k, i: (i, h))
    st = BS((None, G, tq), lambda h, k, i: (h, 0, i))
    kb = BS((tk, HEAD_DIM), lambda h, k, i: (k, h))
    return pl.pallas_call(
        body, name="flash_dkv", grid=(NKV, S // tk, nq),
        out_shape=[jax.ShapeDtypeStruct((S, NKV * HEAD_DIM), CD), jax.ShapeDtypeStruct((S, NKV * HEAD_DIM), CD)],
        in_specs=[qb, kb, BS((tk, HEAD_DIM), lambda h, k, i: (k, NQ + NKV + h)), qb, st, st],
        out_specs=[kb, kb], scratch_shapes=[pltpu.VMEM((tk, HEAD_DIM), F32), pltpu.VMEM((tk, HEAD_DIM), F32)],
        compiler_params=_params("parallel", "parallel", "arbitrary"),
    )(qr, kr, qkv, do, lse_rows, delta_rows)


def _flash_bwd(qr, kr, qkv, do, lse_rows, delta_rows, NQ, NKV):
    S = qr.shape[0]
    G = NQ // NKV
    tq, tk = _tile(S, 512), _tile(S, 512)
    nq, nk = S // tq, S // tk
    GW = G * HEAD_DIM

    def body(q_ref, k_ref, v_ref, do_ref, lse_ref, dl_ref, dq_ref, dk_ref, dv_ref, dq_s, dk_s, dv_s):
        ki, qi = pl.program_id(1), pl.program_id(2)
        rows = pl.ds(pl.multiple_of(qi * tq, tq), tq)

        @pl.when(qi == 0)
        def _():
            dk_s[...] = jnp.zeros(dk_s.shape, F32)
            dv_s[...] = jnp.zeros(dv_s.shape, F32)

        kv, vv = k_ref[...], v_ref[...]
        for g in range(G):
            sl = slice(g * HEAD_DIM, (g + 1) * HEAD_DIM)
            qg, dog = q_ref[:, sl], do_ref[:, sl]
            st = lax.dot_general(kv, qg, NT, preferred_element_type=F32)
            pt = jnp.exp(st - lse_ref[g:g + 1, :])
            dv_s[...] += jnp.dot(pt.astype(CD), dog, preferred_element_type=F32)
            dpt = lax.dot_general(vv, dog, NT, preferred_element_type=F32)
            dst = (pt * (dpt - dl_ref[g:g + 1, :])).astype(CD)
            dk_s[...] += jnp.dot(dst, qg, preferred_element_type=F32)
            dq_part = lax.dot_general(dst, kv, TN, preferred_element_type=F32)

            @pl.when(ki == 0)
            def _():
                dq_s[g, rows, :] = dq_part

            @pl.when(ki > 0)
            def _():
                dq_s[g, rows, :] += dq_part

        @pl.when(qi == nq - 1)
        def _():
            dk_ref[...] = dk_s[...].astype(dk_ref.dtype)
            dv_ref[...] = dv_s[...].astype(dv_ref.dtype)

        @pl.when(ki == nk - 1)
        def _():
            for g in range(G):
                dq_ref[rows, g * HEAD_DIM:(g + 1) * HEAD_DIM] = dq_s[g, rows, :].astype(dq_ref.dtype)

    qb = BS((tq, GW), lambda h, k, i: (i, h))
    st = BS((None, G, tq), lambda h, k, i: (h, 0, i))
    kb = BS((tk, HEAD_DIM), lambda h, k, i: (k, h))
    return pl.pallas_call(
        body, name="flash_bwd", grid=(NKV, nk, nq),
        out_shape=[jax.ShapeDtypeStruct((S, NQ * HEAD_DIM), CD), jax.ShapeDtypeStruct((S, NKV * HEAD_DIM), CD),
                   jax.ShapeDtypeStruct((S, NKV * HEAD_DIM), CD)],
        in_specs=[qb, kb, BS((tk, HEAD_DIM), lambda h, k, i: (k, NQ + NKV + h)), qb, st, st],
        out_specs=[BS((S, GW), lambda h, k, i: (0, h)), kb, kb],
        scratch_shapes=[pltpu.VMEM((G, S, HEAD_DIM), F32), pltpu.VMEM((tk, HEAD_DIM), F32), pltpu.VMEM((tk, HEAD_DIM), F32)],
        compiler_params=_params("parallel", "arbitrary", "arbitrary"),
    )(qr, kr, qkv, do, lse_rows, delta_rows)


HALO8 = 8


def _round(v):
    return v.astype(CD).astype(F32)


def _shift_down(v, k, above):
    T = v.shape[0]
    row = lax.broadcasted_iota(jnp.int32, v.shape, 0)
    out = pltpu.roll(v, k, 0)
    for j in range(k):
        out = jnp.where(row == j, above[above.shape[0] - k + j:above.shape[0] - k + j + 1, :], out)
    return out


def _shift_up(v, k, below):
    T = v.shape[0]
    row = lax.broadcasted_iota(jnp.int32, v.shape, 0)
    out = pltpu.roll(v, T - k, 0)
    for j in range(k):
        out = jnp.where(row == T - k + j, below[j:j + 1, :], out)
    return out


def _halo_specs(T, C, halo, n_chunks, S, col=0, rev=False):
    per = T // halo
    last = S // halo - 1
    ch = (lambda i: n_chunks - 1 - i) if rev else (lambda i: i)
    return (BS((T, C), lambda i: (ch(i), col)),
            BS((halo, C), lambda i: (jnp.maximum(ch(i) * per - 1, 0), col)),
            BS((halo, C), lambda i: (jnp.minimum((ch(i) + 1) * per, last), col)))


def _conv_fwd(xy, w8, bias):
    S, C2 = xy.shape
    C = C2 // 2
    T = _tile(S, 256)
    n = S // T

    def body(x_ref, p_ref, q_ref, w_ref, b_ref, o_ref):
        i = pl.program_id(0)
        xv = _round(x_ref[...])
        above = jnp.where(i > 0, _round(p_ref[...]), 0.0)
        below = jnp.where(i < n - 1, _round(q_ref[...]), 0.0)
        wv = _round(w_ref[...])
        o_ref[...] = (wv[0:1, :] * _shift_down(xv, 2, above) + wv[1:2, :] * _shift_down(xv, 1, above)
                      + wv[2:3, :] * xv + wv[3:4, :] * _shift_up(xv, 1, below) + b_ref[...])

    main, prev, nxt = _halo_specs(T, C, HALO8, n, S)
    return pl.pallas_call(
        body, name="conv_fwd", grid=(n,), out_shape=jax.ShapeDtypeStruct((S, C), F32),
        in_specs=[main, prev, nxt, BS((8, C), lambda i: (0, 0)), BS((1, C), lambda i: (0, 0))],
        out_specs=BS((T, C), lambda i: (i, 0)), compiler_params=_params("parallel"),
    )(xy, xy, xy, w8, bias)


def _tile_scan(a, b, reverse):
    row = lax.broadcasted_iota(jnp.int32, a.shape, 0)
    for sh in (1, 2, 4):
        if reverse:
            a_s, b_s, ok = pltpu.roll(a, 8 - sh, 0), pltpu.roll(b, 8 - sh, 0), row < 8 - sh
        else:
            a_s, b_s, ok = pltpu.roll(a, sh, 0), pltpu.roll(b, sh, 0), row >= sh
        b = jnp.where(ok, a * b_s + b, b)
        a = jnp.where(ok, a * a_s, a)
    return a, b


def _scan_chunk(a_ref, h_ref, carry_ref, reverse):
    T = h_ref.shape[0]
    nt = T // 8

    def step(j, carry):
        t = (nt - 1 - j) if reverse else j
        rows = pl.ds(pl.multiple_of(t * 8, 8), 8)
        ca, hb = _tile_scan(a_ref[rows, :], h_ref[rows, :], reverse)
        hv = hb + ca * carry
        h_ref[rows, :] = hv
        edge = hv[0:1, :] if reverse else hv[7:8, :]
        return jnp.broadcast_to(edge, hv.shape)

    carry_ref[...] = lax.fori_loop(0, nt, step, carry_ref[...])


def _softplus(z):
    return jnp.maximum(z, 0.0) + jnp.log(1.0 + jnp.exp(-jnp.abs(z)))


def _neg_expm1(z):
    return jnp.where(z > -0.01, -z * (1.0 + z * (0.5 + z * (1.0 / 6.0 + z * (1.0 / 24.0)))), 1.0 - jnp.exp(z))


def _gates(xs, wa, wi, ba, bi, sp):
    r = _sigmoid(jnp.dot(xs, wa, preferred_element_type=F32) + ba)
    ig = _sigmoid(jnp.dot(xs, wi, preferred_element_type=F32) + bi)
    la = -LRU_C * r * sp
    return r, ig, jnp.exp(la), jnp.sqrt(_neg_expm1(2.0 * la))


def _lru_fwd(name, xc, wai, d, ba, bi, lam, reverse):
    S, C = xc.shape
    NB, BW = wai.shape[2], wai.shape[3]
    T = _tile(S, 256)
    n = S // T
    ch = (lambda i: n - 1 - i) if reverse else (lambda i: i)

    def body(x_ref, wa_ref, wi_ref, ba_ref, bi_ref, lam_ref, h_ref, a_ref, carry):
        @pl.when(pl.program_id(0) == 0)
        def _():
            carry[...] = jnp.zeros(carry.shape, F32)

        for b in range(NB):
            sl = slice(b * BW, (b + 1) * BW)
            xf = x_ref[:, sl]
            _, ig, av, sq = _gates(xf.astype(CD), wa_ref[b], wi_ref[b], ba_ref[:, sl], bi_ref[:, sl],
                                   _softplus(-lam_ref[:, sl]))
            a_ref[:, sl] = av
            h_ref[:, sl] = sq * ig * xf
        _scan_chunk(a_ref, h_ref, carry, reverse)

    vec = BS((1, C), lambda i: (0, 0))
    blk = BS((T, C), lambda i: (ch(i), 0))
    return pl.pallas_call(
        body, name=name, grid=(n,),
        out_shape=[jax.ShapeDtypeStruct((S, C), F32), jax.ShapeDtypeStruct((S, C), F32)],
        in_specs=[blk, BS((None, None, NB, BW, BW), lambda i: (0, d, 0, 0, 0)),
                  BS((None, None, NB, BW, BW), lambda i: (1, d, 0, 0, 0)), vec, vec, vec],
        out_specs=[blk, blk], scratch_shapes=[pltpu.VMEM((8, C), F32)], compiler_params=_params("arbitrary"),
    )(xc, wai, wai, ba, bi, lam)


_GELU_C = math.sqrt(2.0 / math.pi)


def _rnn_out_fwd(h0, h1, xy):
    S, C = h0.shape
    tr = _tile(S, 256)

    def body(h0_ref, h1_ref, y_ref, o_ref):
        yv = y_ref[...].astype(F32)
        gl = 0.5 * yv * (1.0 + jnp.tanh(_GELU_C * (yv + 0.044715 * yv * yv * yv)))
        o_ref[...] = ((h0_ref[...] + h1_ref[...]) * gl).astype(o_ref.dtype)

    row = BS((tr, C), lambda i: (i, 0))
    return pl.pallas_call(
        body, name="rnn_out_fwd", grid=(S // tr,), out_shape=jax.ShapeDtypeStruct((S, C), CD),
        in_specs=[row, row, BS((tr, C), lambda i: (i, 1))], out_specs=row, compiler_params=_params("parallel"),
    )(h0, h1, xy)


def _rnn_out_bwd(dout, h0, h1, xy):
    S, C = h0.shape
    tr = _tile(S, 256)

    def body(d_ref, h0_ref, h1_ref, y_ref, dh_ref, dy_ref):
        yv = y_ref[...].astype(F32)
        dv = d_ref[...].astype(F32)
        t = jnp.tanh(_GELU_C * (yv + 0.044715 * yv * yv * yv))
        gl = 0.5 * yv * (1.0 + t)
        dgl = 0.5 * (1.0 + t) + 0.5 * yv * (1.0 - t * t) * _GELU_C * (1.0 + 3.0 * 0.044715 * yv * yv)
        dh_ref[...] = dv * gl
        dy_ref[...] = (dv * (h0_ref[...] + h1_ref[...]) * dgl).astype(dy_ref.dtype)

    row = BS((tr, C), lambda i: (i, 0))
    return pl.pallas_call(
        body, name="rnn_out_bwd", grid=(S // tr,),
        out_shape=[jax.ShapeDtypeStruct((S, C), F32), jax.ShapeDtypeStruct((S, C), CD)],
        in_specs=[row, row, row, BS((tr, C), lambda i: (i, 1))], out_specs=[row, row],
        compiler_params=_params("parallel"),
    )(dout, h0, h1, xy)


def _lru_bwd(name, dhs, a, h, xc, wai, d, ba, bi, lam, reverse, dxc_in):
    S, C = xc.shape
    NB, BW = wai.shape[2], wai.shape[3]
    T = _tile(S, 256)
    n = S // T
    back = not reverse

    def body(*refs):
        (dh_ref, a_ref, a_prev, a_next, h_ref, h_prev, h_next, x_ref, wa_ref, wi_ref, ba_ref, bi_ref,
         lam_ref) = refs[:13]
        k = 13
        add_ref = None
        if dxc_in is not None:
            add_ref = refs[k]
            k += 1
        dx_ref, dwa_ref, dwi_ref, dba_ref, dbi_ref, dlam_ref, coef_s, lam_s, carry, dsp_s = refs[k:]
        i = pl.program_id(0)
        first_chunk = i == (n - 1 if back else 0)
        last_chunk = i == (0 if back else n - 1)

        @pl.when(i == 0)
        def _():
            carry[...] = jnp.zeros(carry.shape, F32)
            dsp_s[...] = jnp.zeros(dsp_s.shape, F32)
            dwa_ref[...] = jnp.zeros(dwa_ref.shape, F32)
            dwi_ref[...] = jnp.zeros(dwi_ref.shape, F32)
            dba_ref[...] = jnp.zeros(dba_ref.shape, F32)
            dbi_ref[...] = jnp.zeros(dbi_ref.shape, F32)

        av, hv = a_ref[...], h_ref[...]
        if reverse:
            coef_s[...] = _shift_down(av, 1, a_prev[...])
            h_nb = _shift_up(hv, 1, jnp.where(last_chunk, 0.0, h_next[...]))
        else:
            coef_s[...] = _shift_up(av, 1, a_next[...])
            h_nb = _shift_down(hv, 1, jnp.where(first_chunk, 0.0, h_prev[...]))
        lam_s[...] = dh_ref[...]
        _scan_chunk(coef_s, lam_s, carry, back)

        for b in range(NB):
            sl = slice(b * BW, (b + 1) * BW)
            xf = x_ref[:, sl]
            xs = xf.astype(CD)
            sp = _softplus(-lam_ref[:, sl])
            r, ig, ab, sq = _gates(xs, wa_ref[b], wi_ref[b], ba_ref[:, sl], bi_ref[:, sl], sp)
            du = lam_s[:, sl]
            da = du * h_nb[:, sl]
            d_ig = du * sq * xf
            d_sq = du * ig * xf
            d_la = da * ab - d_sq * (ab * ab) / sq
            d_pa = d_la * (-LRU_C * sp) * r * (1.0 - r)
            d_pi = d_ig * ig * (1.0 - ig)
            dsp_s[:, sl] += jnp.sum(d_la * (-LRU_C * r), axis=0, keepdims=True)
            dba_ref[:, sl] += jnp.sum(d_pa, axis=0, keepdims=True)
            dbi_ref[:, sl] += jnp.sum(d_pi, axis=0, keepdims=True)
            d_pa_c, d_pi_c = d_pa.astype(CD), d_pi.astype(CD)
            dwa_ref[b] += lax.dot_general(xs, d_pa_c, TN, preferred_element_type=F32)
            dwi_ref[b] += lax.dot_general(xs, d_pi_c, TN, preferred_element_type=F32)
            dx = (du * sq * ig + lax.dot_general(d_pa_c, wa_ref[b], NT, preferred_element_type=F32)
                  + lax.dot_general(d_pi_c, wi_ref[b], NT, preferred_element_type=F32))
            if add_ref is not None:
                dx = dx + add_ref[:, sl]
            dx_ref[:, sl] = dx

        @pl.when(i == n - 1)
        def _():
            dlam_ref[...] = -dsp_s[...] * _sigmoid(-lam_ref[...])

    vec = BS((1, C), lambda i: (0, 0))
    main, prev, nxt = _halo_specs(T, C, HALO8, n, S, rev=back)
    wspec = lambda which: BS((None, None, NB, BW, BW), lambda i: (which, d, 0, 0, 0))
    acc = BS((NB, BW, BW), lambda i: (0, 0, 0))
    ins = [dhs, a, a, a, h, h, h, xc, wai, wai, ba, bi, lam]
    in_specs = [main, main, prev, nxt, main, prev, nxt, main, wspec(0), wspec(1), vec, vec, vec]
    if dxc_in is not None:
        ins.append(dxc_in)
        in_specs.append(main)
    return pl.pallas_call(
        body, name=name, grid=(n,),
        out_shape=[jax.ShapeDtypeStruct((S, C), F32), jax.ShapeDtypeStruct((NB, BW, BW), F32),
                   jax.ShapeDtypeStruct((NB, BW, BW), F32), jax.ShapeDtypeStruct((1, C), F32),
                   jax.ShapeDtypeStruct((1, C), F32), jax.ShapeDtypeStruct((1, C), F32)],
        in_specs=in_specs, out_specs=[main, acc, acc, vec, vec, vec],
        scratch_shapes=[pltpu.VMEM((T, C), F32), pltpu.VMEM((T, C), F32), pltpu.VMEM((8, C), F32),
                        pltpu.VMEM((1, C), F32)],
        compiler_params=_params("arbitrary"),
    )(*ins)


def _conv_bwd(dxc, xy, w8):
    S, C = dxc.shape
    T = _tile(S, 256)
    n = S // T

    def body(d_ref, dp_ref, dn_ref, x_ref, xp_ref, xn_ref, w_ref, dx_ref, dw_ref):
        i = pl.program_id(0)
        dv = _round(d_ref[...])
        d_above = jnp.where(i > 0, _round(dp_ref[...]), 0.0)
        d_below = jnp.where(i < n - 1, _round(dn_ref[...]), 0.0)
        wv = _round(w_ref[...])
        dx = (wv[0:1, :] * _shift_up(dv, 2, d_below) + wv[1:2, :] * _shift_up(dv, 1, d_below)
              + wv[2:3, :] * dv + wv[3:4, :] * _shift_down(dv, 1, d_above))
        dx_ref[...] = dx.astype(dx_ref.dtype)
        xv = _round(x_ref[...])
        above = jnp.where(i > 0, _round(xp_ref[...]), 0.0)
        below = jnp.where(i < n - 1, _round(xn_ref[...]), 0.0)
        rows = [jnp.sum(dv * _shift_down(xv, 2, above), axis=0, keepdims=True),
                jnp.sum(dv * _shift_down(xv, 1, above), axis=0, keepdims=True),
                jnp.sum(dv * xv, axis=0, keepdims=True),
                jnp.sum(dv * _shift_up(xv, 1, below), axis=0, keepdims=True),
                jnp.sum(dv, axis=0, keepdims=True)]
        rid = lax.broadcasted_iota(jnp.int32, (8, C), 0)
        part = jnp.zeros((8, C), F32)
        for j, rv in enumerate(rows):
            part = jnp.where(rid == j, rv, part)

        @pl.when(i == 0)
        def _():
            dw_ref[...] = part

        @pl.when(i > 0)
        def _():
            dw_ref[...] += part

    dmain, dprev, dnext = _halo_specs(T, C, HALO8, n, S)
    xmain, xprev, xnext = _halo_specs(T, C, HALO8, n, S)
    w = BS((8, C), lambda i: (0, 0))
    return pl.pallas_call(
        body, name="conv_bwd", grid=(n,),
        out_shape=[jax.ShapeDtypeStruct((S, C), CD), jax.ShapeDtypeStruct((8, C), F32)],
        in_specs=[dmain, dprev, dnext, xmain, xprev, xnext, w], out_specs=[dmain, w],
        compiler_params=_params("arbitrary"),
    )(dxc, dxc, dxc, xy, xy, xy, w8)


def _adamw(name, lands, w, m, v, pick=None):
    c1 = 1.0 - ADAM_B1 ** ADAM_STEP
    c2 = 1.0 - ADAM_B2 ** ADAM_STEP
    L, C = w.shape[0], w.shape[-1]
    w3, m3, v3 = (t.reshape(L, -1, C) for t in (w, m, v))
    R = w3.shape[1]
    tr = R
    for cand in (1024, 512, 256, 128, 64, 32, 16, 8):
        if R % cand == 0 and cand * C * 4 <= (1 << 20):
            tr = cand
            break

    def body(*refs):
        l_refs = refs[:L]
        w_ref, m_ref, v_ref, g_out, d_out, m_out, v_out = refs[L:]

        def update(l_ref):
            g = l_ref[0].astype(F32)
            for s in range(1, NDEV):
                g = g + l_ref[s].astype(F32)
            mn = ADAM_B1 * m_ref[...] + (1.0 - ADAM_B1) * g
            vn = ADAM_B2 * v_ref[...] + (1.0 - ADAM_B2) * (g * g)
            g_out[...] = g
            m_out[...] = mn
            v_out[...] = vn
            d_out[...] = -ADAM_LR * ((mn / c1) / (jnp.sqrt(vn / c2) + ADAM_EPS) + ADAM_WD * w_ref[...])

        for k in range(L):
            pl.when(pl.program_id(0) == k)(functools.partial(update, l_refs[k]))

    if pick is None:
        lands = [t.reshape(NDEV, R, C) for t in lands]
        land_specs = [BS((NDEV, tr, C), lambda l, i, k=k: (0, jnp.where(l == k, i, 0), 0)) for k in range(L)]
    else:
        lands = [t.reshape(NDEV, -1, R, C) for t in lands]
        land_specs = [BS((NDEV, None, tr, C), lambda l, i, k=k: (0, pick, jnp.where(l == k, i, 0), 0)) for k in range(L)]
    spec = BS((None, tr, C), lambda l, i: (l, i, 0))
    sds = jax.ShapeDtypeStruct(w3.shape, F32)
    outs = pl.pallas_call(
        body, name=name, grid=(L, R // tr), out_shape=[sds] * 4, in_specs=land_specs + [spec, spec, spec],
        out_specs=[spec] * 4, compiler_params=_params("arbitrary", "arbitrary"),
    )(*lands, w3, m3, v3)
    return [o.reshape(w.shape) for o in outs]


def kernel(x, norm_mix, norm_ffn, attn_w_qkv, attn_q_gain, attn_k_gain, attn_w_o, rnn_w_in, rnn_conv_w, rnn_conv_b, rnn_w_a, rnn_b_a, rnn_w_i, rnn_b_i, rnn_lambda, rnn_w_out, ffn_w_gate, ffn_w_up, ffn_w_down, loss_target, m_norm_mix, m_norm_ffn, m_attn_w_qkv, m_attn_q_gain, m_attn_k_gain, m_attn_w_o, m_rnn_w_in, m_rnn_conv_w, m_rnn_conv_b, m_rnn_w_a, m_rnn_b_a, m_rnn_w_i, m_rnn_b_i, m_rnn_lambda, m_rnn_w_out, m_ffn_w_gate, m_ffn_w_up, m_ffn_w_down, v_norm_mix, v_norm_ffn, v_attn_w_qkv, v_attn_q_gain, v_attn_k_gain, v_attn_w_o, v_rnn_w_in, v_rnn_conv_w, v_rnn_conv_b, v_rnn_w_a, v_rnn_b_a, v_rnn_w_i, v_rnn_b_i, v_rnn_lambda, v_rnn_w_out, v_ffn_w_gate, v_ffn_w_up, v_ffn_w_down):
    S, D = x.shape[1], x.shape[2]
    depth = norm_mix.shape[0]
    n_attn, n_rnn = attn_w_qkv.shape[0], rnn_w_in.shape[0]
    QS = attn_w_qkv.shape[2]
    NQ = D // HEAD_DIM
    NKV = (QS * NDEV // HEAD_DIM - NQ) // 2
    G = NQ // NKV
    OS = attn_w_o.shape[1]
    IS = rnn_w_in.shape[2]
    C = rnn_w_out.shape[1] * NDEV
    CS = C // NDEV
    NB, BWS, BW = rnn_w_a.shape[2], rnn_w_a.shape[3], rnn_w_a.shape[4]
    F8 = ffn_w_gate.shape[2]
    tm = _tile(S, 1024)
    tn = _tile(D, 1024)
    tk = _tile(S, 2048)

    def rows_at(parts, total):
        out = None
        for at, p in parts:
            cfg = [(0, 0)] * (p.ndim - 2) + [(at, total - at - p.shape[-2]), (0, 0)]
            out = jnp.pad(p, cfg) if out is None else out + jnp.pad(p, cfg)
        return out

    def pack_small(cw, cb, b_a, b_i, lam):
        return rows_at([(0, cw), (8, cb[:, None, :]), (16, b_a), (24, b_i), (32, lam)], SMALL_ROWS)

    def unpack_small(p):
        return p[:, 0:4], p[:, 8], p[:, 16:18], p[:, 24:26], p[:, 32:34]

    gu_sh = jnp.stack([ffn_w_gate, ffn_w_up], axis=1).astype(CD)
    ai_sh = jnp.stack([rnn_w_a, rnn_w_i], axis=1).astype(CD)
    small_sh = pack_small(rnn_conv_w, rnn_conv_b, rnn_b_a, rnn_b_i, rnn_lambda)
    Wqkv0, Wo0 = _exchange("gather_first", [attn_w_qkv[0:1].astype(CD), attn_w_o[0:1].astype(CD)], scatter=False)
    ffn0_handle, ffn0_token = _exchange_start("gather_ffn0_start", [gu_sh[0:1], ffn_w_down[0:1].astype(CD)], scatter=False,
                                              after=Wqkv0)
    rest_handle, rest_token = _exchange_start(
        "gather_rest_start",
        [attn_w_qkv[1:].astype(CD), attn_w_o[1:].astype(CD), rnn_w_in.astype(CD), rnn_w_out.astype(CD), ai_sh, gu_sh[1:],
         ffn_w_down[1:].astype(CD), small_sh], scatter=False, after=ffn0_token)
    started = ffn0_token[0, 0] + rest_token[0, 0]
    wqkv, wo = {0: (Wqkv0, 0)}, {0: (Wo0, 0)}
    win, wout, wgu, wd = {}, {}, {}, {}
    cos_t, sin_t = _rope_tables(S)

    def proj_cols(name, a, W, l, width, dtype):
        return _mm(name, NN, a, W, BS((tm, D), lambda i, j, r: (i, 0)),
                   BS((None, None, D, width), lambda i, j, r: (j, l, 0, 0)),
                   BS((tm, width), lambda i, j, r: (i, j)), (S, NDEV * width), dtype, (S // tm, NDEV, 1))

    def proj_cols_dx(name, dy, W, l, width):
        ns = 4
        return _mm(name, NT, dy, W, BS((tm, ns * width), lambda i, j, r: (i, r)),
                   BS((ns, None, tn, width), lambda i, j, r: (r, l, j, 0)),
                   BS((tm, tn), lambda i, j, r: (i, j)), (S, D), F32, (S // tm, D // tn, NDEV // ns),
                   pairs=lambda a_ref, b_ref: [(a_ref[:, s * width:(s + 1) * width], b_ref[s]) for s in range(ns)])

    def proj_cols_dw(name, a, dy, width):
        return _mm(name, TN, a, dy, BS((tk, tn), lambda i, j, r: (r, i)),
                   BS((tk, width), lambda i, j, r: (r, j)),
                   BS((None, tn, width), lambda i, j, r: (j, i, 0)), (NDEV, D, width), CD, (D // tn, NDEV, S // tk))

    def proj_rows(name, a, W, l, rows, resid):
        return _mm(name, NN, a, W, BS((tm, NDEV * rows), lambda i, j, r: (i, 0)),
                   BS((NDEV, None, rows, tn), lambda i, j, r: (0, l, 0, j)),
                   BS((tm, tn), lambda i, j, r: (i, j)), (S, D), F32, (S // tm, D // tn, 1), add=resid,
                   pairs=lambda a_ref, b_ref: [(a_ref[...], b_ref[...].reshape(NDEV * rows, tn))])

    def proj_rows_dx(name, dy, W, l, rows, dtype):
        return _mm(name, NT, dy, W, BS((tm, D), lambda i, j, r: (i, 0)),
                   BS((None, None, rows, D), lambda i, j, r: (j, l, 0, 0)),
                   BS((tm, rows), lambda i, j, r: (i, j)), (S, NDEV * rows), dtype, (S // tm, NDEV, 1))

    def proj_rows_dw(name, a, dy, rows):
        width = NDEV * rows
        tw = _tile(width, 1024)
        out = _mm(name, TN, a, dy, BS((tk, tw), lambda i, j, r: (r, i)), BS((tk, tn), lambda i, j, r: (r, j)),
                  BS((tw, tn), lambda i, j, r: (i, j)), (width, D), CD, (width // tw, D // tn, S // tk))
        return out.reshape(NDEV, rows, D)

    saved = []
    xs = x[0]
    for i in range(depth):
        l = i // 2
        h1 = _rms_fwd("norm_mix_fwd", xs, norm_mix[i:i + 1] + started if i == 0 else norm_mix[i:i + 1])
        if i % 2 == 0:
            qkv = proj_cols("attn_qkv", h1, *wqkv[l], QS, CD)
            qr, kr = _qk_prep_fwd(qkv, attn_q_gain[l:l + 1], attn_k_gain[l:l + 1], cos_t, sin_t, NQ, NKV)
            o, lse = _flash_fwd(qr, kr, qkv, NQ, NKV)
            x1 = proj_rows("attn_out", o, *wo[l], OS, xs)
            mix = (qkv, qr, kr, o, lse)
        else:
            xy = proj_cols("rnn_in", h1, *win[l], IS, F32)
            w8 = jnp.pad(conv_w_f[l], ((0, 4), (0, 0)))
            xc = _conv_fwd(xy, w8, conv_b_f[l][None])
            hs, avs = [], []
            for d in range(2):
                hd, ad = _lru_fwd("lru_fwd_rev" if d else "lru_fwd", xc, Wai[l], d, b_a_f[l, d][None], b_i_f[l, d][None],
                                  lam_f[l, d][None], bool(d))
                hs.append(hd)
                avs.append(ad)
            ro = _rnn_out_fwd(hs[0], hs[1], xy)
            x1 = proj_rows("rnn_out", ro, *wout[l], CS, xs)
            mix = (xy, w8, xc, hs, avs, ro)
        if i == 0:
            Wgu0, Wd0 = _exchange_wait("gather_ffn0_wait", ffn0_handle, x1)
            wgu[0], wd[0] = (Wgu0, 0), (Wd0, 0)
        h2 = _rms_fwd("norm_ffn_fwd", x1, norm_ffn[i:i + 1])
        Wgu, li = wgu[i]
        gu = _mm("ffn_gate_up", NN, h2, Wgu, BS((tm, D), lambda i_, j, r: (i_, 0)),
                 BS((None, None, None, D, F8), lambda i_, j, r, li=li: (j % NDEV, li, j // NDEV, 0, 0)),
                 BS((None, None, tm, F8), lambda i_, j, r: (j // NDEV, j % NDEV, i_, 0)),
                 (2, NDEV, S, F8), CD, (S // tm, 2 * NDEV, 1))
        act = _swiglu_fwd(gu)
        Wd, li = wd[i]
        x2 = _mm("ffn_down", NN, act, Wd, BS((2, tm, F8), lambda i_, j, r: (r, i_, 0)),
                 BS((2, None, F8, tn), lambda i_, j, r, li=li: (r, li, 0, j)),
                 BS((tm, tn), lambda i_, j, r: (i_, j)), (S, D), F32, (S // tm, D // tn, NDEV // 2), add=x1,
                 pairs=lambda a_ref, b_ref: [(a_ref[s], b_ref[s]) for s in range(2)])
        saved.append((xs, h1, mix, x1, h2, gu, act))
        xs = x2
        if i == 0:
            Wqkv1, Wo1, Win, Wout, Wai, Wgu1, Wd1, small = _exchange_wait("gather_rest_wait", rest_handle, x2)
            for k in range(1, n_attn):
                wqkv[k], wo[k] = (Wqkv1, k - 1), (Wo1, k - 1)
            for k in range(n_rnn):
                win[k], wout[k] = (Win, k), (Wout, k)
            for k in range(1, depth):
                wgu[k], wd[k] = (Wgu1, k - 1), (Wd1, k - 1)
            Wai = jnp.transpose(Wai, (1, 2, 3, 4, 0, 5, 6)).reshape(n_rnn, 2, 2, NB, BW, BW)
            small = jnp.transpose(small, (1, 2, 0, 3)).reshape(n_rnn, SMALL_ROWS, C)
            conv_w_f, conv_b_f, b_a_f, b_i_f, lam_f = unpack_small(small)

    dx, dxb, sq_err = _loss_head(xs, loss_target[0])
    loss = lax.psum(0.5 * sq_err[0, 0] / D, ("x", "y", "c"))

    g_norm_mix, g_norm_ffn = [None] * depth, [None] * depth
    g_qkv, g_wo, g_qg, g_kg = [None] * n_attn, [None] * n_attn, [None] * n_attn, [None] * n_attn
    g_win, g_wout, g_ai, g_small = [None] * n_rnn, [None] * n_rnn, [None] * n_rnn, [None] * n_rnn
    g_gu, g_down = [None] * depth, [None] * depth
    scat, sent = {}, None
    for i in reversed(range(depth)):
        l = i // 2
        xs, h1, mix, x1, h2, gu, act = saved[i]
        gain_ffn = norm_ffn[i:i + 1] if sent is None else norm_ffn[i:i + 1] + sent[0, 0]
        Wd, li = wd[i]
        dact = _mm("ffn_down_dx", NT, dxb, Wd, BS((tm, D), lambda i_, j, r: (i_, 0)),
                   BS((None, None, F8, D), lambda i_, j, r, li=li: (j, li, 0, 0)),
                   BS((None, tm, F8), lambda i_, j, r: (j, i_, 0)), (NDEV, S, F8), CD, (S // tm, NDEV, 1))
        g_down[i] = _mm("ffn_down_dw", TN, act, dxb, BS((None, tk, F8), lambda i_, j, r: (i_, r, 0)),
                        BS((tk, tn), lambda i_, j, r: (r, j)),
                        BS((None, F8, tn), lambda i_, j, r: (i_, 0, j)), (NDEV, F8, D), CD, (NDEV, D // tn, S // tk))
        dgu = _swiglu_bwd(dact, gu)
        Wgu, li = wgu[i]
        dh2 = _mm("ffn_gate_up_dx", NT, dgu, Wgu, BS((None, 4, tm, F8), lambda i_, j, r: (r // 2, r % 2, i_, 0)),
                  BS((4, None, None, tn, F8), lambda i_, j, r, li=li: (r % 2, li, r // 2, j, 0)),
                  BS((tm, tn), lambda i_, j, r: (i_, j)), (S, D), F32, (S // tm, D // tn, 2 * NDEV // 4),
                  pairs=lambda a_ref, b_ref: [(a_ref[s], b_ref[s]) for s in range(4)])
        g_gu[i] = _mm("ffn_gate_up_dw", TN, h2, dgu, BS((tk, tn), lambda i_, j, r: (r, i_)),
                      BS((None, None, tk, F8), lambda i_, j, r: (j // NDEV, j % NDEV, r, 0)),
                      BS((None, None, tn, F8), lambda i_, j, r: (j % NDEV, j // NDEV, i_, 0)),
                      (NDEV, 2, D, F8), CD, (D // tn, 2 * NDEV, S // tk))
        dx, dxb, g_norm_ffn[i] = _rms_bwd("norm_ffn_bwd", dh2, x1, gain_ffn, dx)
        if i == 0:
            scat["ffn0"], sent = _exchange_start("scatter_ffn0_start", [g_gu[0], g_down[0]], scatter=True)
        gain_mix = norm_mix[i:i + 1] + sent[0, 0] if i == 0 else norm_mix[i:i + 1]
        if i % 2 == 0:
            qkv, qr, kr, o, lse = mix
            do = proj_rows_dx("attn_out_dx", dxb, *wo[l], OS, CD)
            g_wo[l] = proj_rows_dw("attn_out_dw", o, dxb, OS)
            delta = _attn_delta(do, o, NQ, NKV)
            dq, dk, dv = _flash_bwd(qr, kr, qkv, do, jnp.transpose(lse[:, :, :G], (0, 2, 1)),
                                    jnp.transpose(delta[:, :, :G], (0, 2, 1)), NQ, NKV)
            dqkv, g_qg[l], g_kg[l] = _qk_prep_bwd(qkv, dq, dk, dv, attn_q_gain[l:l + 1], attn_k_gain[l:l + 1],
                                                  cos_t, sin_t, NQ, NKV)
            g_qkv[l] = proj_cols_dw("attn_qkv_dw", h1, dqkv, QS)
            dh1 = proj_cols_dx("attn_qkv_dx", dqkv, *wqkv[l], QS)
        else:
            xy, w8, xc, hs, avs, ro = mix
            dro = proj_rows_dx("rnn_out_dx", dxb, *wout[l], CS, F32)
            g_wout[l] = proj_rows_dw("rnn_out_dw", ro, dxb, CS)
            dhs, dyb = _rnn_out_bwd(dro, hs[0], hs[1], xy)
            dxc, dwa, dwi, dba, dbi, dlam = None, [], [], [], [], []
            for d in range(2):
                res = _lru_bwd("lru_bwd_rev" if d else "lru_bwd", dhs, avs[d], hs[d], xc, Wai[l], d, b_a_f[l, d][None],
                               b_i_f[l, d][None], lam_f[l, d][None], bool(d), dxc)
                dxc = res[0]
                for lst, val in zip((dwa, dwi, dba, dbi, dlam), res[1:]):
                    lst.append(val)
            dxb_branch, dw8 = _conv_bwd(dxc, xy, w8)
            dxy = jnp.concatenate([dxb_branch, dyb], axis=1)
            g_win[l] = proj_cols_dw("rnn_in_dw", h1, dxy, IS)
            dh1 = proj_cols_dx("rnn_in_dx", dxy, *win[l], IS)
            dai = jnp.stack([jnp.stack(dwa), jnp.stack(dwi)])
            g_ai[l] = jnp.transpose(dai.reshape(2, 2, NB, NDEV, BWS, BW), (3, 0, 1, 2, 4, 5))
            sm = rows_at([(0, dw8[0:4]), (8, dw8[4:5]), (16, dba[0]), (17, dba[1]), (24, dbi[0]), (25, dbi[1]),
                          (32, dlam[0]), (33, dlam[1])], SMALL_ROWS)
            g_small[l] = jnp.transpose(sm.reshape(SMALL_ROWS, NDEV, CS), (1, 0, 2))
        dx, dxb, g_norm_mix[i] = _rms_bwd("norm_mix_bwd", dh1, xs, gain_mix, dx)
        if i % 2 == 1:
            scat[i], sent = _exchange_start(f"scatter_layer{i}_start", [g_gu[i], g_down[i], g_win[l], g_wout[l], g_ai[l], g_small[l]],
                                            scatter=True)
        elif i > 0:
            scat[i], sent = _exchange_start(f"scatter_layer{i}_start", [g_gu[i], g_down[i], g_qkv[l], g_wo[l]], scatter=True)

    L_gu, L_down = [None] * depth, [None] * depth
    L_qkv, L_wo = [None] * n_attn, [None] * n_attn
    L_win, L_wout, L_ai, L_small = [None] * n_rnn, [None] * n_rnn, [None] * n_rnn, [None] * n_rnn
    L_qkv[0], L_wo[0] = _exchange("scatter_last", [g_qkv[0], g_wo[0]], scatter=True)
    norm_part = rows_at(list(enumerate(g_norm_mix + g_norm_ffn)), 2 * depth)
    gain_part = rows_at(list(enumerate(g_qg + g_kg)), 8)
    L_norm, L_gain = _exchange("gather_replicated", [norm_part, gain_part], scatter=False)
    for i in reversed(range(1, depth)):
        l = i // 2
        got = _exchange_wait(f"scatter_layer{i}_wait", scat[i], L_norm)
        if i % 2 == 1:
            L_gu[i], L_down[i], L_win[l], L_wout[l], L_ai[l], L_small[l] = got
        else:
            L_gu[i], L_down[i], L_qkv[l], L_wo[l] = got
    L_gu[0], L_down[0] = _exchange_wait("scatter_ffn0_wait", scat["ffn0"], L_norm)

    out = {}
    out["attn_w_qkv"] = _adamw("adamw_qkv", L_qkv, attn_w_qkv, m_attn_w_qkv, v_attn_w_qkv)
    out["attn_w_o"] = _adamw("adamw_wo", L_wo, attn_w_o, m_attn_w_o, v_attn_w_o)
    out["rnn_w_in"] = _adamw("adamw_win", L_win, rnn_w_in, m_rnn_w_in, v_rnn_w_in)
    out["rnn_w_out"] = _adamw("adamw_wout", L_wout, rnn_w_out, m_rnn_w_out, v_rnn_w_out)
    out["ffn_w_down"] = _adamw("adamw_down", L_down, ffn_w_down, m_ffn_w_down, v_ffn_w_down)
    out["ffn_w_gate"] = _adamw("adamw_gate", L_gu, ffn_w_gate, m_ffn_w_gate, v_ffn_w_gate, pick=0)
    out["ffn_w_up"] = _adamw("adamw_up", L_gu, ffn_w_up, m_ffn_w_up, v_ffn_w_up, pick=1)
    out["rnn_w_a"] = _adamw("adamw_w_a", L_ai, rnn_w_a, m_rnn_w_a, v_rnn_w_a, pick=0)
    out["rnn_w_i"] = _adamw("adamw_w_i", L_ai, rnn_w_i, m_rnn_w_i, v_rnn_w_i, pick=1)
    small_res = _adamw("adamw_small", L_small, small_sh,
                       pack_small(m_rnn_conv_w, m_rnn_conv_b, m_rnn_b_a, m_rnn_b_i, m_rnn_lambda),
                       pack_small(v_rnn_conv_w, v_rnn_conv_b, v_rnn_b_a, v_rnn_b_i, v_rnn_lambda))
    for name, vals in zip(("rnn_conv_w", "rnn_conv_b", "rnn_b_a", "rnn_b_i", "rnn_lambda"),
                          zip(*[unpack_small(r) for r in small_res])):
        out[name] = list(vals)
    one = lambda t: t[None]
    for name, land, w, m, v in (("norm_mix", L_norm[:, :depth], norm_mix, m_norm_mix, v_norm_mix),
                                ("norm_ffn", L_norm[:, depth:], norm_ffn, m_norm_ffn, v_norm_ffn),
                                ("attn_q_gain", L_gain[:, :n_attn], attn_q_gain, m_attn_q_gain, v_attn_q_gain),
                                ("attn_k_gain", L_gain[:, n_attn:2 * n_attn], attn_k_gain, m_attn_k_gain, v_attn_k_gain)):
        out[name] = [r[0] for r in _adamw("adamw_" + name, [land], one(w), one(m), one(v))]

    order = ["norm_mix", "norm_ffn", "attn_w_qkv", "attn_q_gain", "attn_k_gain", "attn_w_o", "rnn_w_in", "rnn_conv_w",
             "rnn_conv_b", "rnn_w_a", "rnn_b_a", "rnn_w_i", "rnn_b_i", "rnn_lambda", "rnn_w_out", "ffn_w_gate",
             "ffn_w_up", "ffn_w_down"]
    return (loss, dx[None], *[out[k][0] for k in order], *[out[k][1] for k in order], *[out[k][2] for k in order],
            *[out[k][3] for k in order])
```

```python
import functools
import math

import jax
import jax.numpy as jnp
from jax import lax
from jax.experimental import pallas as pl
from jax.experimental.pallas import tpu as pltpu

F32 = jnp.float32
CD = jnp.bfloat16
NDEV = 8
HEAD_DIM = 128
GRID_W = 64
ROPE_THETA = 10000.0
ROPE_FREQS = 32
EPS = 1e-6
LRU_C = 8.0
SMALL_ROWS = 40
ADAM_LR, ADAM_B1, ADAM_B2, ADAM_EPS, ADAM_WD, ADAM_STEP = 0.001, 0.9, 0.999, 1e-08, 0.01, 10
V7X_VMEM_LIMIT = 56 * 1024 * 1024
BS = pl.BlockSpec
MESH = pl.DeviceIdType.MESH
NN = (((1,), (0,)), ((), ()))
NT = (((1,), (1,)), ((), ()))
TN = (((0,), (0,)), ((), ()))


def _params(*sem):
    return pltpu.CompilerParams(dimension_semantics=sem, vmem_limit_bytes=V7X_VMEM_LIMIT)


def _tile(n, want):
    return want if n % want == 0 else n


def _exchange(name, arrs, scatter):
    n = len(arrs)
    out_shape = [jax.ShapeDtypeStruct(a.shape if scatter else (NDEV,) + a.shape, a.dtype) for a in arrs]

    def body(*refs):
        ins, outs = refs[:n], refs[n:2 * n]
        send_sems, recv_sems, local_sems = refs[2 * n:]
        x, y, c = lax.axis_index("x"), lax.axis_index("y"), lax.axis_index("c")
        me = 4 * x + 2 * y + c
        local, sends, recvs = [], [], []
        for a in range(n):
            cp = pltpu.make_async_copy(ins[a].at[me] if scatter else ins[a], outs[a].at[me], local_sems.at[a])
            cp.start()
            local.append(cp)
        for k in range(1, NDEV):
            px = 1 - x if k & 4 else x
            py = 1 - y if k & 2 else y
            pc = 1 - c if k & 1 else c
            peer = 4 * px + 2 * py + pc
            for a in range(n):
                src = ins[a].at[peer] if scatter else ins[a]
                kw = dict(send_sem=send_sems.at[a, k - 1], recv_sem=recv_sems.at[a, k - 1],
                          device_id=(px, py, pc), device_id_type=MESH)
                cp = pltpu.make_async_remote_copy(src_ref=src, dst_ref=outs[a].at[me], **kw)
                cp.start()
                sends.append(cp)
                recvs.append(pltpu.make_async_remote_copy(src_ref=src, dst_ref=outs[a].at[peer], **kw))
        for cp in local:
            cp.wait()
        for cp in sends:
            cp.wait_send()
        for cp in recvs:
            cp.wait_recv()

    return pl.pallas_call(
        body, name=name, out_shape=out_shape,
        in_specs=[BS(memory_space=pl.ANY)] * n, out_specs=[BS(memory_space=pl.ANY)] * n,
        scratch_shapes=[pltpu.SemaphoreType.DMA((n, NDEV - 1)), pltpu.SemaphoreType.DMA((n, NDEV - 1)),
                        pltpu.SemaphoreType.DMA((n,))],
    )(*arrs)


HBM = BS(memory_space=pltpu.HBM)
SEM = BS(memory_space=pltpu.SEMAPHORE)
EFFECT = pltpu.SideEffectType.DATAFLOW_SIDE_EFFECTING


def _peers(x, y, c):
    out = []
    for k in range(1, NDEV):
        px = 1 - x if k & 4 else x
        py = 1 - y if k & 2 else y
        pc = 1 - c if k & 1 else c
        out.append(((px, py, pc), 4 * px + 2 * py + pc))
    return out


def _exchange_start(name, arrs, scatter, after=None):
    n = len(arrs)
    lands = [lax.empty(a.shape if scatter else (NDEV,) + a.shape, a.dtype) for a in arrs]
    extra = [] if after is None else [after]

    def body(*refs):
        ins, outs = refs[:n], refs[n:2 * n]
        send_sems, recv_sems, token = refs[2 * n + len(extra)], refs[2 * n + len(extra) + 1], refs[-1]
        x, y, c = lax.axis_index("x"), lax.axis_index("y"), lax.axis_index("c")
        me = 4 * x + 2 * y + c
        for k, (dev, peer) in enumerate(_peers(x, y, c)):
            for a in range(n):
                pltpu.make_async_remote_copy(
                    src_ref=ins[a].at[peer] if scatter else ins[a], dst_ref=outs[a].at[me],
                    send_sem=send_sems.at[a * (NDEV - 1) + k], recv_sem=recv_sems.at[a * (NDEV - 1) + k],
                    device_id=dev, device_id_type=MESH).start()
        token[...] = jnp.zeros(token.shape, F32)

    sems = pltpu.SemaphoreType.DMA((n * (NDEV - 1),))
    res = pl.pallas_call(
        body, name=name,
        out_shape=(sems, sems, *[pltpu.HBM(a.shape, a.dtype) for a in arrs], *[pltpu.HBM(l.shape, l.dtype) for l in lands],
                   jax.ShapeDtypeStruct((8, 128), F32)),
        in_specs=[HBM] * (2 * n) + [BS(memory_space=pl.ANY)] * len(extra),
        out_specs=(SEM, SEM, *([HBM] * (2 * n)), BS(memory_space=pltpu.VMEM)),
        input_output_aliases={i: i + 2 for i in range(2 * n)},
        compiler_params=pltpu.CompilerParams(has_side_effects=EFFECT),
    )(*[pltpu.with_memory_space_constraint(t, pltpu.HBM) for t in list(arrs) + lands], *extra)
    return (res[0], res[1], res[2:2 + n], res[2 + n:2 + 2 * n], scatter), res[-1]


def _exchange_wait(name, handle, after):
    send_sems, recv_sems, srcs, lands, scatter = handle
    n = len(srcs)

    def body(*refs):
        ins, outs = refs[:n], refs[n:2 * n]
        s_sems, r_sems = refs[2 * n], refs[2 * n + 1]
        x, y, c = lax.axis_index("x"), lax.axis_index("y"), lax.axis_index("c")
        me = 4 * x + 2 * y + c
        for k, (dev, peer) in enumerate(_peers(x, y, c)):
            for a in range(n):
                src = ins[a].at[peer] if scatter else ins[a]
                kw = dict(send_sem=s_sems.at[a * (NDEV - 1) + k], recv_sem=r_sems.at[a * (NDEV - 1) + k], device_id=dev,
                          device_id_type=MESH)
                pltpu.make_async_remote_copy(src_ref=src, dst_ref=outs[a].at[me], **kw).wait_send()
                pltpu.make_async_remote_copy(src_ref=src, dst_ref=outs[a].at[peer], **kw).wait_recv()

    res = pl.pallas_call(
        body, name=name, out_shape=tuple(pltpu.HBM(t.shape, t.dtype) for t in list(srcs) + list(lands)),
        in_specs=[HBM] * (2 * n) + [SEM, SEM, BS(memory_space=pl.ANY)], out_specs=tuple([HBM] * (2 * n)),
        input_output_aliases={i: i for i in range(2 * n)},
        compiler_params=pltpu.CompilerParams(has_side_effects=EFFECT),
    )(*srcs, *lands, send_sems, recv_sems, after)
    me = 4 * lax.axis_index("x") + 2 * lax.axis_index("y") + lax.axis_index("c")
    out = []
    for src, land in zip(res[:n], res[n:]):
        own = lax.dynamic_index_in_dim(src, me, 0, keepdims=True) if scatter else src[None]
        out.append(lax.dynamic_update_slice_in_dim(land, own, me, 0))
    return out


def _mm(name, dims, a, b, a_spec, b_spec, o_spec, out_shape, out_dtype, grid, add=None, pairs=None):
    nr = grid[2]
    o_blk = tuple(d for d in o_spec.block_shape if d is not None)

    def body(*refs):
        a_ref, b_ref = refs[0], refs[1]
        add_ref = refs[2] if add is not None else None
        o_ref = refs[3] if add is not None else refs[2]
        part = None
        for av, bv in ([(a_ref[...], b_ref[...])] if pairs is None else pairs(a_ref, b_ref)):
            prod = lax.dot_general(av.astype(CD), bv.astype(CD), dims, preferred_element_type=F32)
            part = prod if part is None else part + prod

        def finish(v):
            if add_ref is not None:
                v = v + add_ref[...].astype(F32)
            o_ref[...] = v.astype(o_ref.dtype)

        if nr == 1:
            finish(part)
        else:
            acc = refs[-1]
            r = pl.program_id(2)

            @pl.when(r == 0)
            def _():
                acc[...] = part

            @pl.when(r > 0)
            def _():
                acc[...] += part

            @pl.when(r == nr - 1)
            def _():
                finish(acc[...])

    return pl.pallas_call(
        body, name=name, grid=grid, out_shape=jax.ShapeDtypeStruct(out_shape, out_dtype),
        in_specs=[a_spec, b_spec] + ([o_spec] if add is not None else []), out_specs=o_spec,
        scratch_shapes=[pltpu.VMEM(o_blk, F32)] if nr > 1 else [],
        compiler_params=_params("parallel", "parallel", "arbitrary"),
    )(*([a, b] + ([add] if add is not None else [])))


def _rms_fwd(name, x, g):
    S, D = x.shape
    tr = _tile(S, 512)

    def body(x_ref, g_ref, o_ref):
        xv = x_ref[...]
        r = lax.rsqrt(jnp.mean(xv * xv, axis=-1, keepdims=True) + EPS)
        o_ref[...] = (xv * r * g_ref[...]).astype(o_ref.dtype)

    return pl.pallas_call(
        body, name=name, grid=(S // tr,), out_shape=jax.ShapeDtypeStruct((S, D), CD),
        in_specs=[BS((tr, D), lambda i: (i, 0)), BS((1, D), lambda i: (0, 0))],
        out_specs=BS((tr, D), lambda i: (i, 0)), compiler_params=_params("parallel"),
    )(x, g)


def _rms_bwd(name, dh, x, g, dx_in):
    S, D = x.shape
    tr = _tile(S, 256)

    def body(dh_ref, x_ref, g_ref, dxin_ref, dx_ref, dxb_ref, dg_ref):
        xv = x_ref[...]
        dhv = dh_ref[...].astype(F32)
        r = lax.rsqrt(jnp.mean(xv * xv, axis=-1, keepdims=True) + EPS)
        yv = xv * r
        dy = dhv * g_ref[...]
        dx = dxin_ref[...] + r * (dy - yv * jnp.mean(dy * yv, axis=-1, keepdims=True))
        dx_ref[...] = dx
        dxb_ref[...] = dx.astype(CD)
        part = jnp.sum(dhv * yv, axis=0, keepdims=True)

        @pl.when(pl.program_id(0) == 0)
        def _():
            dg_ref[...] = part

        @pl.when(pl.program_id(0) > 0)
        def _():
            dg_ref[...] += part

    row = BS((tr, D), lambda i: (i, 0))
    vec = BS((1, D), lambda i: (0, 0))
    return pl.pallas_call(
        body, name=name, grid=(S // tr,),
        out_shape=[jax.ShapeDtypeStruct((S, D), F32), jax.ShapeDtypeStruct((S, D), CD),
                   jax.ShapeDtypeStruct((1, D), F32)],
        in_specs=[row, row, vec, row], out_specs=[row, row, vec], compiler_params=_params("arbitrary"),
    )(dh, x, g, dx_in)


def _loss_head(y, target):
    S, D = y.shape
    tr = _tile(S, 512)

    def body(y_ref, t_ref, dy_ref, dyb_ref, l_ref):
        err = y_ref[...] - t_ref[...]
        dy = err * (1.0 / D)
        dy_ref[...] = dy
        dyb_ref[...] = dy.astype(CD)
        part = jnp.full((8, 128), jnp.sum(err * err), F32)

        @pl.when(pl.program_id(0) == 0)
        def _():
            l_ref[...] = part

        @pl.when(pl.program_id(0) > 0)
        def _():
            l_ref[...] += part

    row = BS((tr, D), lambda i: (i, 0))
    return pl.pallas_call(
        body, name="loss_head", grid=(S // tr,),
        out_shape=[jax.ShapeDtypeStruct((S, D), F32), jax.ShapeDtypeStruct((S, D), CD),
                   jax.ShapeDtypeStruct((8, 128), F32)],
        in_specs=[row, row], out_specs=[row, row, BS((8, 128), lambda i: (0, 0))],
        compiler_params=_params("arbitrary"),
    )(y, target)


def _sigmoid(v):
    return 1.0 / (1.0 + jnp.exp(-v))


def _swiglu_fwd(gu):
    _, J, S, F8 = gu.shape
    tr = _tile(S, 1024)

    def body(gu_ref, o_ref):
        gv = gu_ref[0].astype(F32)
        uv = gu_ref[1].astype(F32)
        o_ref[...] = (gv * _sigmoid(gv) * uv).astype(o_ref.dtype)

    return pl.pallas_call(
        body, name="swiglu_fwd", grid=(J, S // tr), out_shape=jax.ShapeDtypeStruct((J, S, F8), CD),
        in_specs=[BS((2, None, tr, F8), lambda j, i: (0, j, i, 0))],
        out_specs=BS((None, tr, F8), lambda j, i: (j, i, 0)), compiler_params=_params("parallel", "parallel"),
    )(gu)


def _swiglu_bwd(da, gu):
    _, J, S, F8 = gu.shape
    tr = _tile(S, 1024)

    def body(da_ref, gu_ref, o_ref):
        gv = gu_ref[0].astype(F32)
        uv = gu_ref[1].astype(F32)
        dav = da_ref[...].astype(F32)
        sg = _sigmoid(gv)
        o_ref[0] = (dav * uv * sg * (1.0 + gv * (1.0 - sg))).astype(o_ref.dtype)
        o_ref[1] = (dav * gv * sg).astype(o_ref.dtype)

    pair = BS((2, None, tr, F8), lambda j, i: (0, j, i, 0))
    return pl.pallas_call(
        body, name="swiglu_bwd", grid=(J, S // tr), out_shape=jax.ShapeDtypeStruct(gu.shape, CD),
        in_specs=[BS((None, tr, F8), lambda j, i: (j, i, 0)), pair], out_specs=pair,
        compiler_params=_params("parallel", "parallel"),
    )(da, gu)


def _rope_tables(S):
    freqs = ROPE_THETA ** (-jnp.arange(ROPE_FREQS, dtype=F32) / ROPE_FREQS)
    pos = jnp.arange(S, dtype=jnp.int32)
    row = (pos // GRID_W).astype(F32)[:, None] * freqs
    col = (pos % GRID_W).astype(F32)[:, None] * freqs
    cr, sr, cc, sc = jnp.cos(row), jnp.sin(row), jnp.cos(col), jnp.sin(col)
    return jnp.concatenate([cr, cr, cc, cc], axis=1), jnp.concatenate([-sr, sr, -sc, sc], axis=1)


def _swap_halves(v):
    lane = lax.broadcasted_iota(jnp.int32, v.shape, 1)
    return jnp.where((lane & 32) == 0, pltpu.roll(v, 96, 1), pltpu.roll(v, 32, 1))


def _qk_prep_fwd(qkv, qg, kg, cos_t, sin_t, NQ, NKV):
    S, W = qkv.shape
    tr = _tile(S, 256)
    scale = HEAD_DIM ** -0.5

    def body(qkv_ref, qg_ref, kg_ref, c_ref, s_ref, q_out, k_out):
        C, Sg = c_ref[...], s_ref[...]

        def one(xh, gain, sc):
            xh = xh.astype(F32)
            r = lax.rsqrt(jnp.mean(xh * xh, axis=-1, keepdims=True) + EPS)
            nv = xh * r * gain
            return (nv * C + _swap_halves(nv) * Sg) * sc

        for h in range(NQ):
            sl = slice(h * HEAD_DIM, (h + 1) * HEAD_DIM)
            q_out[:, sl] = one(qkv_ref[:, sl], qg_ref[...], scale).astype(CD)
        for h in range(NKV):
            sl = slice((NQ + h) * HEAD_DIM, (NQ + h + 1) * HEAD_DIM)
            k_out[:, h * HEAD_DIM:(h + 1) * HEAD_DIM] = one(qkv_ref[:, sl], kg_ref[...], 1.0).astype(CD)

    vec = BS((1, HEAD_DIM), lambda i: (0, 0))
    tab = BS((tr, HEAD_DIM), lambda i: (i, 0))
    return pl.pallas_call(
        body, name="qk_prep_fwd", grid=(S // tr,),
        out_shape=[jax.ShapeDtypeStruct((S, NQ * HEAD_DIM), CD), jax.ShapeDtypeStruct((S, NKV * HEAD_DIM), CD)],
        in_specs=[BS((tr, W), lambda i: (i, 0)), vec, vec, tab, tab],
        out_specs=[BS((tr, NQ * HEAD_DIM), lambda i: (i, 0)), BS((tr, NKV * HEAD_DIM), lambda i: (i, 0))],
        compiler_params=_params("parallel"),
    )(qkv, qg, kg, cos_t, sin_t)


def _qk_prep_bwd(qkv, dq, dk, dv, qg, kg, cos_t, sin_t, NQ, NKV):
    S, W = qkv.shape
    tr = _tile(S, 256)
    scale = HEAD_DIM ** -0.5

    def body(qkv_ref, dq_ref, dk_ref, dv_ref, qg_ref, kg_ref, c_ref, s_ref, o_ref, dqg_ref, dkg_ref):
        C, Sg = c_ref[...], s_ref[...]

        def one(xh, dout, gain, sc):
            xh = xh.astype(F32)
            r = lax.rsqrt(jnp.mean(xh * xh, axis=-1, keepdims=True) + EPS)
            yv = xh * r
            dout = dout.astype(F32) * sc
            dn = dout * C + _swap_halves(dout * Sg)
            dy = dn * gain
            dx = r * (dy - yv * jnp.mean(dy * yv, axis=-1, keepdims=True))
            return dx, jnp.sum(dn * yv, axis=0, keepdims=True)

        dqg = jnp.zeros((1, HEAD_DIM), F32)
        dkg = jnp.zeros((1, HEAD_DIM), F32)
        for h in range(NQ):
            sl = slice(h * HEAD_DIM, (h + 1) * HEAD_DIM)
            dx, dg = one(qkv_ref[:, sl], dq_ref[:, sl], qg_ref[...], scale)
            o_ref[:, sl] = dx.astype(CD)
            dqg = dqg + dg
        for h in range(NKV):
            sl = slice((NQ + h) * HEAD_DIM, (NQ + h + 1) * HEAD_DIM)
            dx, dg = one(qkv_ref[:, sl], dk_ref[:, h * HEAD_DIM:(h + 1) * HEAD_DIM], kg_ref[...], 1.0)
            o_ref[:, sl] = dx.astype(CD)
            dkg = dkg + dg
        o_ref[:, (NQ + NKV) * HEAD_DIM:] = dv_ref[...].astype(CD)

        @pl.when(pl.program_id(0) == 0)
        def _():
            dqg_ref[...] = dqg
            dkg_ref[...] = dkg

        @pl.when(pl.program_id(0) > 0)
        def _():
            dqg_ref[...] += dqg
            dkg_ref[...] += dkg

    vec = BS((1, HEAD_DIM), lambda i: (0, 0))
    tab = BS((tr, HEAD_DIM), lambda i: (i, 0))
    full = BS((tr, W), lambda i: (i, 0))
    return pl.pallas_call(
        body, name="qk_prep_bwd", grid=(S // tr,),
        out_shape=[jax.ShapeDtypeStruct((S, W), CD), jax.ShapeDtypeStruct((1, HEAD_DIM), F32),
                   jax.ShapeDtypeStruct((1, HEAD_DIM), F32)],
        in_specs=[full, BS((tr, NQ * HEAD_DIM), lambda i: (i, 0)), BS((tr, NKV * HEAD_DIM), lambda i: (i, 0)),
                  BS((tr, NKV * HEAD_DIM), lambda i: (i, 0)), vec, vec, tab, tab],
        out_specs=[full, vec, vec], compiler_params=_params("arbitrary"),
    )(qkv, dq, dk, dv, qg, kg, cos_t, sin_t)


def _lane_pack(cols):
    rows = cols[0].shape[0]
    lane = lax.broadcasted_iota(jnp.int32, (rows, HEAD_DIM), 1)
    out = jnp.zeros((rows, HEAD_DIM), F32)
    for g, col in enumerate(cols):
        out = jnp.where(lane == g, col, out)
    return out


def _flash_fwd(qr, kr, qkv, NQ, NKV):
    S = qr.shape[0]
    G = NQ // NKV
    tq, tk = _tile(S, 512), _tile(S, 512)
    nk = S // tk
    GW = G * HEAD_DIM

    def body(q_ref, k_ref, v_ref, o_ref, lse_ref, m_s, acc_s):
        ki = pl.program_id(2)

        @pl.when(ki == 0)
        def _():
            m_s[...] = jnp.full(m_s.shape, -1e30, F32)
            acc_s[...] = jnp.zeros(acc_s.shape, F32)

        kv = k_ref[...]
        v_ones = jnp.concatenate([v_ref[...], jnp.ones((tk, HEAD_DIM), CD)], axis=1)
        for g in range(G):
            s = lax.dot_general(q_ref[:, g * HEAD_DIM:(g + 1) * HEAD_DIM], kv, NT, preferred_element_type=F32)
            m_prev = m_s[g]
            m_new = jnp.maximum(m_prev, jnp.max(s, axis=-1, keepdims=True))
            alpha = jnp.exp(m_prev - m_new)
            p = jnp.exp(s - jnp.tile(m_new, (1, tk // HEAD_DIM)))
            acc_s[g] = jnp.tile(alpha, (1, 2)) * acc_s[g] + jnp.dot(p.astype(CD), v_ones, preferred_element_type=F32)
            m_s[g] = m_new

        @pl.when(ki == nk - 1)
        def _():
            lane = lax.broadcasted_iota(jnp.int32, (tq, HEAD_DIM), 1)
            lse = jnp.zeros((tq, HEAD_DIM), F32)
            for g in range(G):
                l = acc_s[g, :, HEAD_DIM:]
                o_ref[:, g * HEAD_DIM:(g + 1) * HEAD_DIM] = (acc_s[g, :, :HEAD_DIM] / l).astype(o_ref.dtype)
                lse = jnp.where(lane == g, m_s[g] + jnp.log(l), lse)
            lse_ref[...] = lse

    return pl.pallas_call(
        body, name="flash_fwd", grid=(NKV, S // tq, nk),
        out_shape=[jax.ShapeDtypeStruct((S, NQ * HEAD_DIM), CD), jax.ShapeDtypeStruct((NKV, S, HEAD_DIM), F32)],
        in_specs=[BS((tq, GW), lambda h, i, k: (i, h)), BS((tk, HEAD_DIM), lambda h, i, k: (k, h)),
                  BS((tk, HEAD_DIM), lambda h, i, k: (k, NQ + NKV + h))],
        out_specs=[BS((tq, GW), lambda h, i, k: (i, h)), BS((None, tq, HEAD_DIM), lambda h, i, k: (h, i, 0))],
        scratch_shapes=[pltpu.VMEM((G, tq, HEAD_DIM), F32), pltpu.VMEM((G, tq, 2 * HEAD_DIM), F32)],
        compiler_params=_params("parallel", "parallel", "arbitrary"),
    )(qr, kr, qkv)


def _attn_delta(do, o, NQ, NKV):
    S = o.shape[0]
    G = NQ // NKV
    tr = _tile(S, 512)

    def body(do_ref, o_ref, d_ref):
        for kv in range(NKV):
            cols = []
            for g in range(G):
                sl = slice((kv * G + g) * HEAD_DIM, (kv * G + g + 1) * HEAD_DIM)
                cols.append(jnp.sum(do_ref[:, sl].astype(F32) * o_ref[:, sl].astype(F32), axis=-1, keepdims=True))
            d_ref[kv] = _lane_pack(cols)

    row = BS((tr, NQ * HEAD_DIM), lambda i: (i, 0))
    return pl.pallas_call(
        body, name="attn_delta", grid=(S // tr,), out_shape=jax.ShapeDtypeStruct((NKV, S, HEAD_DIM), F32),
        in_specs=[row, row], out_specs=BS((NKV, tr, HEAD_DIM), lambda i: (0, i, 0)),
        compiler_params=_params("parallel"),
    )(do, o)


def _flash_dq(qr, kr, qkv, do, lse, delta, NQ, NKV):
    S = qr.shape[0]
    G = NQ // NKV
    tq, tk = _tile(S, 512), _tile(S, 512)
    nk = S // tk
    GW = G * HEAD_DIM

    def body(q_ref, k_ref, v_ref, do_ref, lse_ref, dl_ref, dq_ref, acc_s):
        ki = pl.program_id(2)

        @pl.when(ki == 0)
        def _():
            acc_s[...] = jnp.zeros(acc_s.shape, F32)

        kv, vv = k_ref[...], v_ref[...]
        for g in range(G):
            sl = slice(g * HEAD_DIM, (g + 1) * HEAD_DIM)
            s = lax.dot_general(q_ref[:, sl], kv, NT, preferred_element_type=F32)
            p = jnp.exp(s - lse_ref[:, g:g + 1])
            dp = lax.dot_general(do_ref[:, sl], vv, NT, preferred_element_type=F32)
            ds = p * (dp - dl_ref[:, g:g + 1])
            acc_s[g] += jnp.dot(ds.astype(CD), kv, preferred_element_type=F32)

        @pl.when(ki == nk - 1)
        def _():
            for g in range(G):
                dq_ref[:, g * HEAD_DIM:(g + 1) * HEAD_DIM] = acc_s[g].astype(dq_ref.dtype)

    qb = BS((tq, GW), lambda h, i, k: (i, h))
    st = BS((None, tq, HEAD_DIM), lambda h, i, k: (h, i, 0))
    return pl.pallas_call(
        body, name="flash_dq", grid=(NKV, S // tq, nk), out_shape=jax.ShapeDtypeStruct((S, NQ * HEAD_DIM), CD),
        in_specs=[qb, BS((tk, HEAD_DIM), lambda h, i, k: (k, h)), BS((tk, HEAD_DIM), lambda h, i, k: (k, NQ + NKV + h)),
                  qb, st, st],
        out_specs=qb, scratch_shapes=[pltpu.VMEM((G, tq, HEAD_DIM), F32)],
        compiler_params=_params("parallel", "parallel", "arbitrary"),
    )(qr, kr, qkv, do, lse, delta)


def _flash_dkv(qr, kr, qkv, do, lse_rows, delta_rows, NQ, NKV):
    S = qr.shape[0]
    G = NQ // NKV
    tq, tk = _tile(S, 512), _tile(S, 512)
    nq = S // tq
    GW = G * HEAD_DIM

    def body(q_ref, k_ref, v_ref, do_ref, lse_ref, dl_ref, dk_ref, dv_ref, dk_s, dv_s):
        qi = pl.program_id(2)

        @pl.when(qi == 0)
        def _():
            dk_s[...] = jnp.zeros(dk_s.shape, F32)
            dv_s[...] = jnp.zeros(dv_s.shape, F32)

        kv, vv = k_ref[...], v_ref[...]
        for g in range(G):
            sl = slice(g * HEAD_DIM, (g + 1) * HEAD_DIM)
            qg, dog = q_ref[:, sl], do_ref[:, sl]
            st = lax.dot_general(kv, qg, NT, preferred_element_type=F32)
            pt = jnp.exp(st - lse_ref[g:g + 1, :])
            dv_s[...] += jnp.dot(pt.astype(CD), dog, preferred_element_type=F32)
            dpt = lax.dot_general(vv, dog, NT, preferred_element_type=F32)
            dst = pt * (dpt - dl_ref[g:g + 1, :])
            dk_s[...] += jnp.dot(dst.astype(CD), qg, preferred_element_type=F32)

        @pl.when(qi == nq - 1)
        def _():
            dk_ref[...] = dk_s[...].astype(dk_ref.dtype)
            dv_ref[...] = dv_s[...].astype(dv_ref.dtype)

    qb = BS((tq, GW), lambda h, k, i: (i, h))
    st = BS((None, G, tq), lambda h, k, i: (h, 0, i))
    kb = BS((tk, HEAD_DIM), lambda h, k, i: (k, h))
    return pl.pallas_call(
        body, name="flash_dkv", grid=(NKV, S // tk, nq),
        out_shape=[jax.ShapeDtypeStruct((S, NKV * HEAD_DIM), CD), jax.ShapeDtypeStruct((S, NKV * HEAD_DIM), CD)],
        in_specs=[qb, kb, BS((tk, HEAD_DIM), lambda h, k, i: (k, NQ + NKV + h)), qb, st, st],
        out_specs=[kb, kb], scratch_shapes=[pltpu.VMEM((tk, HEAD_DIM), F32), pltpu.VMEM((tk, HEAD_DIM), F32)],
        compiler_params=_params("parallel", "parallel", "arbitrary"),
    )(qr, kr, qkv, do, lse_rows, delta_rows)


def _flash_bwd(qr, kr, qkv, do, lse_rows, delta_rows, NQ, NKV):
    S = qr.shape[0]
    G = NQ // NKV
    tq, tk = _tile(S, 512), _tile(S, 512)
    nq, nk = S // tq, S // tk
    GW = G * HEAD_DIM

    def body(q_ref, k_ref, v_ref, do_ref, lse_ref, dl_ref, dq_ref, dk_ref, dv_ref, dq_s, dk_s, dv_s):
        ki, qi = pl.program_id(1), pl.program_id(2)
        rows = pl.ds(pl.multiple_of(qi * tq, tq), tq)

        @pl.when(qi == 0)
        def _():
            dk_s[...] = jnp.zeros(dk_s.shape, F32)
            dv_s[...] = jnp.zeros(dv_s.shape, F32)

        kv, vv = k_ref[...], v_ref[...]
        for g in range(G):
            sl = slice(g * HEAD_DIM, (g + 1) * HEAD_DIM)
            qg, dog = q_ref[:, sl], do_ref[:, sl]
            st = lax.dot_general(kv, qg, NT, preferred_element_type=F32)
            pt = jnp.exp(st - lse_ref[g:g + 1, :])
            dv_s[...] += jnp.dot(pt.astype(CD), dog, preferred_element_type=F32)
            dpt = lax.dot_general(vv, dog, NT, preferred_element_type=F32)
            dst = (pt * (dpt - dl_ref[g:g + 1, :])).astype(CD)
            dk_s[...] += jnp.dot(dst, qg, preferred_element_type=F32)
            dq_part = lax.dot_general(dst, kv, TN, preferred_element_type=F32)

            @pl.when(ki == 0)
            def _():
                dq_s[g, rows, :] = dq_part

            @pl.when(ki > 0)
            def _():
                dq_s[g, rows, :] += dq_part

        @pl.when(qi == nq - 1)
        def _():
            dk_ref[...] = dk_s[...].astype(dk_ref.dtype)
            dv_ref[...] = dv_s[...].astype(dv_ref.dtype)

        @pl.when(ki == nk - 1)
        def _():
            for g in range(G):
                dq_ref[rows, g * HEAD_DIM:(g + 1) * HEAD_DIM] = dq_s[g, rows, :].astype(dq_ref.dtype)

    qb = BS((tq, GW), lambda h, k, i: (i, h))
    st = BS((None, G, tq), lambda h, k, i: (h, 0, i))
    kb = BS((tk, HEAD_DIM), lambda h, k, i: (k, h))
    return pl.pallas_call(
        body, name="flash_bwd", grid=(NKV, nk, nq),
        out_shape=[jax.ShapeDtypeStruct((S, NQ * HEAD_DIM), CD), jax.ShapeDtypeStruct((S, NKV * HEAD_DIM), CD),
                   jax.ShapeDtypeStruct((S, NKV * HEAD_DIM), CD)],
        in_specs=[qb, kb, BS((tk, HEAD_DIM), lambda h, k, i: (k, NQ + NKV + h)), qb, st, st],
        out_specs=[BS((S, GW), lambda h, k, i: (0, h)), kb, kb],
        scratch_shapes=[pltpu.VMEM((G, S, HEAD_DIM), F32), pltpu.VMEM((tk, HEAD_DIM), F32), pltpu.VMEM((tk, HEAD_DIM), F32)],
        compiler_params=_params("parallel", "arbitrary", "arbitrary"),
    )(qr, kr, qkv, do, lse_rows, delta_rows)


HALO8 = 8


def _round(v):
    return v.astype(CD).astype(F32)


def _shift_down(v, k, above):
    T = v.shape[0]
    row = lax.broadcasted_iota(jnp.int32, v.shape, 0)
    out = pltpu.roll(v, k, 0)
    for j in range(k):
        out = jnp.where(row == j, above[above.shape[0] - k + j:above.shape[0] - k + j + 1, :], out)
    return out


def _shift_up(v, k, below):
    T = v.shape[0]
    row = lax.broadcasted_iota(jnp.int32, v.shape, 0)
    out = pltpu.roll(v, T - k, 0)
    for j in range(k):
        out = jnp.where(row == T - k + j, below[j:j + 1, :], out)
    return out


def _halo_specs(T, C, halo, n_chunks, S, col=0, rev=False):
    per = T // halo
    last = S // halo - 1
    ch = (lambda i: n_chunks - 1 - i) if rev else (lambda i: i)
    return (BS((T, C), lambda i: (ch(i), col)),
            BS((halo, C), lambda i: (jnp.maximum(ch(i) * per - 1, 0), col)),
            BS((halo, C), lambda i: (jnp.minimum((ch(i) + 1) * per, last), col)))


def _conv_fwd(xy, w8, bias):
    S, C2 = xy.shape
    C = C2 // 2
    T = _tile(S, 256)
    n = S // T

    def body(x_ref, p_ref, q_ref, w_ref, b_ref, o_ref):
        i = pl.program_id(0)
        xv = _round(x_ref[...])
        above = jnp.where(i > 0, _round(p_ref[...]), 0.0)
        below = jnp.where(i < n - 1, _round(q_ref[...]), 0.0)
        wv = _round(w_ref[...])
        o_ref[...] = (wv[0:1, :] * _shift_down(xv, 2, above) + wv[1:2, :] * _shift_down(xv, 1, above)
                      + wv[2:3, :] * xv + wv[3:4, :] * _shift_up(xv, 1, below) + b_ref[...])

    main, prev, nxt = _halo_specs(T, C, HALO8, n, S)
    return pl.pallas_call(
        body, name="conv_fwd", grid=(n,), out_shape=jax.ShapeDtypeStruct((S, C), F32),
        in_specs=[main, prev, nxt, BS((8, C), lambda i: (0, 0)), BS((1, C), lambda i: (0, 0))],
        out_specs=BS((T, C), lambda i: (i, 0)), compiler_params=_params("parallel"),
    )(xy, xy, xy, w8, bias)


def _tile_scan(a, b, reverse):
    row = lax.broadcasted_iota(jnp.int32, a.shape, 0)
    for sh in (1, 2, 4):
        if reverse:
            a_s, b_s, ok = pltpu.roll(a, 8 - sh, 0), pltpu.roll(b, 8 - sh, 0), row < 8 - sh
        else:
            a_s, b_s, ok = pltpu.roll(a, sh, 0), pltpu.roll(b, sh, 0), row >= sh
        b = jnp.where(ok, a * b_s + b, b)
        a = jnp.where(ok, a * a_s, a)
    return a, b


def _scan_chunk(a_ref, h_ref, carry_ref, reverse):
    T = h_ref.shape[0]
    nt = T // 8

    def step(j, carry):
        t = (nt - 1 - j) if reverse else j
        rows = pl.ds(pl.multiple_of(t * 8, 8), 8)
        ca, hb = _tile_scan(a_ref[rows, :], h_ref[rows, :], reverse)
        hv = hb + ca * carry
        h_ref[rows, :] = hv
        edge = hv[0:1, :] if reverse else hv[7:8, :]
        return jnp.broadcast_to(edge, hv.shape)

    carry_ref[...] = lax.fori_loop(0, nt, step, carry_ref[...])


def _softplus(z):
    return jnp.maximum(z, 0.0) + jnp.log(1.0 + jnp.exp(-jnp.abs(z)))


def _neg_expm1(z):
    return jnp.where(z > -0.01, -z * (1.0 + z * (0.5 + z * (1.0 / 6.0 + z * (1.0 / 24.0)))), 1.0 - jnp.exp(z))


def _gates(xs, wa, wi, ba, bi, sp):
    r = _sigmoid(jnp.dot(xs, wa, preferred_element_type=F32) + ba)
    ig = _sigmoid(jnp.dot(xs, wi, preferred_element_type=F32) + bi)
    la = -LRU_C * r * sp
    return r, ig, jnp.exp(la), jnp.sqrt(_neg_expm1(2.0 * la))


def _lru_fwd(name, xc, wai, d, ba, bi, lam, reverse):
    S, C = xc.shape
    NB, BW = wai.shape[2], wai.shape[3]
    T = _tile(S, 256)
    n = S // T
    ch = (lambda i: n - 1 - i) if reverse else (lambda i: i)

    def body(x_ref, wa_ref, wi_ref, ba_ref, bi_ref, lam_ref, h_ref, a_ref, carry):
        @pl.when(pl.program_id(0) == 0)
        def _():
            carry[...] = jnp.zeros(carry.shape, F32)

        for b in range(NB):
            sl = slice(b * BW, (b + 1) * BW)
            xf = x_ref[:, sl]
            _, ig, av, sq = _gates(xf.astype(CD), wa_ref[b], wi_ref[b], ba_ref[:, sl], bi_ref[:, sl],
                                   _softplus(-lam_ref[:, sl]))
            a_ref[:, sl] = av
            h_ref[:, sl] = sq * ig * xf
        _scan_chunk(a_ref, h_ref, carry, reverse)

    vec = BS((1, C), lambda i: (0, 0))
    blk = BS((T, C), lambda i: (ch(i), 0))
    return pl.pallas_call(
        body, name=name, grid=(n,),
        out_shape=[jax.ShapeDtypeStruct((S, C), F32), jax.ShapeDtypeStruct((S, C), F32)],
        in_specs=[blk, BS((None, None, NB, BW, BW), lambda i: (0, d, 0, 0, 0)),
                  BS((None, None, NB, BW, BW), lambda i: (1, d, 0, 0, 0)), vec, vec, vec],
        out_specs=[blk, blk], scratch_shapes=[pltpu.VMEM((8, C), F32)], compiler_params=_params("arbitrary"),
    )(xc, wai, wai, ba, bi, lam)


_GELU_C = math.sqrt(2.0 / math.pi)


def _rnn_out_fwd(h0, h1, xy):
    S, C = h0.shape
    tr = _tile(S, 256)

    def body(h0_ref, h1_ref, y_ref, o_ref):
        yv = y_ref[...].astype(F32)
        gl = 0.5 * yv * (1.0 + jnp.tanh(_GELU_C * (yv + 0.044715 * yv * yv * yv)))
        o_ref[...] = ((h0_ref[...] + h1_ref[...]) * gl).astype(o_ref.dtype)

    row = BS((tr, C), lambda i: (i, 0))
    return pl.pallas_call(
        body, name="rnn_out_fwd", grid=(S // tr,), out_shape=jax.ShapeDtypeStruct((S, C), CD),
        in_specs=[row, row, BS((tr, C), lambda i: (i, 1))], out_specs=row, compiler_params=_params("parallel"),
    )(h0, h1, xy)


def _rnn_out_bwd(dout, h0, h1, xy):
    S, C = h0.shape
    tr = _tile(S, 256)

    def body(d_ref, h0_ref, h1_ref, y_ref, dh_ref, dy_ref):
        yv = y_ref[...].astype(F32)
        dv = d_ref[...].astype(F32)
        t = jnp.tanh(_GELU_C * (yv + 0.044715 * yv * yv * yv))
        gl = 0.5 * yv * (1.0 + t)
        dgl = 0.5 * (1.0 + t) + 0.5 * yv * (1.0 - t * t) * _GELU_C * (1.0 + 3.0 * 0.044715 * yv * yv)
        dh_ref[...] = dv * gl
        dy_ref[...] = (dv * (h0_ref[...] + h1_ref[...]) * dgl).astype(dy_ref.dtype)

    row = BS((tr, C), lambda i: (i, 0))
    return pl.pallas_call(
        body, name="rnn_out_bwd", grid=(S // tr,),
        out_shape=[jax.ShapeDtypeStruct((S, C), F32), jax.ShapeDtypeStruct((S, C), CD)],
        in_specs=[row, row, row, BS((tr, C), lambda i: (i, 1))], out_specs=[row, row],
        compiler_params=_params("parallel"),
    )(dout, h0, h1, xy)


def _lru_bwd(name, dhs, a, h, xc, wai, d, ba, bi, lam, reverse, dxc_in):
    S, C = xc.shape
    NB, BW = wai.shape[2], wai.shape[3]
    T = _tile(S, 256)
    n = S // T
    back = not reverse

    def body(*refs):
        (dh_ref, a_ref, a_prev, a_next, h_ref, h_prev, h_next, x_ref, wa_ref, wi_ref, ba_ref, bi_ref,
         lam_ref) = refs[:13]
        k = 13
        add_ref = None
        if dxc_in is not None:
            add_ref = refs[k]
            k += 1
        dx_ref, dwa_ref, dwi_ref, dba_ref, dbi_ref, dlam_ref, coef_s, lam_s, carry, dsp_s = refs[k:]
        i = pl.program_id(0)
        first_chunk = i == (n - 1 if back else 0)
        last_chunk = i == (0 if back else n - 1)

        @pl.when(i == 0)
        def _():
            carry[...] = jnp.zeros(carry.shape, F32)
            dsp_s[...] = jnp.zeros(dsp_s.shape, F32)
            dwa_ref[...] = jnp.zeros(dwa_ref.shape, F32)
            dwi_ref[...] = jnp.zeros(dwi_ref.shape, F32)
            dba_ref[...] = jnp.zeros(dba_ref.shape, F32)
            dbi_ref[...] = jnp.zeros(dbi_ref.shape, F32)

        av, hv = a_ref[...], h_ref[...]
        if reverse:
            coef_s[...] = _shift_down(av, 1, a_prev[...])
            h_nb = _shift_up(hv, 1, jnp.where(last_chunk, 0.0, h_next[...]))
        else:
            coef_s[...] = _shift_up(av, 1, a_next[...])
            h_nb = _shift_down(hv, 1, jnp.where(first_chunk, 0.0, h_prev[...]))
        lam_s[...] = dh_ref[...]
        _scan_chunk(coef_s, lam_s, carry, back)

        for b in range(NB):
            sl = slice(b * BW, (b + 1) * BW)
            xf = x_ref[:, sl]
            xs = xf.astype(CD)
            sp = _softplus(-lam_ref[:, sl])
            r, ig, ab, sq = _gates(xs, wa_ref[b], wi_ref[b], ba_ref[:, sl], bi_ref[:, sl], sp)
            du = lam_s[:, sl]
            da = du * h_nb[:, sl]
            d_ig = du * sq * xf
            d_sq = du * ig * xf
            d_la = da * ab - d_sq * (ab * ab) / sq
            d_pa = d_la * (-LRU_C * sp) * r * (1.0 - r)
            d_pi = d_ig * ig * (1.0 - ig)
            dsp_s[:, sl] += jnp.sum(d_la * (-LRU_C * r), axis=0, keepdims=True)
            dba_ref[:, sl] += jnp.sum(d_pa, axis=0, keepdims=True)
            dbi_ref[:, sl] += jnp.sum(d_pi, axis=0, keepdims=True)
            d_pa_c, d_pi_c = d_pa.astype(CD), d_pi.astype(CD)
            dwa_ref[b] += lax.dot_general(xs, d_pa_c, TN, preferred_element_type=F32)
            dwi_ref[b] += lax.dot_general(xs, d_pi_c, TN, preferred_element_type=F32)
            dx = (du * sq * ig + lax.dot_general(d_pa_c, wa_ref[b], NT, preferred_element_type=F32)
                  + lax.dot_general(d_pi_c, wi_ref[b], NT, preferred_element_type=F32))
            if add_ref is not None:
                dx = dx + add_ref[:, sl]
            dx_ref[:, sl] = dx

        @pl.when(i == n - 1)
        def _():
            dlam_ref[...] = -dsp_s[...] * _sigmoid(-lam_ref[...])

    vec = BS((1, C), lambda i: (0, 0))
    main, prev, nxt = _halo_specs(T, C, HALO8, n, S, rev=back)
    wspec = lambda which: BS((None, None, NB, BW, BW), lambda i: (which, d, 0, 0, 0))
    acc = BS((NB, BW, BW), lambda i: (0, 0, 0))
    ins = [dhs, a, a, a, h, h, h, xc, wai, wai, ba, bi, lam]
    in_specs = [main, main, prev, nxt, main, prev, nxt, main, wspec(0), wspec(1), vec, vec, vec]
    if dxc_in is not None:
        ins.append(dxc_in)
        in_specs.append(main)
    return pl.pallas_call(
        body, name=name, grid=(n,),
        out_shape=[jax.ShapeDtypeStruct((S, C), F32), jax.ShapeDtypeStruct((NB, BW, BW), F32),
                   jax.ShapeDtypeStruct((NB, BW, BW), F32), jax.ShapeDtypeStruct((1, C), F32),
                   jax.ShapeDtypeStruct((1, C), F32), jax.ShapeDtypeStruct((1, C), F32)],
        in_specs=in_specs, out_specs=[main, acc, acc, vec, vec, vec],
        scratch_shapes=[pltpu.VMEM((T, C), F32), pltpu.VMEM((T, C), F32), pltpu.VMEM((8, C), F32),
                        pltpu.VMEM((1, C), F32)],
        compiler_params=_params("arbitrary"),
    )(*ins)


def _conv_bwd(dxc, xy, w8):
    S, C = dxc.shape
    T = _tile(S, 256)
    n = S // T

    def body(d_ref, dp_ref, dn_ref, x_ref, xp_ref, xn_ref, w_ref, dx_ref, dw_ref):
        i = pl.program_id(0)
        dv = _round(d_ref[...])
        d_above = jnp.where(i > 0, _round(dp_ref[...]), 0.0)
        d_below = jnp.where(i < n - 1, _round(dn_ref[...]), 0.0)
        wv = _round(w_ref[...])
        dx = (wv[0:1, :] * _shift_up(dv, 2, d_below) + wv[1:2, :] * _shift_up(dv, 1, d_below)
              + wv[2:3, :] * dv + wv[3:4, :] * _shift_down(dv, 1, d_above))
        dx_ref[...] = dx.astype(dx_ref.dtype)
        xv = _round(x_ref[...])
        above = jnp.where(i > 0, _round(xp_ref[...]), 0.0)
        below = jnp.where(i < n - 1, _round(xn_ref[...]), 0.0)
        rows = [jnp.sum(dv * _shift_down(xv, 2, above), axis=0, keepdims=True),
                jnp.sum(dv * _shift_down(xv, 1, above), axis=0, keepdims=True),
                jnp.sum(dv * xv, axis=0, keepdims=True),
                jnp.sum(dv * _shift_up(xv, 1, below), axis=0, keepdims=True),
                jnp.sum(dv, axis=0, keepdims=True)]
        rid = lax.broadcasted_iota(jnp.int32, (8, C), 0)
        part = jnp.zeros((8, C), F32)
        for j, rv in enumerate(rows):
            part = jnp.where(rid == j, rv, part)

        @pl.when(i == 0)
        def _():
            dw_ref[...] = part

        @pl.when(i > 0)
        def _():
            dw_ref[...] += part

    dmain, dprev, dnext = _halo_specs(T, C, HALO8, n, S)
    xmain, xprev, xnext = _halo_specs(T, C, HALO8, n, S)
    w = BS((8, C), lambda i: (0, 0))
    return pl.pallas_call(
        body, name="conv_bwd", grid=(n,),
        out_shape=[jax.ShapeDtypeStruct((S, C), CD), jax.ShapeDtypeStruct((8, C), F32)],
        in_specs=[dmain, dprev, dnext, xmain, xprev, xnext, w], out_specs=[dmain, w],
        compiler_params=_params("arbitrary"),
    )(dxc, dxc, dxc, xy, xy, xy, w8)


def _adamw(name, lands, w, m, v, pick=None):
    c1 = 1.0 - ADAM_B1 ** ADAM_STEP
    c2 = 1.0 - ADAM_B2 ** ADAM_STEP
    L, C = w.shape[0], w.shape[-1]
    w3, m3, v3 = (t.reshape(L, -1, C) for t in (w, m, v))
    R = w3.shape[1]
    tr = R
    for cand in (1024, 512, 256, 128, 64, 32, 16, 8):
        if R % cand == 0 and cand * C * 4 <= (1 << 20):
            tr = cand
            break

    def body(*refs):
        l_refs = refs[:L]
        w_ref, m_ref, v_ref, g_out, d_out, m_out, v_out = refs[L:]

        def update(l_ref):
            g = l_ref[0].astype(F32)
            for s in range(1, NDEV):
                g = g + l_ref[s].astype(F32)
            mn = ADAM_B1 * m_ref[...] + (1.0 - ADAM_B1) * g
            vn = ADAM_B2 * v_ref[...] + (1.0 - ADAM_B2) * (g * g)
            g_out[...] = g
            m_out[...] = mn
            v_out[...] = vn
            d_out[...] = -ADAM_LR * ((mn / c1) / (jnp.sqrt(vn / c2) + ADAM_EPS) + ADAM_WD * w_ref[...])

        for k in range(L):
            pl.when(pl.program_id(0) == k)(functools.partial(update, l_refs[k]))

    if pick is None:
        lands = [t.reshape(NDEV, R, C) for t in lands]
        land_specs = [BS((NDEV, tr, C), lambda l, i, k=k: (0, jnp.where(l == k, i, 0), 0)) for k in range(L)]
    else:
        lands = [t.reshape(NDEV, -1, R, C) for t in lands]
        land_specs = [BS((NDEV, None, tr, C), lambda l, i, k=k: (0, pick, jnp.where(l == k, i, 0), 0)) for k in range(L)]
    spec = BS((None, tr, C), lambda l, i: (l, i, 0))
    sds = jax.ShapeDtypeStruct(w3.shape, F32)
    outs = pl.pallas_call(
        body, name=name, grid=(L, R // tr), out_shape=[sds] * 4, in_specs=land_specs + [spec, spec, spec],
        out_specs=[spec] * 4, compiler_params=_params("arbitrary", "arbitrary"),
    )(*lands, w3, m3, v3)
    return [o.reshape(w.shape) for o in outs]


def kernel(x, norm_mix, norm_ffn, attn_w_qkv, attn_q_gain, attn_k_gain, attn_w_o, rnn_w_in, rnn_conv_w, rnn_conv_b, rnn_w_a, rnn_b_a, rnn_w_i, rnn_b_i, rnn_lambda, rnn_w_out, ffn_w_gate, ffn_w_up, ffn_w_down, loss_target, m_norm_mix, m_norm_ffn, m_attn_w_qkv, m_attn_q_gain, m_attn_k_gain, m_attn_w_o, m_rnn_w_in, m_rnn_conv_w, m_rnn_conv_b, m_rnn_w_a, m_rnn_b_a, m_rnn_w_i, m_rnn_b_i, m_rnn_lambda, m_rnn_w_out, m_ffn_w_gate, m_ffn_w_up, m_ffn_w_down, v_norm_mix, v_norm_ffn, v_attn_w_qkv, v_attn_q_gain, v_attn_k_gain, v_attn_w_o, v_rnn_w_in, v_rnn_conv_w, v_rnn_conv_b, v_rnn_w_a, v_rnn_b_a, v_rnn_w_i, v_rnn_b_i, v_rnn_lambda, v_rnn_w_out, v_ffn_w_gate, v_ffn_w_up, v_ffn_w_down):
    S, D = x.shape[1], x.shape[2]
    depth = norm_mix.shape[0]
    n_attn, n_rnn = attn_w_qkv.shape[0], rnn_w_in.shape[0]
    QS = attn_w_qkv.shape[2]
    NQ = D // HEAD_DIM
    NKV = (QS * NDEV // HEAD_DIM - NQ) // 2
    G = NQ // NKV
    OS = attn_w_o.shape[1]
    IS = rnn_w_in.shape[2]
    C = rnn_w_out.shape[1] * NDEV
    CS = C // NDEV
    NB, BWS, BW = rnn_w_a.shape[2], rnn_w_a.shape[3], rnn_w_a.shape[4]
    F8 = ffn_w_gate.shape[2]
    tm = _tile(S, 1024)
    tn = _tile(D, 1024)
    tk = _tile(S, 2048)

    def rows_at(parts, total):
        out = None
        for at, p in parts:
            cfg = [(0, 0)] * (p.ndim - 2) + [(at, total - at - p.shape[-2]), (0, 0)]
            out = jnp.pad(p, cfg) if out is None else out + jnp.pad(p, cfg)
        return out

    def pack_small(cw, cb, b_a, b_i, lam):
        return rows_at([(0, cw), (8, cb[:, None, :]), (16, b_a), (24, b_i), (32, lam)], SMALL_ROWS)

    def unpack_small(p):
        return p[:, 0:4], p[:, 8], p[:, 16:18], p[:, 24:26], p[:, 32:34]

    gu_sh = jnp.stack([ffn_w_gate, ffn_w_up], axis=1).astype(CD)
    ai_sh = jnp.stack([rnn_w_a, rnn_w_i], axis=1).astype(CD)
    small_sh = pack_small(rnn_conv_w, rnn_conv_b, rnn_b_a, rnn_b_i, rnn_lambda)
    Wqkv0, Wo0 = _exchange("gather_first", [attn_w_qkv[0:1].astype(CD), attn_w_o[0:1].astype(CD)], scatter=False)
    ffn0_handle, ffn0_token = _exchange_start("gather_ffn0_start", [gu_sh[0:1], ffn_w_down[0:1].astype(CD)], scatter=False,
                                              after=Wqkv0)
    gathers, newest = {}, ffn0_token
    for i in range(1, depth):
        l = i // 2
        if i % 2 == 0:
            arrs = [attn_w_qkv[l:l + 1].astype(CD), attn_w_o[l:l + 1].astype(CD)]
        else:
            arrs = [rnn_w_in[l:l + 1].astype(CD), rnn_w_out[l:l + 1].astype(CD), ai_sh[l:l + 1]] + ([small_sh] if l == 0 else [])
        gathers[i], newest = _exchange_start(f"gather_layer{i}_start", arrs + [gu_sh[i:i + 1], ffn_w_down[i:i + 1].astype(CD)],
                                             scatter=False, after=newest)
    started = newest[0, 0]
    wqkv, wo = {0: (Wqkv0, 0)}, {0: (Wo0, 0)}
    win, wout, wai, wgu, wd = {}, {}, {}, {}, {}
    cos_t, sin_t = _rope_tables(S)

    def proj_cols(name, a, W, l, width, dtype):
        return _mm(name, NN, a, W, BS((tm, D), lambda i, j, r: (i, 0)),
                   BS((None, None, D, width), lambda i, j, r: (j, l, 0, 0)),
                   BS((tm, width), lambda i, j, r: (i, j)), (S, NDEV * width), dtype, (S // tm, NDEV, 1))

    def proj_cols_dx(name, dy, W, l, width):
        ns = 4
        return _mm(name, NT, dy, W, BS((tm, ns * width), lambda i, j, r: (i, r)),
                   BS((ns, None, tn, width), lambda i, j, r: (r, l, j, 0)),
                   BS((tm, tn), lambda i, j, r: (i, j)), (S, D), F32, (S // tm, D // tn, NDEV // ns),
                   pairs=lambda a_ref, b_ref: [(a_ref[:, s * width:(s + 1) * width], b_ref[s]) for s in range(ns)])

    def proj_cols_dw(name, a, dy, width):
        return _mm(name, TN, a, dy, BS((tk, tn), lambda i, j, r: (r, i)),
                   BS((tk, width), lambda i, j, r: (r, j)),
                   BS((None, tn, width), lambda i, j, r: (j, i, 0)), (NDEV, D, width), CD, (D // tn, NDEV, S // tk))

    def proj_rows(name, a, W, l, rows, resid):
        return _mm(name, NN, a, W, BS((tm, NDEV * rows), lambda i, j, r: (i, 0)),
                   BS((NDEV, None, rows, tn), lambda i, j, r: (0, l, 0, j)),
                   BS((tm, tn), lambda i, j, r: (i, j)), (S, D), F32, (S // tm, D // tn, 1), add=resid,
                   pairs=lambda a_ref, b_ref: [(a_ref[...], b_ref[...].reshape(NDEV * rows, tn))])

    def proj_rows_dx(name, dy, W, l, rows, dtype):
        return _mm(name, NT, dy, W, BS((tm, D), lambda i, j, r: (i, 0)),
                   BS((None, None, rows, D), lambda i, j, r: (j, l, 0, 0)),
                   BS((tm, rows), lambda i, j, r: (i, j)), (S, NDEV * rows), dtype, (S // tm, NDEV, 1))

    def proj_rows_dw(name, a, dy, rows):
        width = NDEV * rows
        tw = _tile(width, 1024)
        out = _mm(name, TN, a, dy, BS((tk, tw), lambda i, j, r: (r, i)), BS((tk, tn), lambda i, j, r: (r, j)),
                  BS((tw, tn), lambda i, j, r: (i, j)), (width, D), CD, (width // tw, D // tn, S // tk))
        return out.reshape(NDEV, rows, D)

    saved = []
    xs = x[0]
    for i in range(depth):
        l = i // 2
        h1 = _rms_fwd("norm_mix_fwd", xs, norm_mix[i:i + 1] + started if i == 0 else norm_mix[i:i + 1])
        if i % 2 == 0:
            qkv = proj_cols("attn_qkv", h1, *wqkv[l], QS, CD)
            qr, kr = _qk_prep_fwd(qkv, attn_q_gain[l:l + 1], attn_k_gain[l:l + 1], cos_t, sin_t, NQ, NKV)
            o, lse = _flash_fwd(qr, kr, qkv, NQ, NKV)
            x1 = proj_rows("attn_out", o, *wo[l], OS, xs)
            mix = (qkv, qr, kr, o, lse)
        else:
            xy = proj_cols("rnn_in", h1, *win[l], IS, F32)
            w8 = jnp.pad(conv_w_f[l], ((0, 4), (0, 0)))
            xc = _conv_fwd(xy, w8, conv_b_f[l][None])
            hs, avs = [], []
            for d in range(2):
                hd, ad = _lru_fwd("lru_fwd_rev" if d else "lru_fwd", xc, wai[l], d, b_a_f[l, d][None], b_i_f[l, d][None],
                                  lam_f[l, d][None], bool(d))
                hs.append(hd)
                avs.append(ad)
            ro = _rnn_out_fwd(hs[0], hs[1], xy)
            x1 = proj_rows("rnn_out", ro, *wout[l], CS, xs)
            mix = (xy, w8, xc, hs, avs, ro)
        if i == 0:
            Wgu0, Wd0 = _exchange_wait("gather_ffn0_wait", ffn0_handle, x1)
            wgu[0], wd[0] = (Wgu0, 0), (Wd0, 0)
        h2 = _rms_fwd("norm_ffn_fwd", x1, norm_ffn[i:i + 1])
        Wgu, li = wgu[i]
        gu = _mm("ffn_gate_up", NN, h2, Wgu, BS((tm, D), lambda i_, j, r: (i_, 0)),
                 BS((None, None, None, D, F8), lambda i_, j, r, li=li: (j % NDEV, li, j // NDEV, 0, 0)),
                 BS((None, None, tm, F8), lambda i_, j, r: (j // NDEV, j % NDEV, i_, 0)),
                 (2, NDEV, S, F8), CD, (S // tm, 2 * NDEV, 1))
        act = _swiglu_fwd(gu)
        Wd, li = wd[i]
        x2 = _mm("ffn_down", NN, act, Wd, BS((2, tm, F8), lambda i_, j, r: (r, i_, 0)),
                 BS((2, None, F8, tn), lambda i_, j, r, li=li: (r, li, 0, j)),
                 BS((tm, tn), lambda i_, j, r: (i_, j)), (S, D), F32, (S // tm, D // tn, NDEV // 2), add=x1,
                 pairs=lambda a_ref, b_ref: [(a_ref[s], b_ref[s]) for s in range(2)])
        saved.append((xs, h1, mix, x1, h2, gu, act))
        xs = x2
        if i + 1 < depth:
            k, lk = i + 1, (i + 1) // 2
            got = _exchange_wait(f"gather_layer{k}_wait", gathers[k], x2)
            wgu[k], wd[k] = (got[-2], 0), (got[-1], 0)
            if k % 2 == 0:
                wqkv[lk], wo[lk] = (got[0], 0), (got[1], 0)
            else:
                win[lk], wout[lk] = (got[0], 0), (got[1], 0)
                wai[lk] = jnp.transpose(got[2], (1, 2, 3, 4, 0, 5, 6)).reshape(2, 2, NB, BW, BW)
                if lk == 0:
                    small = jnp.transpose(got[3], (1, 2, 0, 3)).reshape(n_rnn, SMALL_ROWS, C)
                    conv_w_f, conv_b_f, b_a_f, b_i_f, lam_f = unpack_small(small)

    dx, dxb, sq_err = _loss_head(xs, loss_target[0])
    loss = lax.psum(0.5 * sq_err[0, 0] / D, ("x", "y", "c"))

    g_norm_mix, g_norm_ffn = [None] * depth, [None] * depth
    g_qkv, g_wo, g_qg, g_kg = [None] * n_attn, [None] * n_attn, [None] * n_attn, [None] * n_attn
    g_win, g_wout, g_ai, g_small = [None] * n_rnn, [None] * n_rnn, [None] * n_rnn, [None] * n_rnn
    g_gu, g_down = [None] * depth, [None] * depth
    scat, sent = {}, None
    for i in reversed(range(depth)):
        l = i // 2
        xs, h1, mix, x1, h2, gu, act = saved[i]
        gain_ffn = norm_ffn[i:i + 1] if sent is None else norm_ffn[i:i + 1] + sent[0, 0]
        Wd, li = wd[i]
        dact = _mm("ffn_down_dx", NT, dxb, Wd, BS((tm, D), lambda i_, j, r: (i_, 0)),
                   BS((None, None, F8, D), lambda i_, j, r, li=li: (j, li, 0, 0)),
                   BS((None, tm, F8), lambda i_, j, r: (j, i_, 0)), (NDEV, S, F8), CD, (S // tm, NDEV, 1))
        g_down[i] = _mm("ffn_down_dw", TN, act, dxb, BS((None, tk, F8), lambda i_, j, r: (i_, r, 0)),
                        BS((tk, tn), lambda i_, j, r: (r, j)),
                        BS((None, F8, tn), lambda i_, j, r: (i_, 0, j)), (NDEV, F8, D), CD, (NDEV, D // tn, S // tk))
        dgu = _swiglu_bwd(dact, gu)
        Wgu, li = wgu[i]
        dh2 = _mm("ffn_gate_up_dx", NT, dgu, Wgu, BS((None, 4, tm, F8), lambda i_, j, r: (r // 2, r % 2, i_, 0)),
                  BS((4, None, None, tn, F8), lambda i_, j, r, li=li: (r % 2, li, r // 2, j, 0)),
                  BS((tm, tn), lambda i_, j, r: (i_, j)), (S, D), F32, (S // tm, D // tn, 2 * NDEV // 4),
                  pairs=lambda a_ref, b_ref: [(a_ref[s], b_ref[s]) for s in range(4)])
        g_gu[i] = _mm("ffn_gate_up_dw", TN, h2, dgu, BS((tk, tn), lambda i_, j, r: (r, i_)),
                      BS((None, None, tk, F8), lambda i_, j, r: (j // NDEV, j % NDEV, r, 0)),
                      BS((None, None, tn, F8), lambda i_, j, r: (j % NDEV, j // NDEV, i_, 0)),
                      (NDEV, 2, D, F8), CD, (D // tn, 2 * NDEV, S // tk))
        dx, dxb, g_norm_ffn[i] = _rms_bwd("norm_ffn_bwd", dh2, x1, gain_ffn, dx)
        scat["ffn", i], sent = _exchange_start(f"scatter_ffn{i}_start", [g_gu[i], g_down[i]], scatter=True)
        gain_mix = norm_mix[i:i + 1] + sent[0, 0]
        if i % 2 == 0:
            qkv, qr, kr, o, lse = mix
            do = proj_rows_dx("attn_out_dx", dxb, *wo[l], OS, CD)
            g_wo[l] = proj_rows_dw("attn_out_dw", o, dxb, OS)
            delta = _attn_delta(do, o, NQ, NKV)
            dq, dk, dv = _flash_bwd(qr, kr, qkv, do, jnp.transpose(lse[:, :, :G], (0, 2, 1)),
                                    jnp.transpose(delta[:, :, :G], (0, 2, 1)), NQ, NKV)
            dqkv, g_qg[l], g_kg[l] = _qk_prep_bwd(qkv, dq, dk, dv, attn_q_gain[l:l + 1], attn_k_gain[l:l + 1],
                                                  cos_t, sin_t, NQ, NKV)
            g_qkv[l] = proj_cols_dw("attn_qkv_dw", h1, dqkv, QS)
            dh1 = proj_cols_dx("attn_qkv_dx", dqkv, *wqkv[l], QS)
        else:
            xy, w8, xc, hs, avs, ro = mix
            dro = proj_rows_dx("rnn_out_dx", dxb, *wout[l], CS, F32)
            g_wout[l] = proj_rows_dw("rnn_out_dw", ro, dxb, CS)
            dhs, dyb = _rnn_out_bwd(dro, hs[0], hs[1], xy)
            dxc, dwa, dwi, dba, dbi, dlam = None, [], [], [], [], []
            for d in range(2):
                res = _lru_bwd("lru_bwd_rev" if d else "lru_bwd", dhs, avs[d], hs[d], xc, wai[l], d, b_a_f[l, d][None],
                               b_i_f[l, d][None], lam_f[l, d][None], bool(d), dxc)
                dxc = res[0]
                for lst, val in zip((dwa, dwi, dba, dbi, dlam), res[1:]):
                    lst.append(val)
            dxb_branch, dw8 = _conv_bwd(dxc, xy, w8)
            dxy = jnp.concatenate([dxb_branch, dyb], axis=1)
            g_win[l] = proj_cols_dw("rnn_in_dw", h1, dxy, IS)
            dh1 = proj_cols_dx("rnn_in_dx", dxy, *win[l], IS)
            dai = jnp.stack([jnp.stack(dwa), jnp.stack(dwi)])
            g_ai[l] = jnp.transpose(dai.reshape(2, 2, NB, NDEV, BWS, BW), (3, 0, 1, 2, 4, 5))
            sm = rows_at([(0, dw8[0:4]), (8, dw8[4:5]), (16, dba[0]), (17, dba[1]), (24, dbi[0]), (25, dbi[1]),
                          (32, dlam[0]), (33, dlam[1])], SMALL_ROWS)
            g_small[l] = jnp.transpose(sm.reshape(SMALL_ROWS, NDEV, CS), (1, 0, 2))
        dx, dxb, g_norm_mix[i] = _rms_bwd("norm_mix_bwd", dh1, xs, gain_mix, dx)
        if i % 2 == 1:
            scat["mix", i], sent = _exchange_start(f"scatter_mix{i}_start", [g_win[l], g_wout[l], g_ai[l], g_small[l]], scatter=True)
        elif i > 0:
            scat["mix", i], sent = _exchange_start(f"scatter_mix{i}_start", [g_qkv[l], g_wo[l]], scatter=True)

    L_gu, L_down = [None] * depth, [None] * depth
    L_qkv, L_wo = [None] * n_attn, [None] * n_attn
    L_win, L_wout, L_ai, L_small = [None] * n_rnn, [None] * n_rnn, [None] * n_rnn, [None] * n_rnn
    norm_part = rows_at(list(enumerate(g_norm_mix + g_norm_ffn)), 2 * depth)
    gain_part = rows_at(list(enumerate(g_qg + g_kg)), 8)
    to_all = lambda t: jnp.broadcast_to(t[None], (NDEV,) + t.shape)
    L_qkv[0], L_wo[0], L_norm, L_gain = _exchange("scatter_last", [g_qkv[0], g_wo[0], to_all(norm_part), to_all(gain_part)],
                                                  scatter=True)
    for i in reversed(range(depth)):
        l = i // 2
        L_gu[i], L_down[i] = _exchange_wait(f"scatter_ffn{i}_wait", scat["ffn", i], L_norm)
        if i % 2 == 1:
            L_win[l], L_wout[l], L_ai[l], L_small[l] = _exchange_wait(f"scatter_mix{i}_wait", scat["mix", i], L_norm)
        elif i > 0:
            L_qkv[l], L_wo[l] = _exchange_wait(f"scatter_mix{i}_wait", scat["mix", i], L_norm)

    out = {}
    out["attn_w_qkv"] = _adamw("adamw_qkv", L_qkv, attn_w_qkv, m_attn_w_qkv, v_attn_w_qkv)
    out["attn_w_o"] = _adamw("adamw_wo", L_wo, attn_w_o, m_attn_w_o, v_attn_w_o)
    out["rnn_w_in"] = _adamw("adamw_win", L_win, rnn_w_in, m_rnn_w_in, v_rnn_w_in)
    out["rnn_w_out"] = _adamw("adamw_wout", L_wout, rnn_w_out, m_rnn_w_out, v_rnn_w_out)
    out["ffn_w_down"] = _adamw("adamw_down", L_down, ffn_w_down, m_ffn_w_down, v_ffn_w_down)
    out["ffn_w_gate"] = _adamw("adamw_gate", L_gu, ffn_w_gate, m_ffn_w_gate, v_ffn_w_gate, pick=0)
    out["ffn_w_up"] = _adamw("adamw_up", L_gu, ffn_w_up, m_ffn_w_up, v_ffn_w_up, pick=1)
    out["rnn_w_a"] = _adamw("adamw_w_a", L_ai, rnn_w_a, m_rnn_w_a, v_rnn_w_a, pick=0)
    out["rnn_w_i"] = _adamw("adamw_w_i", L_ai, rnn_w_i, m_rnn_w_i, v_rnn_w_i, pick=1)
    small_res = _adamw("adamw_small", L_small, small_sh,
                       pack_small(m_rnn_conv_w, m_rnn_conv_b, m_rnn_b_a, m_rnn_b_i, m_rnn_lambda),
                       pack_small(v_rnn_conv_w, v_rnn_conv_b, v_rnn_b_a, v_rnn_b_i, v_rnn_lambda))
    for name, vals in zip(("rnn_conv_w", "rnn_conv_b", "rnn_b_a", "rnn_b_i", "rnn_lambda"),
                          zip(*[unpack_small(r) for r in small_res])):
        out[name] = list(vals)
    one = lambda t: t[None]
    for name, land, w, m, v in (("norm_mix", L_norm[:, :depth], norm_mix, m_norm_mix, v_norm_mix),
                                ("norm_ffn", L_norm[:, depth:], norm_ffn, m_norm_ffn, v_norm_ffn),
                                ("attn_q_gain", L_gain[:, :n_attn], attn_q_gain, m_attn_q_gain, v_attn_q_gain),
                                ("attn_k_gain", L_gain[:, n_attn:2 * n_attn], attn_k_gain, m_attn_k_gain, v_attn_k_gain)):
        out[name] = [r[0] for r in _adamw("adamw_" + name, [land], one(w), one(m), one(v))]

    order = ["norm_mix", "norm_ffn", "attn_w_qkv", "attn_q_gain", "attn_k_gain", "attn_w_o", "rnn_w_in", "rnn_conv_w",
             "rnn_conv_b", "rnn_w_a", "rnn_b_a", "rnn_w_i", "rnn_b_i", "rnn_lambda", "rnn_w_out", "ffn_w_gate",
             "ffn_w_up", "ffn_w_down"]
    return (loss, dx[None], *[out[k][0] for k in order], *[out[k][1] for k in order], *[out[k][2] for k in order],
            *[out[k][3] for k in order])
```

```python
import functools
import math

import jax
import jax.numpy as jnp
from jax import lax
from jax.experimental import pallas as pl
from jax.experimental.pallas import tpu as pltpu

F32 = jnp.float32
CD = jnp.bfloat16
NDEV = 8
HEAD_DIM = 128
GRID_W = 64
ROPE_THETA = 10000.0
ROPE_FREQS = 32
EPS = 1e-6
LRU_C = 8.0
SMALL_ROWS = 40
ADAM_LR, ADAM_B1, ADAM_B2, ADAM_EPS, ADAM_WD, ADAM_STEP = 0.001, 0.9, 0.999, 1e-08, 0.01, 10
V7X_VMEM_LIMIT = 56 * 1024 * 1024
FWD_TQ, FWD_TK = 1024, 2048
BWD_TQ, BWD_TK = 1024, 2048
BS = pl.BlockSpec
MESH = pl.DeviceIdType.MESH
NN = (((1,), (0,)), ((), ()))
NT = (((1,), (1,)), ((), ()))
TN = (((0,), (0,)), ((), ()))


def _params(*sem):
    return pltpu.CompilerParams(dimension_semantics=sem, vmem_limit_bytes=V7X_VMEM_LIMIT)


def _tile(n, want):
    return want if n % want == 0 else n


def _exchange(name, arrs, scatter):
    n = len(arrs)
    out_shape = [jax.ShapeDtypeStruct(a.shape if scatter else (NDEV,) + a.shape, a.dtype) for a in arrs]

    def body(*refs):
        ins, outs = refs[:n], refs[n:2 * n]
        send_sems, recv_sems, local_sems = refs[2 * n:]
        x, y, c = lax.axis_index("x"), lax.axis_index("y"), lax.axis_index("c")
        me = 4 * x + 2 * y + c
        local, sends, recvs = [], [], []
        for a in range(n):
            cp = pltpu.make_async_copy(ins[a].at[me] if scatter else ins[a], outs[a].at[me], local_sems.at[a])
            cp.start()
            local.append(cp)
        for k in range(1, NDEV):
            px = 1 - x if k & 4 else x
            py = 1 - y if k & 2 else y
            pc = 1 - c if k & 1 else c
            peer = 4 * px + 2 * py + pc
            for a in range(n):
                src = ins[a].at[peer] if scatter else ins[a]
                kw = dict(send_sem=send_sems.at[a, k - 1], recv_sem=recv_sems.at[a, k - 1],
                          device_id=(px, py, pc), device_id_type=MESH)
                cp = pltpu.make_async_remote_copy(src_ref=src, dst_ref=outs[a].at[me], **kw)
                cp.start()
                sends.append(cp)
                recvs.append(pltpu.make_async_remote_copy(src_ref=src, dst_ref=outs[a].at[peer], **kw))
        for cp in local:
            cp.wait()
        for cp in sends:
            cp.wait_send()
        for cp in recvs:
            cp.wait_recv()

    return pl.pallas_call(
        body, name=name, out_shape=out_shape,
        in_specs=[BS(memory_space=pl.ANY)] * n, out_specs=[BS(memory_space=pl.ANY)] * n,
        scratch_shapes=[pltpu.SemaphoreType.DMA((n, NDEV - 1)), pltpu.SemaphoreType.DMA((n, NDEV - 1)),
                        pltpu.SemaphoreType.DMA((n,))],
    )(*arrs)


HBM = BS(memory_space=pltpu.HBM)
SEM = BS(memory_space=pltpu.SEMAPHORE)
EFFECT = pltpu.SideEffectType.DATAFLOW_SIDE_EFFECTING


def _peers(x, y, c):
    out = []
    for k in range(1, NDEV):
        px = 1 - x if k & 4 else x
        py = 1 - y if k & 2 else y
        pc = 1 - c if k & 1 else c
        out.append(((px, py, pc), 4 * px + 2 * py + pc))
    return out


def _exchange_start(name, arrs, scatter, after=None):
    n = len(arrs)
    lands = [lax.empty(a.shape if scatter else (NDEV,) + a.shape, a.dtype) for a in arrs]
    extra = [] if after is None else [after]

    def body(*refs):
        ins, outs = refs[:n], refs[n:2 * n]
        send_sems, recv_sems, token = refs[2 * n + len(extra)], refs[2 * n + len(extra) + 1], refs[-1]
        x, y, c = lax.axis_index("x"), lax.axis_index("y"), lax.axis_index("c")
        me = 4 * x + 2 * y + c
        for k, (dev, peer) in enumerate(_peers(x, y, c)):
            for a in range(n):
                pltpu.make_async_remote_copy(
                    src_ref=ins[a].at[peer] if scatter else ins[a], dst_ref=outs[a].at[me],
                    send_sem=send_sems.at[a * (NDEV - 1) + k], recv_sem=recv_sems.at[a * (NDEV - 1) + k],
                    device_id=dev, device_id_type=MESH).start()
        token[...] = jnp.zeros(token.shape, F32)

    sems = pltpu.SemaphoreType.DMA((n * (NDEV - 1),))
    res = pl.pallas_call(
        body, name=name,
        out_shape=(sems, sems, *[pltpu.HBM(a.shape, a.dtype) for a in arrs], *[pltpu.HBM(l.shape, l.dtype) for l in lands],
                   jax.ShapeDtypeStruct((8, 128), F32)),
        in_specs=[HBM] * (2 * n) + [BS(memory_space=pl.ANY)] * len(extra),
        out_specs=(SEM, SEM, *([HBM] * (2 * n)), BS(memory_space=pltpu.VMEM)),
        input_output_aliases={i: i + 2 for i in range(2 * n)},
        compiler_params=pltpu.CompilerParams(has_side_effects=EFFECT),
    )(*[pltpu.with_memory_space_constraint(t, pltpu.HBM) for t in list(arrs) + lands], *extra)
    return (res[0], res[1], res[2:2 + n], res[2 + n:2 + 2 * n], scatter), res[-1]


def _exchange_wait(name, handle, after):
    send_sems, recv_sems, srcs, lands, scatter = handle
    n = len(srcs)

    def body(*refs):
        ins, outs = refs[:n], refs[n:2 * n]
        s_sems, r_sems = refs[2 * n], refs[2 * n + 1]
        x, y, c = lax.axis_index("x"), lax.axis_index("y"), lax.axis_index("c")
        me = 4 * x + 2 * y + c
        for k, (dev, peer) in enumerate(_peers(x, y, c)):
            for a in range(n):
                src = ins[a].at[peer] if scatter else ins[a]
                kw = dict(send_sem=s_sems.at[a * (NDEV - 1) + k], recv_sem=r_sems.at[a * (NDEV - 1) + k], device_id=dev,
                          device_id_type=MESH)
                pltpu.make_async_remote_copy(src_ref=src, dst_ref=outs[a].at[me], **kw).wait_send()
                pltpu.make_async_remote_copy(src_ref=src, dst_ref=outs[a].at[peer], **kw).wait_recv()

    res = pl.pallas_call(
        body, name=name, out_shape=tuple(pltpu.HBM(t.shape, t.dtype) for t in list(srcs) + list(lands)),
        in_specs=[HBM] * (2 * n) + [SEM, SEM, BS(memory_space=pl.ANY)], out_specs=tuple([HBM] * (2 * n)),
        input_output_aliases={i: i for i in range(2 * n)},
        compiler_params=pltpu.CompilerParams(has_side_effects=EFFECT),
    )(*srcs, *lands, send_sems, recv_sems, after)
    me = 4 * lax.axis_index("x") + 2 * lax.axis_index("y") + lax.axis_index("c")
    out = []
    for src, land in zip(res[:n], res[n:]):
        own = lax.dynamic_index_in_dim(src, me, 0, keepdims=True) if scatter else src[None]
        out.append(lax.dynamic_update_slice_in_dim(land, own, me, 0))
    return out


def _mm(name, dims, a, b, a_spec, b_spec, o_spec, out_shape, out_dtype, grid, add=None, pairs=None):
    nr = grid[2]
    o_blk = tuple(d for d in o_spec.block_shape if d is not None)

    def body(*refs):
        a_ref, b_ref = refs[0], refs[1]
        add_ref = refs[2] if add is not None else None
        o_ref = refs[3] if add is not None else refs[2]
        part = None
        for av, bv in ([(a_ref[...], b_ref[...])] if pairs is None else pairs(a_ref, b_ref)):
            prod = lax.dot_general(av.astype(CD), bv.astype(CD), dims, preferred_element_type=F32)
            part = prod if part is None else part + prod

        def finish(v):
            if add_ref is not None:
                v = v + add_ref[...].astype(F32)
            o_ref[...] = v.astype(o_ref.dtype)

        if nr == 1:
            finish(part)
        else:
            acc = refs[-1]
            r = pl.program_id(2)

            @pl.when(r == 0)
            def _():
                acc[...] = part

            @pl.when(r > 0)
            def _():
                acc[...] += part

            @pl.when(r == nr - 1)
            def _():
                finish(acc[...])

    return pl.pallas_call(
        body, name=name, grid=grid, out_shape=jax.ShapeDtypeStruct(out_shape, out_dtype),
        in_specs=[a_spec, b_spec] + ([o_spec] if add is not None else []), out_specs=o_spec,
        scratch_shapes=[pltpu.VMEM(o_blk, F32)] if nr > 1 else [],
        compiler_params=_params("parallel", "parallel", "arbitrary"),
    )(*([a, b] + ([add] if add is not None else [])))


def _rms_fwd(name, x, g):
    S, D = x.shape
    tr = _tile(S, 512)

    def body(x_ref, g_ref, o_ref):
        xv = x_ref[...]
        r = lax.rsqrt(jnp.mean(xv * xv, axis=-1, keepdims=True) + EPS)
        o_ref[...] = (xv * r * g_ref[...]).astype(o_ref.dtype)

    return pl.pallas_call(
        body, name=name, grid=(S // tr,), out_shape=jax.ShapeDtypeStruct((S, D), CD),
        in_specs=[BS((tr, D), lambda i: (i, 0)), BS((1, D), lambda i: (0, 0))],
        out_specs=BS((tr, D), lambda i: (i, 0)), compiler_params=_params("parallel"),
    )(x, g)


def _rms_bwd(name, dh, x, g, dx_in):
    S, D = x.shape
    tr = _tile(S, 256)

    def body(dh_ref, x_ref, g_ref, dxin_ref, dx_ref, dxb_ref, dg_ref):
        xv = x_ref[...]
        dhv = dh_ref[...].astype(F32)
        r = lax.rsqrt(jnp.mean(xv * xv, axis=-1, keepdims=True) + EPS)
        yv = xv * r
        dy = dhv * g_ref[...]
        dx = dxin_ref[...] + r * (dy - yv * jnp.mean(dy * yv, axis=-1, keepdims=True))
        dx_ref[...] = dx
        dxb_ref[...] = dx.astype(CD)
        part = jnp.sum(dhv * yv, axis=0, keepdims=True)

        @pl.when(pl.program_id(0) == 0)
        def _():
            dg_ref[...] = part

        @pl.when(pl.program_id(0) > 0)
        def _():
            dg_ref[...] += part

    row = BS((tr, D), lambda i: (i, 0))
    vec = BS((1, D), lambda i: (0, 0))
    return pl.pallas_call(
        body, name=name, grid=(S // tr,),
        out_shape=[jax.ShapeDtypeStruct((S, D), F32), jax.ShapeDtypeStruct((S, D), CD),
                   jax.ShapeDtypeStruct((1, D), F32)],
        in_specs=[row, row, vec, row], out_specs=[row, row, vec], compiler_params=_params("arbitrary"),
    )(dh, x, g, dx_in)


def _loss_head(y, target):
    S, D = y.shape
    tr = _tile(S, 512)

    def body(y_ref, t_ref, dy_ref, dyb_ref, l_ref):
        err = y_ref[...] - t_ref[...]
        dy = err * (1.0 / D)
        dy_ref[...] = dy
        dyb_ref[...] = dy.astype(CD)
        part = jnp.full((8, 128), jnp.sum(err * err), F32)

        @pl.when(pl.program_id(0) == 0)
        def _():
            l_ref[...] = part

        @pl.when(pl.program_id(0) > 0)
        def _():
            l_ref[...] += part

    row = BS((tr, D), lambda i: (i, 0))
    return pl.pallas_call(
        body, name="loss_head", grid=(S // tr,),
        out_shape=[jax.ShapeDtypeStruct((S, D), F32), jax.ShapeDtypeStruct((S, D), CD),
                   jax.ShapeDtypeStruct((8, 128), F32)],
        in_specs=[row, row], out_specs=[row, row, BS((8, 128), lambda i: (0, 0))],
        compiler_params=_params("arbitrary"),
    )(y, target)


def _sigmoid(v):
    return 1.0 / (1.0 + jnp.exp(-v))


def _swiglu_fwd(gu):
    _, J, S, F8 = gu.shape
    tr = _tile(S, 1024)

    def body(gu_ref, o_ref):
        gv = gu_ref[0].astype(F32)
        uv = gu_ref[1].astype(F32)
        o_ref[...] = (gv * _sigmoid(gv) * uv).astype(o_ref.dtype)

    return pl.pallas_call(
        body, name="swiglu_fwd", grid=(J, S // tr), out_shape=jax.ShapeDtypeStruct((J, S, F8), CD),
        in_specs=[BS((2, None, tr, F8), lambda j, i: (0, j, i, 0))],
        out_specs=BS((None, tr, F8), lambda j, i: (j, i, 0)), compiler_params=_params("parallel", "parallel"),
    )(gu)


def _swiglu_bwd(da, gu):
    _, J, S, F8 = gu.shape
    tr = _tile(S, 1024)

    def body(da_ref, gu_ref, o_ref):
        gv = gu_ref[0].astype(F32)
        uv = gu_ref[1].astype(F32)
        dav = da_ref[...].astype(F32)
        sg = _sigmoid(gv)
        o_ref[0] = (dav * uv * sg * (1.0 + gv * (1.0 - sg))).astype(o_ref.dtype)
        o_ref[1] = (dav * gv * sg).astype(o_ref.dtype)

    pair = BS((2, None, tr, F8), lambda j, i: (0, j, i, 0))
    return pl.pallas_call(
        body, name="swiglu_bwd", grid=(J, S // tr), out_shape=jax.ShapeDtypeStruct(gu.shape, CD),
        in_specs=[BS((None, tr, F8), lambda j, i: (j, i, 0)), pair], out_specs=pair,
        compiler_params=_params("parallel", "parallel"),
    )(da, gu)


def _rope_tables(S):
    freqs = ROPE_THETA ** (-jnp.arange(ROPE_FREQS, dtype=F32) / ROPE_FREQS)
    pos = jnp.arange(S, dtype=jnp.int32)
    row = (pos // GRID_W).astype(F32)[:, None] * freqs
    col = (pos % GRID_W).astype(F32)[:, None] * freqs
    cr, sr, cc, sc = jnp.cos(row), jnp.sin(row), jnp.cos(col), jnp.sin(col)
    return jnp.concatenate([cr, cr, cc, cc], axis=1), jnp.concatenate([-sr, sr, -sc, sc], axis=1)


def _swap_halves(v):
    lane = lax.broadcasted_iota(jnp.int32, v.shape, 1)
    return jnp.where((lane & 32) == 0, pltpu.roll(v, 96, 1), pltpu.roll(v, 32, 1))


def _qk_prep_fwd(qkv, qg, kg, cos_t, sin_t, NQ, NKV):
    S, W = qkv.shape
    tr = _tile(S, 256)
    scale = HEAD_DIM ** -0.5

    def body(qkv_ref, qg_ref, kg_ref, c_ref, s_ref, q_out, k_out):
        C, Sg = c_ref[...], s_ref[...]

        def one(xh, gain, sc):
            xh = xh.astype(F32)
            r = lax.rsqrt(jnp.mean(xh * xh, axis=-1, keepdims=True) + EPS)
            nv = xh * r * gain
            return (nv * C + _swap_halves(nv) * Sg) * sc

        for h in range(NQ):
            sl = slice(h * HEAD_DIM, (h + 1) * HEAD_DIM)
            q_out[:, sl] = one(qkv_ref[:, sl], qg_ref[...], scale).astype(CD)
        for h in range(NKV):
            sl = slice((NQ + h) * HEAD_DIM, (NQ + h + 1) * HEAD_DIM)
            k_out[:, h * HEAD_DIM:(h + 1) * HEAD_DIM] = one(qkv_ref[:, sl], kg_ref[...], 1.0).astype(CD)

    vec = BS((1, HEAD_DIM), lambda i: (0, 0))
    tab = BS((tr, HEAD_DIM), lambda i: (i, 0))
    return pl.pallas_call(
        body, name="qk_prep_fwd", grid=(S // tr,),
        out_shape=[jax.ShapeDtypeStruct((S, NQ * HEAD_DIM), CD), jax.ShapeDtypeStruct((S, NKV * HEAD_DIM), CD)],
        in_specs=[BS((tr, W), lambda i: (i, 0)), vec, vec, tab, tab],
        out_specs=[BS((tr, NQ * HEAD_DIM), lambda i: (i, 0)), BS((tr, NKV * HEAD_DIM), lambda i: (i, 0))],
        compiler_params=_params("parallel"),
    )(qkv, qg, kg, cos_t, sin_t)


def _qk_prep_bwd(qkv, dq, dk, dv, qg, kg, cos_t, sin_t, NQ, NKV):
    S, W = qkv.shape
    tr = _tile(S, 256)
    scale = HEAD_DIM ** -0.5

    def body(qkv_ref, dq_ref, dk_ref, dv_ref, qg_ref, kg_ref, c_ref, s_ref, o_ref, dqg_ref, dkg_ref):
        C, Sg = c_ref[...], s_ref[...]

        def one(xh, dout, gain, sc):
            xh = xh.astype(F32)
            r = lax.rsqrt(jnp.mean(xh * xh, axis=-1, keepdims=True) + EPS)
            yv = xh * r
            dout = dout.astype(F32) * sc
            dn = dout * C + _swap_halves(dout * Sg)
            dy = dn * gain
            dx = r * (dy - yv * jnp.mean(dy * yv, axis=-1, keepdims=True))
            return dx, jnp.sum(dn * yv, axis=0, keepdims=True)

        dqg = jnp.zeros((1, HEAD_DIM), F32)
        dkg = jnp.zeros((1, HEAD_DIM), F32)
        for h in range(NQ):
            sl = slice(h * HEAD_DIM, (h + 1) * HEAD_DIM)
            dx, dg = one(qkv_ref[:, sl], dq_ref[:, sl], qg_ref[...], scale)
            o_ref[:, sl] = dx.astype(CD)
            dqg = dqg + dg
        for h in range(NKV):
            sl = slice((NQ + h) * HEAD_DIM, (NQ + h + 1) * HEAD_DIM)
            dx, dg = one(qkv_ref[:, sl], dk_ref[:, h * HEAD_DIM:(h + 1) * HEAD_DIM], kg_ref[...], 1.0)
            o_ref[:, sl] = dx.astype(CD)
            dkg = dkg + dg
        o_ref[:, (NQ + NKV) * HEAD_DIM:] = dv_ref[...].astype(CD)

        @pl.when(pl.program_id(0) == 0)
        def _():
            dqg_ref[...] = dqg
            dkg_ref[...] = dkg

        @pl.when(pl.program_id(0) > 0)
        def _():
            dqg_ref[...] += dqg
            dkg_ref[...] += dkg

    vec = BS((1, HEAD_DIM), lambda i: (0, 0))
    tab = BS((tr, HEAD_DIM), lambda i: (i, 0))
    full = BS((tr, W), lambda i: (i, 0))
    return pl.pallas_call(
        body, name="qk_prep_bwd", grid=(S // tr,),
        out_shape=[jax.ShapeDtypeStruct((S, W), CD), jax.ShapeDtypeStruct((1, HEAD_DIM), F32),
                   jax.ShapeDtypeStruct((1, HEAD_DIM), F32)],
        in_specs=[full, BS((tr, NQ * HEAD_DIM), lambda i: (i, 0)), BS((tr, NKV * HEAD_DIM), lambda i: (i, 0)),
                  BS((tr, NKV * HEAD_DIM), lambda i: (i, 0)), vec, vec, tab, tab],
        out_specs=[full, vec, vec], compiler_params=_params("arbitrary"),
    )(qkv, dq, dk, dv, qg, kg, cos_t, sin_t)


def _lane_pack(cols):
    rows = cols[0].shape[0]
    lane = lax.broadcasted_iota(jnp.int32, (rows, HEAD_DIM), 1)
    out = jnp.zeros((rows, HEAD_DIM), F32)
    for g, col in enumerate(cols):
        out = jnp.where(lane == g, col, out)
    return out


def _flash_fwd(qr, kr, qkv, NQ, NKV):
    S = qr.shape[0]
    G = NQ // NKV
    tq, tk = _tile(S, FWD_TQ), _tile(S, FWD_TK)
    nk = S // tk
    GW = G * HEAD_DIM

    def body(q_ref, k_ref, v_ref, o_ref, lse_ref, m_s, acc_s):
        ki = pl.program_id(2)

        @pl.when(ki == 0)
        def _():
            m_s[...] = jnp.full(m_s.shape, -1e30, F32)
            acc_s[...] = jnp.zeros(acc_s.shape, F32)

        kv = k_ref[...]
        v_ones = jnp.concatenate([v_ref[...], jnp.ones((tk, HEAD_DIM), CD)], axis=1)
        for g in range(G):
            s = lax.dot_general(q_ref[:, g * HEAD_DIM:(g + 1) * HEAD_DIM], kv, NT, preferred_element_type=F32)
            m_prev = m_s[g]
            m_new = jnp.maximum(m_prev, jnp.max(s, axis=-1, keepdims=True))
            alpha = jnp.exp(m_prev - m_new)
            p = jnp.exp(s - jnp.tile(m_new, (1, tk // HEAD_DIM)))
            acc_s[g] = jnp.tile(alpha, (1, 2)) * acc_s[g] + jnp.dot(p.astype(CD), v_ones, preferred_element_type=F32)
            m_s[g] = m_new

        @pl.when(ki == nk - 1)
        def _():
            lane = lax.broadcasted_iota(jnp.int32, (tq, HEAD_DIM), 1)
            lse = jnp.zeros((tq, HEAD_DIM), F32)
            for g in range(G):
                l = acc_s[g, :, HEAD_DIM:]
                o_ref[:, g * HEAD_DIM:(g + 1) * HEAD_DIM] = (acc_s[g, :, :HEAD_DIM] / l).astype(o_ref.dtype)
                lse = jnp.where(lane == g, m_s[g] + jnp.log(l), lse)
            lse_ref[...] = lse

    return pl.pallas_call(
        body, name="flash_fwd", grid=(NKV, S // tq, nk),
        out_shape=[jax.ShapeDtypeStruct((S, NQ * HEAD_DIM), CD), jax.ShapeDtypeStruct((NKV, S, HEAD_DIM), F32)],
        in_specs=[BS((tq, GW), lambda h, i, k: (i, h)), BS((tk, HEAD_DIM), lambda h, i, k: (k, h)),
                  BS((tk, HEAD_DIM), lambda h, i, k: (k, NQ + NKV + h))],
        out_specs=[BS((tq, GW), lambda h, i, k: (i, h)), BS((None, tq, HEAD_DIM), lambda h, i, k: (h, i, 0))],
        scratch_shapes=[pltpu.VMEM((G, tq, HEAD_DIM), F32), pltpu.VMEM((G, tq, 2 * HEAD_DIM), F32)],
        compiler_params=_params("parallel", "parallel", "arbitrary"),
    )(qr, kr, qkv)


def _attn_delta(do, o, NQ, NKV):
    S = o.shape[0]
    G = NQ // NKV
    tr = _tile(S, 512)

    def body(do_ref, o_ref, d_ref):
        for kv in range(NKV):
            cols = []
            for g in range(G):
                sl = slice((kv * G + g) * HEAD_DIM, (kv * G + g + 1) * HEAD_DIM)
                cols.append(jnp.sum(do_ref[:, sl].astype(F32) * o_ref[:, sl].astype(F32), axis=-1, keepdims=True))
            d_ref[kv] = _lane_pack(cols)

    row = BS((tr, NQ * HEAD_DIM), lambda i: (i, 0))
    return pl.pallas_call(
        body, name="attn_delta", grid=(S // tr,), out_shape=jax.ShapeDtypeStruct((NKV, S, HEAD_DIM), F32),
        in_specs=[row, row], out_specs=BS((NKV, tr, HEAD_DIM), lambda i: (0, i, 0)),
        compiler_params=_params("parallel"),
    )(do, o)


def _flash_bwd(qr, kr, qkv, do, lse_rows, delta_rows, NQ, NKV):
    S = qr.shape[0]
    G = NQ // NKV
    tq, tk = _tile(S, BWD_TQ), _tile(S, BWD_TK)
    nq, nk = S // tq, S // tk
    GW = G * HEAD_DIM

    def body(q_ref, k_ref, v_ref, do_ref, lse_ref, dl_ref, dq_ref, dk_ref, dv_ref, dq_s, dk_s, dv_s):
        ki, qi = pl.program_id(1), pl.program_id(2)
        rows = pl.ds(pl.multiple_of(qi * tq, tq), tq)

        @pl.when(qi == 0)
        def _():
            dk_s[...] = jnp.zeros(dk_s.shape, F32)
            dv_s[...] = jnp.zeros(dv_s.shape, F32)

        kv, vv = k_ref[...], v_ref[...]
        for g in range(G):
            sl = slice(g * HEAD_DIM, (g + 1) * HEAD_DIM)
            qg, dog = q_ref[:, sl], do_ref[:, sl]
            st = lax.dot_general(kv, qg, NT, preferred_element_type=F32)
            pt = jnp.exp(st - lse_ref[g:g + 1, :])
            dv_s[...] += jnp.dot(pt.astype(CD), dog, preferred_element_type=F32)
            dpt = lax.dot_general(vv, dog, NT, preferred_element_type=F32)
            dst = (pt * (dpt - dl_ref[g:g + 1, :])).astype(CD)
            dk_s[...] += jnp.dot(dst, qg, preferred_element_type=F32)
            dq_part = lax.dot_general(dst, kv, TN, preferred_element_type=F32)

            @pl.when(ki == 0)
            def _():
                dq_s[g, rows, :] = dq_part

            @pl.when(ki > 0)
            def _():
                dq_s[g, rows, :] += dq_part

        @pl.when(qi == nq - 1)
        def _():
            dk_ref[...] = dk_s[...].astype(dk_ref.dtype)
            dv_ref[...] = dv_s[...].astype(dv_ref.dtype)

        @pl.when(ki == nk - 1)
        def _():
            for g in range(G):
                dq_ref[rows, g * HEAD_DIM:(g + 1) * HEAD_DIM] = dq_s[g, rows, :].astype(dq_ref.dtype)

    qb = BS((tq, GW), lambda h, k, i: (i, h))
    st = BS((None, G, tq), lambda h, k, i: (h, 0, i))
    kb = BS((tk, HEAD_DIM), lambda h, k, i: (k, h))
    return pl.pallas_call(
        body, name="flash_bwd", grid=(NKV, nk, nq),
        out_shape=[jax.ShapeDtypeStruct((S, NQ * HEAD_DIM), CD), jax.ShapeDtypeStruct((S, NKV * HEAD_DIM), CD),
                   jax.ShapeDtypeStruct((S, NKV * HEAD_DIM), CD)],
        in_specs=[qb, kb, BS((tk, HEAD_DIM), lambda h, k, i: (k, NQ + NKV + h)), qb, st, st],
        out_specs=[BS((S, GW), lambda h, k, i: (0, h)), kb, kb],
        scratch_shapes=[pltpu.VMEM((G, S, HEAD_DIM), F32), pltpu.VMEM((tk, HEAD_DIM), F32), pltpu.VMEM((tk, HEAD_DIM), F32)],
        compiler_params=_params("parallel", "arbitrary", "arbitrary"),
    )(qr, kr, qkv, do, lse_rows, delta_rows)


HALO8 = 8


def _round(v):
    return v.astype(CD).astype(F32)


def _shift_down(v, k, above):
    T = v.shape[0]
    row = lax.broadcasted_iota(jnp.int32, v.shape, 0)
    out = pltpu.roll(v, k, 0)
    for j in range(k):
        out = jnp.where(row == j, above[above.shape[0] - k + j:above.shape[0] - k + j + 1, :], out)
    return out


def _shift_up(v, k, below):
    T = v.shape[0]
    row = lax.broadcasted_iota(jnp.int32, v.shape, 0)
    out = pltpu.roll(v, T - k, 0)
    for j in range(k):
        out = jnp.where(row == T - k + j, below[j:j + 1, :], out)
    return out


def _halo_specs(T, C, halo, n_chunks, S, col=0, rev=False):
    per = T // halo
    last = S // halo - 1
    ch = (lambda i: n_chunks - 1 - i) if rev else (lambda i: i)
    return (BS((T, C), lambda i: (ch(i), col)),
            BS((halo, C), lambda i: (jnp.maximum(ch(i) * per - 1, 0), col)),
            BS((halo, C), lambda i: (jnp.minimum((ch(i) + 1) * per, last), col)))


def _conv_fwd(xy, w8, bias):
    S, C2 = xy.shape
    C = C2 // 2
    T = _tile(S, 256)
    n = S // T

    def body(x_ref, p_ref, q_ref, w_ref, b_ref, o_ref):
        i = pl.program_id(0)
        xv = _round(x_ref[...])
        above = jnp.where(i > 0, _round(p_ref[...]), 0.0)
        below = jnp.where(i < n - 1, _round(q_ref[...]), 0.0)
        wv = _round(w_ref[...])
        o_ref[...] = (wv[0:1, :] * _shift_down(xv, 2, above) + wv[1:2, :] * _shift_down(xv, 1, above)
                      + wv[2:3, :] * xv + wv[3:4, :] * _shift_up(xv, 1, below) + b_ref[...])

    main, prev, nxt = _halo_specs(T, C, HALO8, n, S)
    return pl.pallas_call(
        body, name="conv_fwd", grid=(n,), out_shape=jax.ShapeDtypeStruct((S, C), F32),
        in_specs=[main, prev, nxt, BS((8, C), lambda i: (0, 0)), BS((1, C), lambda i: (0, 0))],
        out_specs=BS((T, C), lambda i: (i, 0)), compiler_params=_params("parallel"),
    )(xy, xy, xy, w8, bias)


def _tile_scan(a, b, reverse):
    row = lax.broadcasted_iota(jnp.int32, a.shape, 0)
    for sh in (1, 2, 4):
        if reverse:
            a_s, b_s, ok = pltpu.roll(a, 8 - sh, 0), pltpu.roll(b, 8 - sh, 0), row < 8 - sh
        else:
            a_s, b_s, ok = pltpu.roll(a, sh, 0), pltpu.roll(b, sh, 0), row >= sh
        b = jnp.where(ok, a * b_s + b, b)
        a = jnp.where(ok, a * a_s, a)
    return a, b


def _scan_chunk(a_ref, h_ref, carry_ref, reverse):
    T = h_ref.shape[0]
    nt = T // 8

    def step(j, carry):
        t = (nt - 1 - j) if reverse else j
        rows = pl.ds(pl.multiple_of(t * 8, 8), 8)
        ca, hb = _tile_scan(a_ref[rows, :], h_ref[rows, :], reverse)
        hv = hb + ca * carry
        h_ref[rows, :] = hv
        edge = hv[0:1, :] if reverse else hv[7:8, :]
        return jnp.broadcast_to(edge, hv.shape)

    carry_ref[...] = lax.fori_loop(0, nt, step, carry_ref[...])


def _softplus(z):
    return jnp.maximum(z, 0.0) + jnp.log(1.0 + jnp.exp(-jnp.abs(z)))


def _neg_expm1(z):
    return jnp.where(z > -0.01, -z * (1.0 + z * (0.5 + z * (1.0 / 6.0 + z * (1.0 / 24.0)))), 1.0 - jnp.exp(z))


def _gates(xs, wa, wi, ba, bi, sp):
    r = _sigmoid(jnp.dot(xs, wa, preferred_element_type=F32) + ba)
    ig = _sigmoid(jnp.dot(xs, wi, preferred_element_type=F32) + bi)
    la = -LRU_C * r * sp
    return r, ig, jnp.exp(la), jnp.sqrt(_neg_expm1(2.0 * la))


def _lru_fwd(name, xc, wai, d, ba, bi, lam, reverse):
    S, C = xc.shape
    NB, BW = wai.shape[2], wai.shape[3]
    T = _tile(S, 256)
    n = S // T
    ch = (lambda i: n - 1 - i) if reverse else (lambda i: i)

    def body(x_ref, wa_ref, wi_ref, ba_ref, bi_ref, lam_ref, h_ref, a_ref, carry):
        @pl.when(pl.program_id(0) == 0)
        def _():
            carry[...] = jnp.zeros(carry.shape, F32)

        for b in range(NB):
            sl = slice(b * BW, (b + 1) * BW)
            xf = x_ref[:, sl]
            _, ig, av, sq = _gates(xf.astype(CD), wa_ref[b], wi_ref[b], ba_ref[:, sl], bi_ref[:, sl],
                                   _softplus(-lam_ref[:, sl]))
            a_ref[:, sl] = av
            h_ref[:, sl] = sq * ig * xf
        _scan_chunk(a_ref, h_ref, carry, reverse)

    vec = BS((1, C), lambda i: (0, 0))
    blk = BS((T, C), lambda i: (ch(i), 0))
    return pl.pallas_call(
        body, name=name, grid=(n,),
        out_shape=[jax.ShapeDtypeStruct((S, C), F32), jax.ShapeDtypeStruct((S, C), F32)],
        in_specs=[blk, BS((None, None, NB, BW, BW), lambda i: (0, d, 0, 0, 0)),
                  BS((None, None, NB, BW, BW), lambda i: (1, d, 0, 0, 0)), vec, vec, vec],
        out_specs=[blk, blk], scratch_shapes=[pltpu.VMEM((8, C), F32)], compiler_params=_params("arbitrary"),
    )(xc, wai, wai, ba, bi, lam)


_GELU_C = math.sqrt(2.0 / math.pi)


def _rnn_out_fwd(h0, h1, xy):
    S, C = h0.shape
    tr = _tile(S, 256)

    def body(h0_ref, h1_ref, y_ref, o_ref):
        yv = y_ref[...].astype(F32)
        gl = 0.5 * yv * (1.0 + jnp.tanh(_GELU_C * (yv + 0.044715 * yv * yv * yv)))
        o_ref[...] = ((h0_ref[...] + h1_ref[...]) * gl).astype(o_ref.dtype)

    row = BS((tr, C), lambda i: (i, 0))
    return pl.pallas_call(
        body, name="rnn_out_fwd", grid=(S // tr,), out_shape=jax.ShapeDtypeStruct((S, C), CD),
        in_specs=[row, row, BS((tr, C), lambda i: (i, 1))], out_specs=row, compiler_params=_params("parallel"),
    )(h0, h1, xy)


def _rnn_out_bwd(dout, h0, h1, xy):
    S, C = h0.shape
    tr = _tile(S, 256)

    def body(d_ref, h0_ref, h1_ref, y_ref, dh_ref, dy_ref):
        yv = y_ref[...].astype(F32)
        dv = d_ref[...].astype(F32)
        t = jnp.tanh(_GELU_C * (yv + 0.044715 * yv * yv * yv))
        gl = 0.5 * yv * (1.0 + t)
        dgl = 0.5 * (1.0 + t) + 0.5 * yv * (1.0 - t * t) * _GELU_C * (1.0 + 3.0 * 0.044715 * yv * yv)
        dh_ref[...] = dv * gl
        dy_ref[...] = (dv * (h0_ref[...] + h1_ref[...]) * dgl).astype(dy_ref.dtype)

    row = BS((tr, C), lambda i: (i, 0))
    return pl.pallas_call(
        body, name="rnn_out_bwd", grid=(S // tr,),
        out_shape=[jax.ShapeDtypeStruct((S, C), F32), jax.ShapeDtypeStruct((S, C), CD)],
        in_specs=[row, row, row, BS((tr, C), lambda i: (i, 1))], out_specs=[row, row],
        compiler_params=_params("parallel"),
    )(dout, h0, h1, xy)


def _lru_bwd(name, dhs, a, h, xc, wai, d, ba, bi, lam, reverse, dxc_in):
    S, C = xc.shape
    NB, BW = wai.shape[2], wai.shape[3]
    T = _tile(S, 256)
    n = S // T
    back = not reverse

    def body(*refs):
        (dh_ref, a_ref, a_prev, a_next, h_ref, h_prev, h_next, x_ref, wa_ref, wi_ref, ba_ref, bi_ref,
         lam_ref) = refs[:13]
        k = 13
        add_ref = None
        if dxc_in is not None:
            add_ref = refs[k]
            k += 1
        dx_ref, dwa_ref, dwi_ref, dba_ref, dbi_ref, dlam_ref, coef_s, lam_s, carry, dsp_s = refs[k:]
        i = pl.program_id(0)
        first_chunk = i == (n - 1 if back else 0)
        last_chunk = i == (0 if back else n - 1)

        @pl.when(i == 0)
        def _():
            carry[...] = jnp.zeros(carry.shape, F32)
            dsp_s[...] = jnp.zeros(dsp_s.shape, F32)
            dwa_ref[...] = jnp.zeros(dwa_ref.shape, F32)
            dwi_ref[...] = jnp.zeros(dwi_ref.shape, F32)
            dba_ref[...] = jnp.zeros(dba_ref.shape, F32)
            dbi_ref[...] = jnp.zeros(dbi_ref.shape, F32)

        av, hv = a_ref[...], h_ref[...]
        if reverse:
            coef_s[...] = _shift_down(av, 1, a_prev[...])
            h_nb = _shift_up(hv, 1, jnp.where(last_chunk, 0.0, h_next[...]))
        else:
            coef_s[...] = _shift_up(av, 1, a_next[...])
            h_nb = _shift_down(hv, 1, jnp.where(first_chunk, 0.0, h_prev[...]))
        lam_s[...] = dh_ref[...]
        _scan_chunk(coef_s, lam_s, carry, back)

        for b in range(NB):
            sl = slice(b * BW, (b + 1) * BW)
            xf = x_ref[:, sl]
            xs = xf.astype(CD)
            sp = _softplus(-lam_ref[:, sl])
            r, ig, ab, sq = _gates(xs, wa_ref[b], wi_ref[b], ba_ref[:, sl], bi_ref[:, sl], sp)
            du = lam_s[:, sl]
            da = du * h_nb[:, sl]
            d_ig = du * sq * xf
            d_sq = du * ig * xf
            d_la = da * ab - d_sq * (ab * ab) / sq
            d_pa = d_la * (-LRU_C * sp) * r * (1.0 - r)
            d_pi = d_ig * ig * (1.0 - ig)
            dsp_s[:, sl] += jnp.sum(d_la * (-LRU_C * r), axis=0, keepdims=True)
            dba_ref[:, sl] += jnp.sum(d_pa, axis=0, keepdims=True)
            dbi_ref[:, sl] += jnp.sum(d_pi, axis=0, keepdims=True)
            d_pa_c, d_pi_c = d_pa.astype(CD), d_pi.astype(CD)
            dwa_ref[b] += lax.dot_general(xs, d_pa_c, TN, preferred_element_type=F32)
            dwi_ref[b] += lax.dot_general(xs, d_pi_c, TN, preferred_element_type=F32)
            dx = (du * sq * ig + lax.dot_general(d_pa_c, wa_ref[b], NT, preferred_element_type=F32)
                  + lax.dot_general(d_pi_c, wi_ref[b], NT, preferred_element_type=F32))
            if add_ref is not None:
                dx = dx + add_ref[:, sl]
            dx_ref[:, sl] = dx

        @pl.when(i == n - 1)
        def _():
            dlam_ref[...] = -dsp_s[...] * _sigmoid(-lam_ref[...])

    vec = BS((1, C), lambda i: (0, 0))
    main, prev, nxt = _halo_specs(T, C, HALO8, n, S, rev=back)
    wspec = lambda which: BS((None, None, NB, BW, BW), lambda i: (which, d, 0, 0, 0))
    acc = BS((NB, BW, BW), lambda i: (0, 0, 0))
    ins = [dhs, a, a, a, h, h, h, xc, wai, wai, ba, bi, lam]
    in_specs = [main, main, prev, nxt, main, prev, nxt, main, wspec(0), wspec(1), vec, vec, vec]
    if dxc_in is not None:
        ins.append(dxc_in)
        in_specs.append(main)
    return pl.pallas_call(
        body, name=name, grid=(n,),
        out_shape=[jax.ShapeDtypeStruct((S, C), F32), jax.ShapeDtypeStruct((NB, BW, BW), F32),
                   jax.ShapeDtypeStruct((NB, BW, BW), F32), jax.ShapeDtypeStruct((1, C), F32),
                   jax.ShapeDtypeStruct((1, C), F32), jax.ShapeDtypeStruct((1, C), F32)],
        in_specs=in_specs, out_specs=[main, acc, acc, vec, vec, vec],
        scratch_shapes=[pltpu.VMEM((T, C), F32), pltpu.VMEM((T, C), F32), pltpu.VMEM((8, C), F32),
                        pltpu.VMEM((1, C), F32)],
        compiler_params=_params("arbitrary"),
    )(*ins)


def _conv_bwd(dxc, xy, w8):
    S, C = dxc.shape
    T = _tile(S, 256)
    n = S // T

    def body(d_ref, dp_ref, dn_ref, x_ref, xp_ref, xn_ref, w_ref, dx_ref, dw_ref):
        i = pl.program_id(0)
        dv = _round(d_ref[...])
        d_above = jnp.where(i > 0, _round(dp_ref[...]), 0.0)
        d_below = jnp.where(i < n - 1, _round(dn_ref[...]), 0.0)
        wv = _round(w_ref[...])
        dx = (wv[0:1, :] * _shift_up(dv, 2, d_below) + wv[1:2, :] * _shift_up(dv, 1, d_below)
              + wv[2:3, :] * dv + wv[3:4, :] * _shift_down(dv, 1, d_above))
        dx_ref[...] = dx.astype(dx_ref.dtype)
        xv = _round(x_ref[...])
        above = jnp.where(i > 0, _round(xp_ref[...]), 0.0)
        below = jnp.where(i < n - 1, _round(xn_ref[...]), 0.0)
        rows = [jnp.sum(dv * _shift_down(xv, 2, above), axis=0, keepdims=True),
                jnp.sum(dv * _shift_down(xv, 1, above), axis=0, keepdims=True),
                jnp.sum(dv * xv, axis=0, keepdims=True),
                jnp.sum(dv * _shift_up(xv, 1, below), axis=0, keepdims=True),
                jnp.sum(dv, axis=0, keepdims=True)]
        rid = lax.broadcasted_iota(jnp.int32, (8, C), 0)
        part = jnp.zeros((8, C), F32)
        for j, rv in enumerate(rows):
            part = jnp.where(rid == j, rv, part)

        @pl.when(i == 0)
        def _():
            dw_ref[...] = part

        @pl.when(i > 0)
        def _():
            dw_ref[...] += part

    dmain, dprev, dnext = _halo_specs(T, C, HALO8, n, S)
    xmain, xprev, xnext = _halo_specs(T, C, HALO8, n, S)
    w = BS((8, C), lambda i: (0, 0))
    return pl.pallas_call(
        body, name="conv_bwd", grid=(n,),
        out_shape=[jax.ShapeDtypeStruct((S, C), CD), jax.ShapeDtypeStruct((8, C), F32)],
        in_specs=[dmain, dprev, dnext, xmain, xprev, xnext, w], out_specs=[dmain, w],
        compiler_params=_params("arbitrary"),
    )(dxc, dxc, dxc, xy, xy, xy, w8)


def _adamw(name, lands, w, m, v, pick=None):
    c1 = 1.0 - ADAM_B1 ** ADAM_STEP
    c2 = 1.0 - ADAM_B2 ** ADAM_STEP
    L, C = w.shape[0], w.shape[-1]
    w3, m3, v3 = (t.reshape(L, -1, C) for t in (w, m, v))
    R = w3.shape[1]
    tr = R
    for cand in (1024, 512, 256, 128, 64, 32, 16, 8):
        if R % cand == 0 and cand * C * 4 <= (1 << 20):
            tr = cand
            break

    def body(*refs):
        l_refs = refs[:L]
        w_ref, m_ref, v_ref, g_out, d_out, m_out, v_out = refs[L:]

        def update(l_ref):
            g = l_ref[0].astype(F32)
            for s in range(1, NDEV):
                g = g + l_ref[s].astype(F32)
            mn = ADAM_B1 * m_ref[...] + (1.0 - ADAM_B1) * g
            vn = ADAM_B2 * v_ref[...] + (1.0 - ADAM_B2) * (g * g)
            g_out[...] = g
            m_out[...] = mn
            v_out[...] = vn
            d_out[...] = -ADAM_LR * ((mn / c1) / (jnp.sqrt(vn / c2) + ADAM_EPS) + ADAM_WD * w_ref[...])

        for k in range(L):
            pl.when(pl.program_id(0) == k)(functools.partial(update, l_refs[k]))

    if pick is None:
        lands = [t.reshape(NDEV, R, C) for t in lands]
        land_specs = [BS((NDEV, tr, C), lambda l, i, k=k: (0, jnp.where(l == k, i, 0), 0)) for k in range(L)]
    else:
        lands = [t.reshape(NDEV, -1, R, C) for t in lands]
        land_specs = [BS((NDEV, None, tr, C), lambda l, i, k=k: (0, pick, jnp.where(l == k, i, 0), 0)) for k in range(L)]
    spec = BS((None, tr, C), lambda l, i: (l, i, 0))
    sds = jax.ShapeDtypeStruct(w3.shape, F32)
    outs = pl.pallas_call(
        body, name=name, grid=(L, R // tr), out_shape=[sds] * 4, in_specs=land_specs + [spec, spec, spec],
        out_specs=[spec] * 4, compiler_params=_params("arbitrary", "arbitrary"),
    )(*lands, w3, m3, v3)
    return [o.reshape(w.shape) for o in outs]


def kernel(x, norm_mix, norm_ffn, attn_w_qkv, attn_q_gain, attn_k_gain, attn_w_o, rnn_w_in, rnn_conv_w, rnn_conv_b, rnn_w_a, rnn_b_a, rnn_w_i, rnn_b_i, rnn_lambda, rnn_w_out, ffn_w_gate, ffn_w_up, ffn_w_down, loss_target, m_norm_mix, m_norm_ffn, m_attn_w_qkv, m_attn_q_gain, m_attn_k_gain, m_attn_w_o, m_rnn_w_in, m_rnn_conv_w, m_rnn_conv_b, m_rnn_w_a, m_rnn_b_a, m_rnn_w_i, m_rnn_b_i, m_rnn_lambda, m_rnn_w_out, m_ffn_w_gate, m_ffn_w_up, m_ffn_w_down, v_norm_mix, v_norm_ffn, v_attn_w_qkv, v_attn_q_gain, v_attn_k_gain, v_attn_w_o, v_rnn_w_in, v_rnn_conv_w, v_rnn_conv_b, v_rnn_w_a, v_rnn_b_a, v_rnn_w_i, v_rnn_b_i, v_rnn_lambda, v_rnn_w_out, v_ffn_w_gate, v_ffn_w_up, v_ffn_w_down):
    S, D = x.shape[1], x.shape[2]
    depth = norm_mix.shape[0]
    n_attn, n_rnn = attn_w_qkv.shape[0], rnn_w_in.shape[0]
    QS = attn_w_qkv.shape[2]
    NQ = D // HEAD_DIM
    NKV = (QS * NDEV // HEAD_DIM - NQ) // 2
    G = NQ // NKV
    OS = attn_w_o.shape[1]
    IS = rnn_w_in.shape[2]
    C = rnn_w_out.shape[1] * NDEV
    CS = C // NDEV
    NB, BWS, BW = rnn_w_a.shape[2], rnn_w_a.shape[3], rnn_w_a.shape[4]
    F8 = ffn_w_gate.shape[2]
    tm = _tile(S, 1024)
    tn = _tile(D, 1024)
    tk = _tile(S, 2048)

    def rows_at(parts, total):
        out = None
        for at, p in parts:
            cfg = [(0, 0)] * (p.ndim - 2) + [(at, total - at - p.shape[-2]), (0, 0)]
            out = jnp.pad(p, cfg) if out is None else out + jnp.pad(p, cfg)
        return out

    def pack_small(cw, cb, b_a, b_i, lam):
        return rows_at([(0, cw), (8, cb[:, None, :]), (16, b_a), (24, b_i), (32, lam)], SMALL_ROWS)

    def unpack_small(p):
        return p[:, 0:4], p[:, 8], p[:, 16:18], p[:, 24:26], p[:, 32:34]

    gu_sh = jnp.stack([ffn_w_gate, ffn_w_up], axis=1).astype(CD)
    ai_sh = jnp.stack([rnn_w_a, rnn_w_i], axis=1).astype(CD)
    small_sh = pack_small(rnn_conv_w, rnn_conv_b, rnn_b_a, rnn_b_i, rnn_lambda)
    Wqkv0, Wo0 = _exchange("gather_first", [attn_w_qkv[0:1].astype(CD), attn_w_o[0:1].astype(CD)], scatter=False)
    ffn0_handle, ffn0_token = _exchange_start("gather_ffn0_start", [gu_sh[0:1], ffn_w_down[0:1].astype(CD)], scatter=False,
                                              after=Wqkv0)
    gathers, newest = {}, ffn0_token
    for i in range(1, depth):
        l = i // 2
        if i % 2 == 0:
            arrs = [attn_w_qkv[l:l + 1].astype(CD), attn_w_o[l:l + 1].astype(CD)]
        else:
            arrs = [rnn_w_in[l:l + 1].astype(CD), rnn_w_out[l:l + 1].astype(CD), ai_sh[l:l + 1]] + ([small_sh] if l == 0 else [])
        gathers[i], newest = _exchange_start(f"gather_layer{i}_start", arrs + [gu_sh[i:i + 1], ffn_w_down[i:i + 1].astype(CD)],
                                             scatter=False, after=newest)
    started = newest[0, 0]
    wqkv, wo = {0: (Wqkv0, 0)}, {0: (Wo0, 0)}
    win, wout, wai, wgu, wd = {}, {}, {}, {}, {}
    cos_t, sin_t = _rope_tables(S)

    def proj_cols(name, a, W, l, width, dtype):
        return _mm(name, NN, a, W, BS((tm, D), lambda i, j, r: (i, 0)),
                   BS((None, None, D, width), lambda i, j, r: (j, l, 0, 0)),
                   BS((tm, width), lambda i, j, r: (i, j)), (S, NDEV * width), dtype, (S // tm, NDEV, 1))

    def proj_cols_dx(name, dy, W, l, width):
        ns = 4
        return _mm(name, NT, dy, W, BS((tm, ns * width), lambda i, j, r: (i, r)),
                   BS((ns, None, tn, width), lambda i, j, r: (r, l, j, 0)),
                   BS((tm, tn), lambda i, j, r: (i, j)), (S, D), F32, (S // tm, D // tn, NDEV // ns),
                   pairs=lambda a_ref, b_ref: [(a_ref[:, s * width:(s + 1) * width], b_ref[s]) for s in range(ns)])

    def proj_cols_dw(name, a, dy, width):
        return _mm(name, TN, a, dy, BS((tk, tn), lambda i, j, r: (r, i)),
                   BS((tk, width), lambda i, j, r: (r, j)),
                   BS((None, tn, width), lambda i, j, r: (j, i, 0)), (NDEV, D, width), CD, (D // tn, NDEV, S // tk))

    def proj_rows(name, a, W, l, rows, resid):
        return _mm(name, NN, a, W, BS((tm, NDEV * rows), lambda i, j, r: (i, 0)),
                   BS((NDEV, None, rows, tn), lambda i, j, r: (0, l, 0, j)),
                   BS((tm, tn), lambda i, j, r: (i, j)), (S, D), F32, (S // tm, D // tn, 1), add=resid,
                   pairs=lambda a_ref, b_ref: [(a_ref[...], b_ref[...].reshape(NDEV * rows, tn))])

    def proj_rows_dx(name, dy, W, l, rows, dtype):
        return _mm(name, NT, dy, W, BS((tm, D), lambda i, j, r: (i, 0)),
                   BS((None, None, rows, D), lambda i, j, r: (j, l, 0, 0)),
                   BS((tm, rows), lambda i, j, r: (i, j)), (S, NDEV * rows), dtype, (S // tm, NDEV, 1))

    def proj_rows_dw(name, a, dy, rows):
        width = NDEV * rows
        tw = _tile(width, 1024)
        out = _mm(name, TN, a, dy, BS((tk, tw), lambda i, j, r: (r, i)), BS((tk, tn), lambda i, j, r: (r, j)),
                  BS((tw, tn), lambda i, j, r: (i, j)), (width, D), CD, (width // tw, D // tn, S // tk))
        return out.reshape(NDEV, rows, D)

    saved = []
    xs = x[0]
    for i in range(depth):
        l = i // 2
        h1 = _rms_fwd("norm_mix_fwd", xs, norm_mix[i:i + 1] + started if i == 0 else norm_mix[i:i + 1])
        if i % 2 == 0:
            qkv = proj_cols("attn_qkv", h1, *wqkv[l], QS, CD)
            qr, kr = _qk_prep_fwd(qkv, attn_q_gain[l:l + 1], attn_k_gain[l:l + 1], cos_t, sin_t, NQ, NKV)
            o, lse = _flash_fwd(qr, kr, qkv, NQ, NKV)
            x1 = proj_rows("attn_out", o, *wo[l], OS, xs)
            mix = (qkv, qr, kr, o, lse)
        else:
            xy = proj_cols("rnn_in", h1, *win[l], IS, F32)
            w8 = jnp.pad(conv_w_f[l], ((0, 4), (0, 0)))
            xc = _conv_fwd(xy, w8, conv_b_f[l][None])
            hs, avs = [], []
            for d in range(2):
                hd, ad = _lru_fwd("lru_fwd_rev" if d else "lru_fwd", xc, wai[l], d, b_a_f[l, d][None], b_i_f[l, d][None],
                                  lam_f[l, d][None], bool(d))
                hs.append(hd)
                avs.append(ad)
            ro = _rnn_out_fwd(hs[0], hs[1], xy)
            x1 = proj_rows("rnn_out", ro, *wout[l], CS, xs)
            mix = (xy, w8, xc, hs, avs, ro)
        if i == 0:
            Wgu0, Wd0 = _exchange_wait("gather_ffn0_wait", ffn0_handle, x1)
            wgu[0], wd[0] = (Wgu0, 0), (Wd0, 0)
        h2 = _rms_fwd("norm_ffn_fwd", x1, norm_ffn[i:i + 1])
        Wgu, li = wgu[i]
        gu = _mm("ffn_gate_up", NN, h2, Wgu, BS((tm, D), lambda i_, j, r: (i_, 0)),
                 BS((None, None, None, D, F8), lambda i_, j, r, li=li: (j % NDEV, li, j // NDEV, 0, 0)),
                 BS((None, None, tm, F8), lambda i_, j, r: (j // NDEV, j % NDEV, i_, 0)),
                 (2, NDEV, S, F8), CD, (S // tm, 2 * NDEV, 1))
        act = _swiglu_fwd(gu)
        Wd, li = wd[i]
        x2 = _mm("ffn_down", NN, act, Wd, BS((2, tm, F8), lambda i_, j, r: (r, i_, 0)),
                 BS((2, None, F8, tn), lambda i_, j, r, li=li: (r, li, 0, j)),
                 BS((tm, tn), lambda i_, j, r: (i_, j)), (S, D), F32, (S // tm, D // tn, NDEV // 2), add=x1,
                 pairs=lambda a_ref, b_ref: [(a_ref[s], b_ref[s]) for s in range(2)])
        saved.append((xs, h1, mix, x1, h2, gu, act))
        xs = x2
        if i + 1 < depth:
            k, lk = i + 1, (i + 1) // 2
            got = _exchange_wait(f"gather_layer{k}_wait", gathers[k], x2)
            wgu[k], wd[k] = (got[-2], 0), (got[-1], 0)
            if k % 2 == 0:
                wqkv[lk], wo[lk] = (got[0], 0), (got[1], 0)
            else:
                win[lk], wout[lk] = (got[0], 0), (got[1], 0)
                wai[lk] = jnp.transpose(got[2], (1, 2, 3, 4, 0, 5, 6)).reshape(2, 2, NB, BW, BW)
                if lk == 0:
                    small = jnp.transpose(got[3], (1, 2, 0, 3)).reshape(n_rnn, SMALL_ROWS, C)
                    conv_w_f, conv_b_f, b_a_f, b_i_f, lam_f = unpack_small(small)

    dx, dxb, sq_err = _loss_head(xs, loss_target[0])
    loss = lax.psum(0.5 * sq_err[0, 0] / D, ("x", "y", "c"))

    g_norm_mix, g_norm_ffn = [None] * depth, [None] * depth
    g_qkv, g_wo, g_qg, g_kg = [None] * n_attn, [None] * n_attn, [None] * n_attn, [None] * n_attn
    g_win, g_wout, g_ai, g_small = [None] * n_rnn, [None] * n_rnn, [None] * n_rnn, [None] * n_rnn
    g_gu, g_down = [None] * depth, [None] * depth
    scat, sent = {}, None
    for i in reversed(range(depth)):
        l = i // 2
        xs, h1, mix, x1, h2, gu, act = saved[i]
        gain_ffn = norm_ffn[i:i + 1] if sent is None else norm_ffn[i:i + 1] + sent[0, 0]
        Wd, li = wd[i]
        dact = _mm("ffn_down_dx", NT, dxb, Wd, BS((tm, D), lambda i_, j, r: (i_, 0)),
                   BS((None, None, F8, D), lambda i_, j, r, li=li: (j, li, 0, 0)),
                   BS((None, tm, F8), lambda i_, j, r: (j, i_, 0)), (NDEV, S, F8), CD, (S // tm, NDEV, 1))
        g_down[i] = _mm("ffn_down_dw", TN, act, dxb, BS((None, tk, F8), lambda i_, j, r: (i_, r, 0)),
                        BS((tk, tn), lambda i_, j, r: (r, j)),
                        BS((None, F8, tn), lambda i_, j, r: (i_, 0, j)), (NDEV, F8, D), CD, (NDEV, D // tn, S // tk))
        dgu = _swiglu_bwd(dact, gu)
        Wgu, li = wgu[i]
        dh2 = _mm("ffn_gate_up_dx", NT, dgu, Wgu, BS((None, 4, tm, F8), lambda i_, j, r: (r // 2, r % 2, i_, 0)),
                  BS((4, None, None, tn, F8), lambda i_, j, r, li=li: (r % 2, li, r // 2, j, 0)),
                  BS((tm, tn), lambda i_, j, r: (i_, j)), (S, D), F32, (S // tm, D // tn, 2 * NDEV // 4),
                  pairs=lambda a_ref, b_ref: [(a_ref[s], b_ref[s]) for s in range(4)])
        g_gu[i] = _mm("ffn_gate_up_dw", TN, h2, dgu, BS((tk, tn), lambda i_, j, r: (r, i_)),
                      BS((None, None, tk, F8), lambda i_, j, r: (j // NDEV, j % NDEV, r, 0)),
                      BS((None, None, tn, F8), lambda i_, j, r: (j % NDEV, j // NDEV, i_, 0)),
                      (NDEV, 2, D, F8), CD, (D // tn, 2 * NDEV, S // tk))
        dx, dxb, g_norm_ffn[i] = _rms_bwd("norm_ffn_bwd", dh2, x1, gain_ffn, dx)
        scat["ffn", i], sent = _exchange_start(f"scatter_ffn{i}_start", [g_gu[i], g_down[i]], scatter=True)
        gain_mix = norm_mix[i:i + 1] + sent[0, 0]
        if i % 2 == 0:
            qkv, qr, kr, o, lse = mix
            do = proj_rows_dx("attn_out_dx", dxb, *wo[l], OS, CD)
            g_wo[l] = proj_rows_dw("attn_out_dw", o, dxb, OS)
            delta = _attn_delta(do, o, NQ, NKV)
            dq, dk, dv = _flash_bwd(qr, kr, qkv, do, jnp.transpose(lse[:, :, :G], (0, 2, 1)),
                                    jnp.transpose(delta[:, :, :G], (0, 2, 1)), NQ, NKV)
            dqkv, g_qg[l], g_kg[l] = _qk_prep_bwd(qkv, dq, dk, dv, attn_q_gain[l:l + 1], attn_k_gain[l:l + 1],
                                                  cos_t, sin_t, NQ, NKV)
            g_qkv[l] = proj_cols_dw("attn_qkv_dw", h1, dqkv, QS)
            dh1 = proj_cols_dx("attn_qkv_dx", dqkv, *wqkv[l], QS)
        else:
            xy, w8, xc, hs, avs, ro = mix
            dro = proj_rows_dx("rnn_out_dx", dxb, *wout[l], CS, F32)
            g_wout[l] = proj_rows_dw("rnn_out_dw", ro, dxb, CS)
            dhs, dyb = _rnn_out_bwd(dro, hs[0], hs[1], xy)
            dxc, dwa, dwi, dba, dbi, dlam = None, [], [], [], [], []
            for d in range(2):
                res = _lru_bwd("lru_bwd_rev" if d else "lru_bwd", dhs, avs[d], hs[d], xc, wai[l], d, b_a_f[l, d][None],
                               b_i_f[l, d][None], lam_f[l, d][None], bool(d), dxc)
                dxc = res[0]
                for lst, val in zip((dwa, dwi, dba, dbi, dlam), res[1:]):
                    lst.append(val)
            dxb_branch, dw8 = _conv_bwd(dxc, xy, w8)
            dxy = jnp.concatenate([dxb_branch, dyb], axis=1)
            g_win[l] = proj_cols_dw("rnn_in_dw", h1, dxy, IS)
            dh1 = proj_cols_dx("rnn_in_dx", dxy, *win[l], IS)
            dai = jnp.stack([jnp.stack(dwa), jnp.stack(dwi)])
            g_ai[l] = jnp.transpose(dai.reshape(2, 2, NB, NDEV, BWS, BW), (3, 0, 1, 2, 4, 5))
            sm = rows_at([(0, dw8[0:4]), (8, dw8[4:5]), (16, dba[0]), (17, dba[1]), (24, dbi[0]), (25, dbi[1]),
                          (32, dlam[0]), (33, dlam[1])], SMALL_ROWS)
            g_small[l] = jnp.transpose(sm.reshape(SMALL_ROWS, NDEV, CS), (1, 0, 2))
        dx, dxb, g_norm_mix[i] = _rms_bwd("norm_mix_bwd", dh1, xs, gain_mix, dx)
        if i % 2 == 1:
            scat["mix", i], sent = _exchange_start(f"scatter_mix{i}_start", [g_win[l], g_wout[l], g_ai[l], g_small[l]], scatter=True)
        elif i > 0:
            scat["mix", i], sent = _exchange_start(f"scatter_mix{i}_start", [g_qkv[l], g_wo[l]], scatter=True)

    L_gu, L_down = [None] * depth, [None] * depth
    L_qkv, L_wo = [None] * n_attn, [None] * n_attn
    L_win, L_wout, L_ai, L_small = [None] * n_rnn, [None] * n_rnn, [None] * n_rnn, [None] * n_rnn
    norm_part = rows_at(list(enumerate(g_norm_mix + g_norm_ffn)), 2 * depth)
    gain_part = rows_at(list(enumerate(g_qg + g_kg)), 8)
    to_all = lambda t: jnp.broadcast_to(t[None], (NDEV,) + t.shape)
    L_qkv[0], L_wo[0], L_norm, L_gain = _exchange("scatter_last", [g_qkv[0], g_wo[0], to_all(norm_part), to_all(gain_part)],
                                                  scatter=True)
    for i in reversed(range(depth)):
        l = i // 2
        L_gu[i], L_down[i] = _exchange_wait(f"scatter_ffn{i}_wait", scat["ffn", i], L_norm)
        if i % 2 == 1:
            L_win[l], L_wout[l], L_ai[l], L_small[l] = _exchange_wait(f"scatter_mix{i}_wait", scat["mix", i], L_norm)
        elif i > 0:
            L_qkv[l], L_wo[l] = _exchange_wait(f"scatter_mix{i}_wait", scat["mix", i], L_norm)

    out = {}
    out["attn_w_qkv"] = _adamw("adamw_qkv", L_qkv, attn_w_qkv, m_attn_w_qkv, v_attn_w_qkv)
    out["attn_w_o"] = _adamw("adamw_wo", L_wo, attn_w_o, m_attn_w_o, v_attn_w_o)
    out["rnn_w_in"] = _adamw("adamw_win", L_win, rnn_w_in, m_rnn_w_in, v_rnn_w_in)
    out["rnn_w_out"] = _adamw("adamw_wout", L_wout, rnn_w_out, m_rnn_w_out, v_rnn_w_out)
    out["ffn_w_down"] = _adamw("adamw_down", L_down, ffn_w_down, m_ffn_w_down, v_ffn_w_down)
    out["ffn_w_gate"] = _adamw("adamw_gate", L_gu, ffn_w_gate, m_ffn_w_gate, v_ffn_w_gate, pick=0)
    out["ffn_w_up"] = _adamw("adamw_up", L_gu, ffn_w_up, m_ffn_w_up, v_ffn_w_up, pick=1)
    out["rnn_w_a"] = _adamw("adamw_w_a", L_ai, rnn_w_a, m_rnn_w_a, v_rnn_w_a, pick=0)
    out["rnn_w_i"] = _adamw("adamw_w_i", L_ai, rnn_w_i, m_rnn_w_i, v_rnn_w_i, pick=1)
    small_res = _adamw("adamw_small", L_small, small_sh,
                       pack_small(m_rnn_conv_w, m_rnn_conv_b, m_rnn_b_a, m_rnn_b_i, m_rnn_lambda),
                       pack_small(v_rnn_conv_w, v_rnn_conv_b, v_rnn_b_a, v_rnn_b_i, v_rnn_lambda))
    for name, vals in zip(("rnn_conv_w", "rnn_conv_b", "rnn_b_a", "rnn_b_i", "rnn_lambda"),
                          zip(*[unpack_small(r) for r in small_res])):
        out[name] = list(vals)
    one = lambda t: t[None]
    for name, land, w, m, v in (("norm_mix", L_norm[:, :depth], norm_mix, m_norm_mix, v_norm_mix),
                                ("norm_ffn", L_norm[:, depth:], norm_ffn, m_norm_ffn, v_norm_ffn),
                                ("attn_q_gain", L_gain[:, :n_attn], attn_q_gain, m_attn_q_gain, v_attn_q_gain),
                                ("attn_k_gain", L_gain[:, n_attn:2 * n_attn], attn_k_gain, m_attn_k_gain, v_attn_k_gain)):
        out[name] = [r[0] for r in _adamw("adamw_" + name, [land], one(w), one(m), one(v))]

    order = ["norm_mix", "norm_ffn", "attn_w_qkv", "attn_q_gain", "attn_k_gain", "attn_w_o", "rnn_w_in", "rnn_conv_w",
             "rnn_conv_b", "rnn_w_a", "rnn_b_a", "rnn_w_i", "rnn_b_i", "rnn_lambda", "rnn_w_out", "ffn_w_gate",
             "ffn_w_up", "ffn_w_down"]
    return (loss, dx[None], *[out[k][0] for k in order], *[out[k][1] for k in order], *[out[k][2] for k in order],
            *[out[k][3] for k in order])
```

```python
import functools
import math

import jax
import jax.numpy as jnp
from jax import lax
from jax.experimental import pallas as pl
from jax.experimental.pallas import tpu as pltpu

F32 = jnp.float32
CD = jnp.bfloat16
NDEV = 8
HEAD_DIM = 128
GRID_W = 64
ROPE_THETA = 10000.0
ROPE_FREQS = 32
EPS = 1e-6
LRU_C = 8.0
SMALL_ROWS = 40
ADAM_LR, ADAM_B1, ADAM_B2, ADAM_EPS, ADAM_WD, ADAM_STEP = 0.001, 0.9, 0.999, 1e-08, 0.01, 10
V7X_VMEM_LIMIT = 56 * 1024 * 1024
FWD_TQ, FWD_TK = 1024, 2048
BWD_TQ, BWD_TK = 1024, 2048
BS = pl.BlockSpec
MESH = pl.DeviceIdType.MESH
NN = (((1,), (0,)), ((), ()))
NT = (((1,), (1,)), ((), ()))
TN = (((0,), (0,)), ((), ()))


def _params(*sem):
    return pltpu.CompilerParams(dimension_semantics=sem, vmem_limit_bytes=V7X_VMEM_LIMIT)


def _tile(n, want):
    return want if n % want == 0 else n


def _exchange(name, arrs, scatter):
    n = len(arrs)
    out_shape = [jax.ShapeDtypeStruct(a.shape if scatter else (NDEV,) + a.shape, a.dtype) for a in arrs]

    def body(*refs):
        ins, outs = refs[:n], refs[n:2 * n]
        send_sems, recv_sems, local_sems = refs[2 * n:]
        x, y, c = lax.axis_index("x"), lax.axis_index("y"), lax.axis_index("c")
        me = 4 * x + 2 * y + c
        local, sends, recvs = [], [], []
        for a in range(n):
            cp = pltpu.make_async_copy(ins[a].at[me] if scatter else ins[a], outs[a].at[me], local_sems.at[a])
            cp.start()
            local.append(cp)
        for k in range(1, NDEV):
            px = 1 - x if k & 4 else x
            py = 1 - y if k & 2 else y
            pc = 1 - c if k & 1 else c
            peer = 4 * px + 2 * py + pc
            for a in range(n):
                src = ins[a].at[peer] if scatter else ins[a]
                kw = dict(send_sem=send_sems.at[a, k - 1], recv_sem=recv_sems.at[a, k - 1],
                          device_id=(px, py, pc), device_id_type=MESH)
                cp = pltpu.make_async_remote_copy(src_ref=src, dst_ref=outs[a].at[me], **kw)
                cp.start()
                sends.append(cp)
                recvs.append(pltpu.make_async_remote_copy(src_ref=src, dst_ref=outs[a].at[peer], **kw))
        for cp in local:
            cp.wait()
        for cp in sends:
            cp.wait_send()
        for cp in recvs:
            cp.wait_recv()

    return pl.pallas_call(
        body, name=name, out_shape=out_shape,
        in_specs=[BS(memory_space=pl.ANY)] * n, out_specs=[BS(memory_space=pl.ANY)] * n,
        scratch_shapes=[pltpu.SemaphoreType.DMA((n, NDEV - 1)), pltpu.SemaphoreType.DMA((n, NDEV - 1)),
                        pltpu.SemaphoreType.DMA((n,))],
    )(*arrs)


HBM = BS(memory_space=pltpu.HBM)
SEM = BS(memory_space=pltpu.SEMAPHORE)
EFFECT = pltpu.SideEffectType.DATAFLOW_SIDE_EFFECTING


def _peers(x, y, c):
    out = []
    for k in range(1, NDEV):
        px = 1 - x if k & 4 else x
        py = 1 - y if k & 2 else y
        pc = 1 - c if k & 1 else c
        out.append(((px, py, pc), 4 * px + 2 * py + pc))
    return out


def _exchange_start(name, arrs, scatter, after=None):
    n = len(arrs)
    lands = [lax.empty(a.shape if scatter else (NDEV,) + a.shape, a.dtype) for a in arrs]
    extra = [] if after is None else [after]

    def body(*refs):
        ins, outs = refs[:n], refs[n:2 * n]
        send_sems, recv_sems, token = refs[2 * n + len(extra)], refs[2 * n + len(extra) + 1], refs[-1]
        x, y, c = lax.axis_index("x"), lax.axis_index("y"), lax.axis_index("c")
        me = 4 * x + 2 * y + c
        for k, (dev, peer) in enumerate(_peers(x, y, c)):
            for a in range(n):
                pltpu.make_async_remote_copy(
                    src_ref=ins[a].at[peer] if scatter else ins[a], dst_ref=outs[a].at[me],
                    send_sem=send_sems.at[a * (NDEV - 1) + k], recv_sem=recv_sems.at[a * (NDEV - 1) + k],
                    device_id=dev, device_id_type=MESH).start()
        token[...] = jnp.zeros(token.shape, F32)

    sems = pltpu.SemaphoreType.DMA((n * (NDEV - 1),))
    res = pl.pallas_call(
        body, name=name,
        out_shape=(sems, sems, *[pltpu.HBM(a.shape, a.dtype) for a in arrs], *[pltpu.HBM(l.shape, l.dtype) for l in lands],
                   jax.ShapeDtypeStruct((8, 128), F32)),
        in_specs=[HBM] * (2 * n) + [BS(memory_space=pl.ANY)] * len(extra),
        out_specs=(SEM, SEM, *([HBM] * (2 * n)), BS(memory_space=pltpu.VMEM)),
        input_output_aliases={i: i + 2 for i in range(2 * n)},
        compiler_params=pltpu.CompilerParams(has_side_effects=EFFECT),
    )(*[pltpu.with_memory_space_constraint(t, pltpu.HBM) for t in list(arrs) + lands], *extra)
    return (res[0], res[1], res[2:2 + n], res[2 + n:2 + 2 * n], scatter), res[-1]


def _exchange_wait(name, handle, after):
    send_sems, recv_sems, srcs, lands, scatter = handle
    n = len(srcs)

    def body(*refs):
        ins, outs = refs[:n], refs[n:2 * n]
        s_sems, r_sems = refs[2 * n], refs[2 * n + 1]
        x, y, c = lax.axis_index("x"), lax.axis_index("y"), lax.axis_index("c")
        me = 4 * x + 2 * y + c
        for k, (dev, peer) in enumerate(_peers(x, y, c)):
            for a in range(n):
                src = ins[a].at[peer] if scatter else ins[a]
                kw = dict(send_sem=s_sems.at[a * (NDEV - 1) + k], recv_sem=r_sems.at[a * (NDEV - 1) + k], device_id=dev,
                          device_id_type=MESH)
                pltpu.make_async_remote_copy(src_ref=src, dst_ref=outs[a].at[me], **kw).wait_send()
                pltpu.make_async_remote_copy(src_ref=src, dst_ref=outs[a].at[peer], **kw).wait_recv()

    res = pl.pallas_call(
        body, name=name, out_shape=tuple(pltpu.HBM(t.shape, t.dtype) for t in list(srcs) + list(lands)),
        in_specs=[HBM] * (2 * n) + [SEM, SEM, BS(memory_space=pl.ANY)], out_specs=tuple([HBM] * (2 * n)),
        input_output_aliases={i: i for i in range(2 * n)},
        compiler_params=pltpu.CompilerParams(has_side_effects=EFFECT),
    )(*srcs, *lands, send_sems, recv_sems, after)
    me = 4 * lax.axis_index("x") + 2 * lax.axis_index("y") + lax.axis_index("c")
    out = []
    for src, land in zip(res[:n], res[n:]):
        own = lax.dynamic_index_in_dim(src, me, 0, keepdims=True) if scatter else src[None]
        out.append(lax.dynamic_update_slice_in_dim(land, own, me, 0))
    return out


def _mm(name, dims, a, b, a_spec, b_spec, o_spec, out_shape, out_dtype, grid, add=None, pairs=None):
    nr = grid[2]
    o_blk = tuple(d for d in o_spec.block_shape if d is not None)

    def body(*refs):
        a_ref, b_ref = refs[0], refs[1]
        add_ref = refs[2] if add is not None else None
        o_ref = refs[3] if add is not None else refs[2]
        part = None
        for av, bv in ([(a_ref[...], b_ref[...])] if pairs is None else pairs(a_ref, b_ref)):
            prod = lax.dot_general(av.astype(CD), bv.astype(CD), dims, preferred_element_type=F32)
            part = prod if part is None else part + prod

        def finish(v):
            if add_ref is not None:
                v = v + add_ref[...].astype(F32)
            o_ref[...] = v.astype(o_ref.dtype)

        if nr == 1:
            finish(part)
        else:
            acc = refs[-1]
            r = pl.program_id(2)

            @pl.when(r == 0)
            def _():
                acc[...] = part

            @pl.when(r > 0)
            def _():
                acc[...] += part

            @pl.when(r == nr - 1)
            def _():
                finish(acc[...])

    return pl.pallas_call(
        body, name=name, grid=grid, out_shape=jax.ShapeDtypeStruct(out_shape, out_dtype),
        in_specs=[a_spec, b_spec] + ([o_spec] if add is not None else []), out_specs=o_spec,
        scratch_shapes=[pltpu.VMEM(o_blk, F32)] if nr > 1 else [],
        compiler_params=_params("parallel", "parallel", "arbitrary"),
    )(*([a, b] + ([add] if add is not None else [])))


def _rms_fwd(name, x, g):
    S, D = x.shape
    tr = _tile(S, 512)

    def body(x_ref, g_ref, o_ref):
        xv = x_ref[...]
        r = lax.rsqrt(jnp.mean(xv * xv, axis=-1, keepdims=True) + EPS)
        o_ref[...] = (xv * r * g_ref[...]).astype(o_ref.dtype)

    return pl.pallas_call(
        body, name=name, grid=(S // tr,), out_shape=jax.ShapeDtypeStruct((S, D), CD),
        in_specs=[BS((tr, D), lambda i: (i, 0)), BS((1, D), lambda i: (0, 0))],
        out_specs=BS((tr, D), lambda i: (i, 0)), compiler_params=_params("parallel"),
    )(x, g)


def _rms_bwd(name, dh, x, g, dx_in):
    S, D = x.shape
    tr = _tile(S, 256)

    def body(dh_ref, x_ref, g_ref, dxin_ref, dx_ref, dxb_ref, dg_ref):
        xv = x_ref[...]
        dhv = dh_ref[...].astype(F32)
        r = lax.rsqrt(jnp.mean(xv * xv, axis=-1, keepdims=True) + EPS)
        yv = xv * r
        dy = dhv * g_ref[...]
        dx = dxin_ref[...] + r * (dy - yv * jnp.mean(dy * yv, axis=-1, keepdims=True))
        dx_ref[...] = dx
        dxb_ref[...] = dx.astype(CD)
        part = jnp.sum(dhv * yv, axis=0, keepdims=True)

        @pl.when(pl.program_id(0) == 0)
        def _():
            dg_ref[...] = part

        @pl.when(pl.program_id(0) > 0)
        def _():
            dg_ref[...] += part

    row = BS((tr, D), lambda i: (i, 0))
    vec = BS((1, D), lambda i: (0, 0))
    return pl.pallas_call(
        body, name=name, grid=(S // tr,),
        out_shape=[jax.ShapeDtypeStruct((S, D), F32), jax.ShapeDtypeStruct((S, D), CD),
                   jax.ShapeDtypeStruct((1, D), F32)],
        in_specs=[row, row, vec, row], out_specs=[row, row, vec], compiler_params=_params("arbitrary"),
    )(dh, x, g, dx_in)


def _loss_head(y, target):
    S, D = y.shape
    tr = _tile(S, 512)

    def body(y_ref, t_ref, dy_ref, dyb_ref, l_ref):
        err = y_ref[...] - t_ref[...]
        dy = err * (1.0 / D)
        dy_ref[...] = dy
        dyb_ref[...] = dy.astype(CD)
        part = jnp.full((8, 128), jnp.sum(err * err), F32)

        @pl.when(pl.program_id(0) == 0)
        def _():
            l_ref[...] = part

        @pl.when(pl.program_id(0) > 0)
        def _():
            l_ref[...] += part

    row = BS((tr, D), lambda i: (i, 0))
    return pl.pallas_call(
        body, name="loss_head", grid=(S // tr,),
        out_shape=[jax.ShapeDtypeStruct((S, D), F32), jax.ShapeDtypeStruct((S, D), CD),
                   jax.ShapeDtypeStruct((8, 128), F32)],
        in_specs=[row, row], out_specs=[row, row, BS((8, 128), lambda i: (0, 0))],
        compiler_params=_params("arbitrary"),
    )(y, target)


def _sigmoid(v):
    return 1.0 / (1.0 + jnp.exp(-v))


def _swiglu_fwd(gu):
    _, J, S, F8 = gu.shape
    tr = _tile(S, 1024)

    def body(gu_ref, o_ref):
        gv = gu_ref[0].astype(F32)
        uv = gu_ref[1].astype(F32)
        o_ref[...] = (gv * _sigmoid(gv) * uv).astype(o_ref.dtype)

    return pl.pallas_call(
        body, name="swiglu_fwd", grid=(J, S // tr), out_shape=jax.ShapeDtypeStruct((J, S, F8), CD),
        in_specs=[BS((2, None, tr, F8), lambda j, i: (0, j, i, 0))],
        out_specs=BS((None, tr, F8), lambda j, i: (j, i, 0)), compiler_params=_params("parallel", "parallel"),
    )(gu)


def _swiglu_bwd(da, gu):
    _, J, S, F8 = gu.shape
    tr = _tile(S, 1024)

    def body(da_ref, gu_ref, o_ref):
        gv = gu_ref[0].astype(F32)
        uv = gu_ref[1].astype(F32)
        dav = da_ref[...].astype(F32)
        sg = _sigmoid(gv)
        o_ref[0] = (dav * uv * sg * (1.0 + gv * (1.0 - sg))).astype(o_ref.dtype)
        o_ref[1] = (dav * gv * sg).astype(o_ref.dtype)

    pair = BS((2, None, tr, F8), lambda j, i: (0, j, i, 0))
    return pl.pallas_call(
        body, name="swiglu_bwd", grid=(J, S // tr), out_shape=jax.ShapeDtypeStruct(gu.shape, CD),
        in_specs=[BS((None, tr, F8), lambda j, i: (j, i, 0)), pair], out_specs=pair,
        compiler_params=_params("parallel", "parallel"),
    )(da, gu)


def _rope_tables(S):
    freqs = ROPE_THETA ** (-jnp.arange(ROPE_FREQS, dtype=F32) / ROPE_FREQS)
    pos = jnp.arange(S, dtype=jnp.int32)
    row = (pos // GRID_W).astype(F32)[:, None] * freqs
    col = (pos % GRID_W).astype(F32)[:, None] * freqs
    cr, sr, cc, sc = jnp.cos(row), jnp.sin(row), jnp.cos(col), jnp.sin(col)
    return jnp.concatenate([cr, cr, cc, cc], axis=1), jnp.concatenate([-sr, sr, -sc, sc], axis=1)


def _swap_halves(v):
    lane = lax.broadcasted_iota(jnp.int32, v.shape, 1)
    return jnp.where((lane & 32) == 0, pltpu.roll(v, 96, 1), pltpu.roll(v, 32, 1))


def _qk_prep_fwd(qkv, qg, kg, cos_t, sin_t, NQ, NKV):
    S, W = qkv.shape
    tr = _tile(S, 256)
    scale = HEAD_DIM ** -0.5

    def body(qkv_ref, qg_ref, kg_ref, c_ref, s_ref, q_out, k_out):
        C, Sg = c_ref[...], s_ref[...]

        def one(xh, gain, sc):
            xh = xh.astype(F32)
            r = lax.rsqrt(jnp.mean(xh * xh, axis=-1, keepdims=True) + EPS)
            nv = xh * r * gain
            return (nv * C + _swap_halves(nv) * Sg) * sc

        for h in range(NQ):
            sl = slice(h * HEAD_DIM, (h + 1) * HEAD_DIM)
            q_out[:, sl] = one(qkv_ref[:, sl], qg_ref[...], scale).astype(CD)
        for h in range(NKV):
            sl = slice((NQ + h) * HEAD_DIM, (NQ + h + 1) * HEAD_DIM)
            k_out[:, h * HEAD_DIM:(h + 1) * HEAD_DIM] = one(qkv_ref[:, sl], kg_ref[...], 1.0).astype(CD)

    vec = BS((1, HEAD_DIM), lambda i: (0, 0))
    tab = BS((tr, HEAD_DIM), lambda i: (i, 0))
    return pl.pallas_call(
        body, name="qk_prep_fwd", grid=(S // tr,),
        out_shape=[jax.ShapeDtypeStruct((S, NQ * HEAD_DIM), CD), jax.ShapeDtypeStruct((S, NKV * HEAD_DIM), CD)],
        in_specs=[BS((tr, W), lambda i: (i, 0)), vec, vec, tab, tab],
        out_specs=[BS((tr, NQ * HEAD_DIM), lambda i: (i, 0)), BS((tr, NKV * HEAD_DIM), lambda i: (i, 0))],
        compiler_params=_params("parallel"),
    )(qkv, qg, kg, cos_t, sin_t)


def _qk_prep_bwd(qkv, dq, dk, dv, qg, kg, cos_t, sin_t, NQ, NKV):
    S, W = qkv.shape
    tr = _tile(S, 256)
    scale = HEAD_DIM ** -0.5

    def body(qkv_ref, dq_ref, dk_ref, dv_ref, qg_ref, kg_ref, c_ref, s_ref, o_ref, dqg_ref, dkg_ref):
        C, Sg = c_ref[...], s_ref[...]

        def one(xh, dout, gain, sc):
            xh = xh.astype(F32)
            r = lax.rsqrt(jnp.mean(xh * xh, axis=-1, keepdims=True) + EPS)
            yv = xh * r
            dout = dout.astype(F32) * sc
            dn = dout * C + _swap_halves(dout * Sg)
            dy = dn * gain
            dx = r * (dy - yv * jnp.mean(dy * yv, axis=-1, keepdims=True))
            return dx, jnp.sum(dn * yv, axis=0, keepdims=True)

        dqg = jnp.zeros((1, HEAD_DIM), F32)
        dkg = jnp.zeros((1, HEAD_DIM), F32)
        for h in range(NQ):
            sl = slice(h * HEAD_DIM, (h + 1) * HEAD_DIM)
            dx, dg = one(qkv_ref[:, sl], dq_ref[:, sl], qg_ref[...], scale)
            o_ref[:, sl] = dx.astype(CD)
            dqg = dqg + dg
        for h in range(NKV):
            sl = slice((NQ + h) * HEAD_DIM, (NQ + h + 1) * HEAD_DIM)
            dx, dg = one(qkv_ref[:, sl], dk_ref[:, h * HEAD_DIM:(h + 1) * HEAD_DIM], kg_ref[...], 1.0)
            o_ref[:, sl] = dx.astype(CD)
            dkg = dkg + dg
        o_ref[:, (NQ + NKV) * HEAD_DIM:] = dv_ref[...].astype(CD)

        @pl.when(pl.program_id(0) == 0)
        def _():
            dqg_ref[...] = dqg
            dkg_ref[...] = dkg

        @pl.when(pl.program_id(0) > 0)
        def _():
            dqg_ref[...] += dqg
            dkg_ref[...] += dkg

    vec = BS((1, HEAD_DIM), lambda i: (0, 0))
    tab = BS((tr, HEAD_DIM), lambda i: (i, 0))
    full = BS((tr, W), lambda i: (i, 0))
    return pl.pallas_call(
        body, name="qk_prep_bwd", grid=(S // tr,),
        out_shape=[jax.ShapeDtypeStruct((S, W), CD), jax.ShapeDtypeStruct((1, HEAD_DIM), F32),
                   jax.ShapeDtypeStruct((1, HEAD_DIM), F32)],
        in_specs=[full, BS((tr, NQ * HEAD_DIM), lambda i: (i, 0)), BS((tr, NKV * HEAD_DIM), lambda i: (i, 0)),
                  BS((tr, NKV * HEAD_DIM), lambda i: (i, 0)), vec, vec, tab, tab],
        out_specs=[full, vec, vec], compiler_params=_params("arbitrary"),
    )(qkv, dq, dk, dv, qg, kg, cos_t, sin_t)


def _lane_pack(cols):
    rows = cols[0].shape[0]
    lane = lax.broadcasted_iota(jnp.int32, (rows, HEAD_DIM), 1)
    out = jnp.zeros((rows, HEAD_DIM), F32)
    for g, col in enumerate(cols):
        out = jnp.where(lane == g, col, out)
    return out


def _flash_fwd(qr, kr, qkv, NQ, NKV):
    S = qr.shape[0]
    G = NQ // NKV
    tq, tk = _tile(S, FWD_TQ), _tile(S, FWD_TK)
    nk = S // tk
    GW = G * HEAD_DIM

    def body(q_ref, k_ref, v_ref, o_ref, lse_ref, m_s, acc_s):
        ki = pl.program_id(2)

        @pl.when(ki == 0)
        def _():
            m_s[...] = jnp.full(m_s.shape, -1e30, F32)
            acc_s[...] = jnp.zeros(acc_s.shape, F32)

        kv = k_ref[...]
        v_ones = jnp.concatenate([v_ref[...], jnp.ones((tk, HEAD_DIM), CD)], axis=1)
        for g in range(G):
            s = lax.dot_general(q_ref[:, g * HEAD_DIM:(g + 1) * HEAD_DIM], kv, NT, preferred_element_type=F32)
            m_prev = m_s[g]
            m_new = jnp.maximum(m_prev, jnp.max(s, axis=-1, keepdims=True))
            alpha = jnp.exp(m_prev - m_new)
            p = jnp.exp(s - jnp.tile(m_new, (1, tk // HEAD_DIM)))
            acc_s[g] = jnp.tile(alpha, (1, 2)) * acc_s[g] + jnp.dot(p.astype(CD), v_ones, preferred_element_type=F32)
            m_s[g] = m_new

        @pl.when(ki == nk - 1)
        def _():
            lane = lax.broadcasted_iota(jnp.int32, (tq, HEAD_DIM), 1)
            lse = jnp.zeros((tq, HEAD_DIM), F32)
            for g in range(G):
                l = acc_s[g, :, HEAD_DIM:]
                o_ref[:, g * HEAD_DIM:(g + 1) * HEAD_DIM] = (acc_s[g, :, :HEAD_DIM] / l).astype(o_ref.dtype)
                lse = jnp.where(lane == g, m_s[g] + jnp.log(l), lse)
            lse_ref[...] = lse

    return pl.pallas_call(
        body, name="flash_fwd", grid=(NKV, S // tq, nk),
        out_shape=[jax.ShapeDtypeStruct((S, NQ * HEAD_DIM), CD), jax.ShapeDtypeStruct((NKV, S, HEAD_DIM), F32)],
        in_specs=[BS((tq, GW), lambda h, i, k: (i, h)), BS((tk, HEAD_DIM), lambda h, i, k: (k, h)),
                  BS((tk, HEAD_DIM), lambda h, i, k: (k, NQ + NKV + h))],
        out_specs=[BS((tq, GW), lambda h, i, k: (i, h)), BS((None, tq, HEAD_DIM), lambda h, i, k: (h, i, 0))],
        scratch_shapes=[pltpu.VMEM((G, tq, HEAD_DIM), F32), pltpu.VMEM((G, tq, 2 * HEAD_DIM), F32)],
        compiler_params=_params("parallel", "parallel", "arbitrary"),
    )(qr, kr, qkv)


def _attn_delta(do, o, NQ, NKV):
    S = o.shape[0]
    G = NQ // NKV
    tr = _tile(S, 512)

    def body(do_ref, o_ref, d_ref):
        for kv in range(NKV):
            cols = []
            for g in range(G):
                sl = slice((kv * G + g) * HEAD_DIM, (kv * G + g + 1) * HEAD_DIM)
                cols.append(jnp.sum(do_ref[:, sl].astype(F32) * o_ref[:, sl].astype(F32), axis=-1, keepdims=True))
            d_ref[kv] = _lane_pack(cols)

    row = BS((tr, NQ * HEAD_DIM), lambda i: (i, 0))
    return pl.pallas_call(
        body, name="attn_delta", grid=(S // tr,), out_shape=jax.ShapeDtypeStruct((NKV, S, HEAD_DIM), F32),
        in_specs=[row, row], out_specs=BS((NKV, tr, HEAD_DIM), lambda i: (0, i, 0)),
        compiler_params=_params("parallel"),
    )(do, o)


def _flash_bwd(qr, kr, qkv, do, lse_rows, delta_rows, NQ, NKV):
    S = qr.shape[0]
    G = NQ // NKV
    tq, tk = _tile(S, BWD_TQ), _tile(S, BWD_TK)
    nq, nk = S // tq, S // tk
    GW = G * HEAD_DIM

    def body(q_ref, k_ref, v_ref, do_ref, lse_ref, dl_ref, dq_ref, dk_ref, dv_ref, dq_s, dk_s, dv_s):
        ki, qi = pl.program_id(1), pl.program_id(2)
        rows = pl.ds(pl.multiple_of(qi * tq, tq), tq)

        @pl.when(qi == 0)
        def _():
            dk_s[...] = jnp.zeros(dk_s.shape, F32)
            dv_s[...] = jnp.zeros(dv_s.shape, F32)

        kv, vv = k_ref[...], v_ref[...]
        for g in range(G):
            sl = slice(g * HEAD_DIM, (g + 1) * HEAD_DIM)
            qg, dog = q_ref[:, sl], do_ref[:, sl]
            st = lax.dot_general(kv, qg, NT, preferred_element_type=F32)
            pt = jnp.exp(st - lse_ref[g:g + 1, :])
            dv_s[...] += jnp.dot(pt.astype(CD), dog, preferred_element_type=F32)
            dpt = lax.dot_general(vv, dog, NT, preferred_element_type=F32)
            dst = (pt * (dpt - dl_ref[g:g + 1, :])).astype(CD)
            dk_s[...] += jnp.dot(dst, qg, preferred_element_type=F32)
            dq_part = lax.dot_general(dst, kv, TN, preferred_element_type=F32)

            @pl.when(ki == 0)
            def _():
                dq_s[g, rows, :] = dq_part

            @pl.when(ki > 0)
            def _():
                dq_s[g, rows, :] += dq_part

        @pl.when(qi == nq - 1)
        def _():
            dk_ref[...] = dk_s[...].astype(dk_ref.dtype)
            dv_ref[...] = dv_s[...].astype(dv_ref.dtype)

        @pl.when(ki == nk - 1)
        def _():
            for g in range(G):
                dq_ref[rows, g * HEAD_DIM:(g + 1) * HEAD_DIM] = dq_s[g, rows, :].astype(dq_ref.dtype)

    qb = BS((tq, GW), lambda h, k, i: (i, h))
    st = BS((None, G, tq), lambda h, k, i: (h, 0, i))
    kb = BS((tk, HEAD_DIM), lambda h, k, i: (k, h))
    return pl.pallas_call(
        body, name="flash_bwd", grid=(NKV, nk, nq),
        out_shape=[jax.ShapeDtypeStruct((S, NQ * HEAD_DIM), CD), jax.ShapeDtypeStruct((S, NKV * HEAD_DIM), CD),
                   jax.ShapeDtypeStruct((S, NKV * HEAD_DIM), CD)],
        in_specs=[qb, kb, BS((tk, HEAD_DIM), lambda h, k, i: (k, NQ + NKV + h)), qb, st, st],
        out_specs=[BS((S, GW), lambda h, k, i: (0, h)), kb, kb],
        scratch_shapes=[pltpu.VMEM((G, S, HEAD_DIM), F32), pltpu.VMEM((tk, HEAD_DIM), F32), pltpu.VMEM((tk, HEAD_DIM), F32)],
        compiler_params=_params("parallel", "arbitrary", "arbitrary"),
    )(qr, kr, qkv, do, lse_rows, delta_rows)


HALO8 = 8


def _round(v):
    return v.astype(CD).astype(F32)


def _shift_down(v, k, above):
    T = v.shape[0]
    row = lax.broadcasted_iota(jnp.int32, v.shape, 0)
    out = pltpu.roll(v, k, 0)
    for j in range(k):
        out = jnp.where(row == j, above[above.shape[0] - k + j:above.shape[0] - k + j + 1, :], out)
    return out


def _shift_up(v, k, below):
    T = v.shape[0]
    row = lax.broadcasted_iota(jnp.int32, v.shape, 0)
    out = pltpu.roll(v, T - k, 0)
    for j in range(k):
        out = jnp.where(row == T - k + j, below[j:j + 1, :], out)
    return out


def _halo_specs(T, C, halo, n_chunks, S, col=0, rev=False):
    per = T // halo
    last = S // halo - 1
    ch = (lambda i: n_chunks - 1 - i) if rev else (lambda i: i)
    return (BS((T, C), lambda i: (ch(i), col)),
            BS((halo, C), lambda i: (jnp.maximum(ch(i) * per - 1, 0), col)),
            BS((halo, C), lambda i: (jnp.minimum((ch(i) + 1) * per, last), col)))


def _conv_fwd(xy, w8, bias):
    S, C2 = xy.shape
    C = C2 // 2
    T = _tile(S, 256)
    n = S // T

    def body(x_ref, p_ref, q_ref, w_ref, b_ref, o_ref):
        i = pl.program_id(0)
        xv = _round(x_ref[...])
        above = jnp.where(i > 0, _round(p_ref[...]), 0.0)
        below = jnp.where(i < n - 1, _round(q_ref[...]), 0.0)
        wv = _round(w_ref[...])
        o_ref[...] = (wv[0:1, :] * _shift_down(xv, 2, above) + wv[1:2, :] * _shift_down(xv, 1, above)
                      + wv[2:3, :] * xv + wv[3:4, :] * _shift_up(xv, 1, below) + b_ref[...])

    main, prev, nxt = _halo_specs(T, C, HALO8, n, S)
    return pl.pallas_call(
        body, name="conv_fwd", grid=(n,), out_shape=jax.ShapeDtypeStruct((S, C), F32),
        in_specs=[main, prev, nxt, BS((8, C), lambda i: (0, 0)), BS((1, C), lambda i: (0, 0))],
        out_specs=BS((T, C), lambda i: (i, 0)), compiler_params=_params("parallel"),
    )(xy, xy, xy, w8, bias)


def _tile_scan(a, b, reverse):
    row = lax.broadcasted_iota(jnp.int32, a.shape, 0)
    for sh in (1, 2, 4):
        if reverse:
            a_s, b_s, ok = pltpu.roll(a, 8 - sh, 0), pltpu.roll(b, 8 - sh, 0), row < 8 - sh
        else:
            a_s, b_s, ok = pltpu.roll(a, sh, 0), pltpu.roll(b, sh, 0), row >= sh
        b = jnp.where(ok, a * b_s + b, b)
        a = jnp.where(ok, a * a_s, a)
    return a, b


def _scan_chunk(a_ref, h_ref, carry_ref, reverse):
    T = h_ref.shape[0]
    nt = T // 8

    def step(j, carry):
        t = (nt - 1 - j) if reverse else j
        rows = pl.ds(pl.multiple_of(t * 8, 8), 8)
        ca, hb = _tile_scan(a_ref[rows, :], h_ref[rows, :], reverse)
        hv = hb + ca * carry
        h_ref[rows, :] = hv
        edge = hv[0:1, :] if reverse else hv[7:8, :]
        return jnp.broadcast_to(edge, hv.shape)

    carry_ref[...] = lax.fori_loop(0, nt, step, carry_ref[...])


def _softplus(z):
    return jnp.maximum(z, 0.0) + jnp.log(1.0 + jnp.exp(-jnp.abs(z)))


def _neg_expm1(z):
    return jnp.where(z > -0.01, -z * (1.0 + z * (0.5 + z * (1.0 / 6.0 + z * (1.0 / 24.0)))), 1.0 - jnp.exp(z))


def _gates(xs, wa, wi, ba, bi, sp):
    r = _sigmoid(jnp.dot(xs, wa, preferred_element_type=F32) + ba)
    ig = _sigmoid(jnp.dot(xs, wi, preferred_element_type=F32) + bi)
    la = -LRU_C * r * sp
    return r, ig, jnp.exp(la), jnp.sqrt(_neg_expm1(2.0 * la))


def _lru_fwd(name, xc, wai, d, ba, bi, lam, reverse):
    S, C = xc.shape
    NB, BW = wai.shape[2], wai.shape[3]
    T = _tile(S, 256)
    n = S // T
    ch = (lambda i: n - 1 - i) if reverse else (lambda i: i)

    def body(x_ref, wa_ref, wi_ref, ba_ref, bi_ref, lam_ref, h_ref, a_ref, carry):
        @pl.when(pl.program_id(0) == 0)
        def _():
            carry[...] = jnp.zeros(carry.shape, F32)

        for b in range(NB):
            sl = slice(b * BW, (b + 1) * BW)
            xf = x_ref[:, sl]
            _, ig, av, sq = _gates(xf.astype(CD), wa_ref[b], wi_ref[b], ba_ref[:, sl], bi_ref[:, sl],
                                   _softplus(-lam_ref[:, sl]))
            a_ref[:, sl] = av
            h_ref[:, sl] = sq * ig * xf
        _scan_chunk(a_ref, h_ref, carry, reverse)

    vec = BS((1, C), lambda i: (0, 0))
    blk = BS((T, C), lambda i: (ch(i), 0))
    return pl.pallas_call(
        body, name=name, grid=(n,),
        out_shape=[jax.ShapeDtypeStruct((S, C), F32), jax.ShapeDtypeStruct((S, C), F32)],
        in_specs=[blk, BS((None, None, NB, BW, BW), lambda i: (0, d, 0, 0, 0)),
                  BS((None, None, NB, BW, BW), lambda i: (1, d, 0, 0, 0)), vec, vec, vec],
        out_specs=[blk, blk], scratch_shapes=[pltpu.VMEM((8, C), F32)], compiler_params=_params("arbitrary"),
    )(xc, wai, wai, ba, bi, lam)


_GELU_C = math.sqrt(2.0 / math.pi)


def _rnn_out_fwd(h0, h1, xy):
    S, C = h0.shape
    tr = _tile(S, 256)

    def body(h0_ref, h1_ref, y_ref, o_ref):
        yv = y_ref[...].astype(F32)
        gl = 0.5 * yv * (1.0 + jnp.tanh(_GELU_C * (yv + 0.044715 * yv * yv * yv)))
        o_ref[...] = ((h0_ref[...] + h1_ref[...]) * gl).astype(o_ref.dtype)

    row = BS((tr, C), lambda i: (i, 0))
    return pl.pallas_call(
        body, name="rnn_out_fwd", grid=(S // tr,), out_shape=jax.ShapeDtypeStruct((S, C), CD),
        in_specs=[row, row, BS((tr, C), lambda i: (i, 1))], out_specs=row, compiler_params=_params("parallel"),
    )(h0, h1, xy)


def _rnn_out_bwd(dout, h0, h1, xy):
    S, C = h0.shape
    tr = _tile(S, 256)

    def body(d_ref, h0_ref, h1_ref, y_ref, dh_ref, dy_ref):
        yv = y_ref[...].astype(F32)
        dv = d_ref[...].astype(F32)
        t = jnp.tanh(_GELU_C * (yv + 0.044715 * yv * yv * yv))
        gl = 0.5 * yv * (1.0 + t)
        dgl = 0.5 * (1.0 + t) + 0.5 * yv * (1.0 - t * t) * _GELU_C * (1.0 + 3.0 * 0.044715 * yv * yv)
        dh_ref[...] = dv * gl
        dy_ref[...] = (dv * (h0_ref[...] + h1_ref[...]) * dgl).astype(dy_ref.dtype)

    row = BS((tr, C), lambda i: (i, 0))
    return pl.pallas_call(
        body, name="rnn_out_bwd", grid=(S // tr,),
        out_shape=[jax.ShapeDtypeStruct((S, C), F32), jax.ShapeDtypeStruct((S, C), CD)],
        in_specs=[row, row, row, BS((tr, C), lambda i: (i, 1))], out_specs=[row, row],
        compiler_params=_params("parallel"),
    )(dout, h0, h1, xy)


def _lru_bwd(name, dhs, a, h, xc, wai, d, ba, bi, lam, reverse, dxc_in):
    S, C = xc.shape
    NB, BW = wai.shape[2], wai.shape[3]
    T = _tile(S, 256)
    n = S // T
    back = not reverse

    def body(*refs):
        (dh_ref, a_ref, a_prev, a_next, h_ref, h_prev, h_next, x_ref, wa_ref, wi_ref, ba_ref, bi_ref,
         lam_ref) = refs[:13]
        k = 13
        add_ref = None
        if dxc_in is not None:
            add_ref = refs[k]
            k += 1
        dx_ref, dwa_ref, dwi_ref, dba_ref, dbi_ref, dlam_ref, coef_s, lam_s, carry, dsp_s = refs[k:]
        i = pl.program_id(0)
        first_chunk = i == (n - 1 if back else 0)
        last_chunk = i == (0 if back else n - 1)

        @pl.when(i == 0)
        def _():
            carry[...] = jnp.zeros(carry.shape, F32)
            dsp_s[...] = jnp.zeros(dsp_s.shape, F32)
            dwa_ref[...] = jnp.zeros(dwa_ref.shape, F32)
            dwi_ref[...] = jnp.zeros(dwi_ref.shape, F32)
            dba_ref[...] = jnp.zeros(dba_ref.shape, F32)
            dbi_ref[...] = jnp.zeros(dbi_ref.shape, F32)

        av, hv = a_ref[...], h_ref[...]
        if reverse:
            coef_s[...] = _shift_down(av, 1, a_prev[...])
            h_nb = _shift_up(hv, 1, jnp.where(last_chunk, 0.0, h_next[...]))
        else:
            coef_s[...] = _shift_up(av, 1, a_next[...])
            h_nb = _shift_down(hv, 1, jnp.where(first_chunk, 0.0, h_prev[...]))
        lam_s[...] = dh_ref[...]
        _scan_chunk(coef_s, lam_s, carry, back)

        for b in range(NB):
            sl = slice(b * BW, (b + 1) * BW)
            xf = x_ref[:, sl]
            xs = xf.astype(CD)
            sp = _softplus(-lam_ref[:, sl])
            r, ig, ab, sq = _gates(xs, wa_ref[b], wi_ref[b], ba_ref[:, sl], bi_ref[:, sl], sp)
            du = lam_s[:, sl]
            da = du * h_nb[:, sl]
            d_ig = du * sq * xf
            d_sq = du * ig * xf
            d_la = da * ab - d_sq * (ab * ab) / sq
            d_pa = d_la * (-LRU_C * sp) * r * (1.0 - r)
            d_pi = d_ig * ig * (1.0 - ig)
            dsp_s[:, sl] += jnp.sum(d_la * (-LRU_C * r), axis=0, keepdims=True)
            dba_ref[:, sl] += jnp.sum(d_pa, axis=0, keepdims=True)
            dbi_ref[:, sl] += jnp.sum(d_pi, axis=0, keepdims=True)
            d_pa_c, d_pi_c = d_pa.astype(CD), d_pi.astype(CD)
            dwa_ref[b] += lax.dot_general(xs, d_pa_c, TN, preferred_element_type=F32)
            dwi_ref[b] += lax.dot_general(xs, d_pi_c, TN, preferred_element_type=F32)
            dx = (du * sq * ig + lax.dot_general(d_pa_c, wa_ref[b], NT, preferred_element_type=F32)
                  + lax.dot_general(d_pi_c, wi_ref[b], NT, preferred_element_type=F32))
            if add_ref is not None:
                dx = dx + add_ref[:, sl]
            dx_ref[:, sl] = dx

        @pl.when(i == n - 1)
        def _():
            dlam_ref[...] = -dsp_s[...] * _sigmoid(-lam_ref[...])

    vec = BS((1, C), lambda i: (0, 0))
    main, prev, nxt = _halo_specs(T, C, HALO8, n, S, rev=back)
    wspec = lambda which: BS((None, None, NB, BW, BW), lambda i: (which, d, 0, 0, 0))
    acc = BS((NB, BW, BW), lambda i: (0, 0, 0))
    ins = [dhs, a, a, a, h, h, h, xc, wai, wai, ba, bi, lam]
    in_specs = [main, main, prev, nxt, main, prev, nxt, main, wspec(0), wspec(1), vec, vec, vec]
    if dxc_in is not None:
        ins.append(dxc_in)
        in_specs.append(main)
    return pl.pallas_call(
        body, name=name, grid=(n,),
        out_shape=[jax.ShapeDtypeStruct((S, C), F32), jax.ShapeDtypeStruct((NB, BW, BW), F32),
                   jax.ShapeDtypeStruct((NB, BW, BW), F32), jax.ShapeDtypeStruct((1, C), F32),
                   jax.ShapeDtypeStruct((1, C), F32), jax.ShapeDtypeStruct((1, C), F32)],
        in_specs=in_specs, out_specs=[main, acc, acc, vec, vec, vec],
        scratch_shapes=[pltpu.VMEM((T, C), F32), pltpu.VMEM((T, C), F32), pltpu.VMEM((8, C), F32),
                        pltpu.VMEM((1, C), F32)],
        compiler_params=_params("arbitrary"),
    )(*ins)


def _conv_bwd(dxc, xy, w8):
    S, C = dxc.shape
    T = _tile(S, 256)
    n = S // T

    def body(d_ref, dp_ref, dn_ref, x_ref, xp_ref, xn_ref, w_ref, dx_ref, dw_ref):
        i = pl.program_id(0)
        dv = _round(d_ref[...])
        d_above = jnp.where(i > 0, _round(dp_ref[...]), 0.0)
        d_below = jnp.where(i < n - 1, _round(dn_ref[...]), 0.0)
        wv = _round(w_ref[...])
        dx = (wv[0:1, :] * _shift_up(dv, 2, d_below) + wv[1:2, :] * _shift_up(dv, 1, d_below)
              + wv[2:3, :] * dv + wv[3:4, :] * _shift_down(dv, 1, d_above))
        dx_ref[...] = dx.astype(dx_ref.dtype)
        xv = _round(x_ref[...])
        above = jnp.where(i > 0, _round(xp_ref[...]), 0.0)
        below = jnp.where(i < n - 1, _round(xn_ref[...]), 0.0)
        rows = [jnp.sum(dv * _shift_down(xv, 2, above), axis=0, keepdims=True),
                jnp.sum(dv * _shift_down(xv, 1, above), axis=0, keepdims=True),
                jnp.sum(dv * xv, axis=0, keepdims=True),
                jnp.sum(dv * _shift_up(xv, 1, below), axis=0, keepdims=True),
                jnp.sum(dv, axis=0, keepdims=True)]
        rid = lax.broadcasted_iota(jnp.int32, (8, C), 0)
        part = jnp.zeros((8, C), F32)
        for j, rv in enumerate(rows):
            part = jnp.where(rid == j, rv, part)

        @pl.when(i == 0)
        def _():
            dw_ref[...] = part

        @pl.when(i > 0)
        def _():
            dw_ref[...] += part

    dmain, dprev, dnext = _halo_specs(T, C, HALO8, n, S)
    xmain, xprev, xnext = _halo_specs(T, C, HALO8, n, S)
    w = BS((8, C), lambda i: (0, 0))
    return pl.pallas_call(
        body, name="conv_bwd", grid=(n,),
        out_shape=[jax.ShapeDtypeStruct((S, C), CD), jax.ShapeDtypeStruct((8, C), F32)],
        in_specs=[dmain, dprev, dnext, xmain, xprev, xnext, w], out_specs=[dmain, w],
        compiler_params=_params("arbitrary"),
    )(dxc, dxc, dxc, xy, xy, xy, w8)


def _adamw(name, lands, w, m, v, pick=None):
    c1 = 1.0 - ADAM_B1 ** ADAM_STEP
    c2 = 1.0 - ADAM_B2 ** ADAM_STEP
    L, C = w.shape[0], w.shape[-1]
    w3, m3, v3 = (t.reshape(L, -1, C) for t in (w, m, v))
    R = w3.shape[1]
    tr = R
    for cand in (1024, 512, 256, 128, 64, 32, 16, 8):
        if R % cand == 0 and cand * C * 4 <= (1 << 20):
            tr = cand
            break

    def body(*refs):
        l_refs = refs[:L]
        w_ref, m_ref, v_ref, g_out, d_out, m_out, v_out = refs[L:]

        def update(l_ref):
            g = l_ref[0].astype(F32)
            for s in range(1, NDEV):
                g = g + l_ref[s].astype(F32)
            mn = ADAM_B1 * m_ref[...] + (1.0 - ADAM_B1) * g
            vn = ADAM_B2 * v_ref[...] + (1.0 - ADAM_B2) * (g * g)
            g_out[...] = g
            m_out[...] = mn
            v_out[...] = vn
            d_out[...] = -ADAM_LR * ((mn / c1) / (jnp.sqrt(vn / c2) + ADAM_EPS) + ADAM_WD * w_ref[...])

        for k in range(L):
            pl.when(pl.program_id(0) == k)(functools.partial(update, l_refs[k]))

    if pick is None:
        lands = [t.reshape(NDEV, R, C) for t in lands]
        land_specs = [BS((NDEV, tr, C), lambda l, i, k=k: (0, jnp.where(l == k, i, 0), 0)) for k in range(L)]
    else:
        lands = [t.reshape(NDEV, -1, R, C) for t in lands]
        land_specs = [BS((NDEV, None, tr, C), lambda l, i, k=k: (0, pick, jnp.where(l == k, i, 0), 0)) for k in range(L)]
    spec = BS((None, tr, C), lambda l, i: (l, i, 0))
    sds = jax.ShapeDtypeStruct(w3.shape, F32)
    outs = pl.pallas_call(
        body, name=name, grid=(L, R // tr), out_shape=[sds] * 4, in_specs=land_specs + [spec, spec, spec],
        out_specs=[spec] * 4, compiler_params=_params("arbitrary", "arbitrary"),
    )(*lands, w3, m3, v3)
    return [o.reshape(w.shape) for o in outs]


def kernel(x, norm_mix, norm_ffn, attn_w_qkv, attn_q_gain, attn_k_gain, attn_w_o, rnn_w_in, rnn_conv_w, rnn_conv_b, rnn_w_a, rnn_b_a, rnn_w_i, rnn_b_i, rnn_lambda, rnn_w_out, ffn_w_gate, ffn_w_up, ffn_w_down, loss_target, m_norm_mix, m_norm_ffn, m_attn_w_qkv, m_attn_q_gain, m_attn_k_gain, m_attn_w_o, m_rnn_w_in, m_rnn_conv_w, m_rnn_conv_b, m_rnn_w_a, m_rnn_b_a, m_rnn_w_i, m_rnn_b_i, m_rnn_lambda, m_rnn_w_out, m_ffn_w_gate, m_ffn_w_up, m_ffn_w_down, v_norm_mix, v_norm_ffn, v_attn_w_qkv, v_attn_q_gain, v_attn_k_gain, v_attn_w_o, v_rnn_w_in, v_rnn_conv_w, v_rnn_conv_b, v_rnn_w_a, v_rnn_b_a, v_rnn_w_i, v_rnn_b_i, v_rnn_lambda, v_rnn_w_out, v_ffn_w_gate, v_ffn_w_up, v_ffn_w_down):
    S, D = x.shape[1], x.shape[2]
    depth = norm_mix.shape[0]
    n_attn, n_rnn = attn_w_qkv.shape[0], rnn_w_in.shape[0]
    QS = attn_w_qkv.shape[2]
    NQ = D // HEAD_DIM
    NKV = (QS * NDEV // HEAD_DIM - NQ) // 2
    G = NQ // NKV
    OS = attn_w_o.shape[1]
    IS = rnn_w_in.shape[2]
    C = rnn_w_out.shape[1] * NDEV
    CS = C // NDEV
    NB, BWS, BW = rnn_w_a.shape[2], rnn_w_a.shape[3], rnn_w_a.shape[4]
    F8 = ffn_w_gate.shape[2]
    tm = _tile(S, 1024)
    tn = _tile(D, 1024)
    tk = _tile(S, 2048)

    def rows_at(parts, total):
        out = None
        for at, p in parts:
            cfg = [(0, 0)] * (p.ndim - 2) + [(at, total - at - p.shape[-2]), (0, 0)]
            out = jnp.pad(p, cfg) if out is None else out + jnp.pad(p, cfg)
        return out

    def pack_small(cw, cb, b_a, b_i, lam):
        return rows_at([(0, cw), (8, cb[:, None, :]), (16, b_a), (24, b_i), (32, lam)], SMALL_ROWS)

    def unpack_small(p):
        return p[:, 0:4], p[:, 8], p[:, 16:18], p[:, 24:26], p[:, 32:34]

    gu_sh = jnp.stack([ffn_w_gate, ffn_w_up], axis=1).astype(CD)
    ai_sh = jnp.stack([rnn_w_a, rnn_w_i], axis=1).astype(CD)
    small_sh = pack_small(rnn_conv_w, rnn_conv_b, rnn_b_a, rnn_b_i, rnn_lambda)
    (Wqkv0,) = _exchange("gather_first", [attn_w_qkv[0:1].astype(CD)], scatter=False)
    wo0_handle, wo0_token = _exchange_start("gather_wo0_start", [attn_w_o[0:1].astype(CD)], scatter=False, after=Wqkv0)
    ffn0_handle, ffn0_token = _exchange_start("gather_ffn0_start", [gu_sh[0:1], ffn_w_down[0:1].astype(CD)], scatter=False,
                                              after=wo0_token)
    gathers, newest = {}, ffn0_token
    for i in range(1, depth):
        l = i // 2
        if i % 2 == 0:
            arrs = [attn_w_qkv[l:l + 1].astype(CD), attn_w_o[l:l + 1].astype(CD)]
        else:
            arrs = [rnn_w_in[l:l + 1].astype(CD), rnn_w_out[l:l + 1].astype(CD), ai_sh[l:l + 1]] + ([small_sh] if l == 0 else [])
        gathers[i], newest = _exchange_start(f"gather_layer{i}_start", arrs + [gu_sh[i:i + 1], ffn_w_down[i:i + 1].astype(CD)],
                                             scatter=False, after=newest)
    started = newest[0, 0]
    wqkv, wo = {0: (Wqkv0, 0)}, {}
    win, wout, wai, wgu, wd = {}, {}, {}, {}, {}
    cos_t, sin_t = _rope_tables(S)

    def proj_cols(name, a, W, l, width, dtype):
        return _mm(name, NN, a, W, BS((tm, D), lambda i, j, r: (i, 0)),
                   BS((None, None, D, width), lambda i, j, r: (j, l, 0, 0)),
                   BS((tm, width), lambda i, j, r: (i, j)), (S, NDEV * width), dtype, (S // tm, NDEV, 1))

    def proj_cols_dx(name, dy, W, l, width):
        ns = 4
        return _mm(name, NT, dy, W, BS((tm, ns * width), lambda i, j, r: (i, r)),
                   BS((ns, None, tn, width), lambda i, j, r: (r, l, j, 0)),
                   BS((tm, tn), lambda i, j, r: (i, j)), (S, D), F32, (S // tm, D // tn, NDEV // ns),
                   pairs=lambda a_ref, b_ref: [(a_ref[:, s * width:(s + 1) * width], b_ref[s]) for s in range(ns)])

    def proj_cols_dw(name, a, dy, width):
        return _mm(name, TN, a, dy, BS((tk, tn), lambda i, j, r: (r, i)),
                   BS((tk, width), lambda i, j, r: (r, j)),
                   BS((None, tn, width), lambda i, j, r: (j, i, 0)), (NDEV, D, width), CD, (D // tn, NDEV, S // tk))

    def proj_rows(name, a, W, l, rows, resid):
        return _mm(name, NN, a, W, BS((tm, NDEV * rows), lambda i, j, r: (i, 0)),
                   BS((NDEV, None, rows, tn), lambda i, j, r: (0, l, 0, j)),
                   BS((tm, tn), lambda i, j, r: (i, j)), (S, D), F32, (S // tm, D // tn, 1), add=resid,
                   pairs=lambda a_ref, b_ref: [(a_ref[...], b_ref[...].reshape(NDEV * rows, tn))])

    def proj_rows_dx(name, dy, W, l, rows, dtype):
        return _mm(name, NT, dy, W, BS((tm, D), lambda i, j, r: (i, 0)),
                   BS((None, None, rows, D), lambda i, j, r: (j, l, 0, 0)),
                   BS((tm, rows), lambda i, j, r: (i, j)), (S, NDEV * rows), dtype, (S // tm, NDEV, 1))

    def proj_rows_dw(name, a, dy, rows):
        width = NDEV * rows
        tw = _tile(width, 1024)
        out = _mm(name, TN, a, dy, BS((tk, tw), lambda i, j, r: (r, i)), BS((tk, tn), lambda i, j, r: (r, j)),
                  BS((tw, tn), lambda i, j, r: (i, j)), (width, D), CD, (width // tw, D // tn, S // tk))
        return out.reshape(NDEV, rows, D)

    saved = []
    xs = x[0]
    for i in range(depth):
        l = i // 2
        h1 = _rms_fwd("norm_mix_fwd", xs, norm_mix[i:i + 1] + started if i == 0 else norm_mix[i:i + 1])
        if i % 2 == 0:
            qkv = proj_cols("attn_qkv", h1, *wqkv[l], QS, CD)
            qr, kr = _qk_prep_fwd(qkv, attn_q_gain[l:l + 1], attn_k_gain[l:l + 1], cos_t, sin_t, NQ, NKV)
            o, lse = _flash_fwd(qr, kr, qkv, NQ, NKV)
            if i == 0:
                wo[0] = (_exchange_wait("gather_wo0_wait", wo0_handle, o)[0], 0)
            x1 = proj_rows("attn_out", o, *wo[l], OS, xs)
            mix = (qkv, qr, kr, o, lse)
        else:
            xy = proj_cols("rnn_in", h1, *win[l], IS, F32)
            w8 = jnp.pad(conv_w_f[l], ((0, 4), (0, 0)))
            xc = _conv_fwd(xy, w8, conv_b_f[l][None])
            hs, avs = [], []
            for d in range(2):
                hd, ad = _lru_fwd("lru_fwd_rev" if d else "lru_fwd", xc, wai[l], d, b_a_f[l, d][None], b_i_f[l, d][None],
                                  lam_f[l, d][None], bool(d))
                hs.append(hd)
                avs.append(ad)
            ro = _rnn_out_fwd(hs[0], hs[1], xy)
            x1 = proj_rows("rnn_out", ro, *wout[l], CS, xs)
            mix = (xy, w8, xc, hs, avs, ro)
        if i == 0:
            Wgu0, Wd0 = _exchange_wait("gather_ffn0_wait", ffn0_handle, x1)
            wgu[0], wd[0] = (Wgu0, 0), (Wd0, 0)
        h2 = _rms_fwd("norm_ffn_fwd", x1, norm_ffn[i:i + 1])
        Wgu, li = wgu[i]
        gu = _mm("ffn_gate_up", NN, h2, Wgu, BS((tm, D), lambda i_, j, r: (i_, 0)),
                 BS((None, None, None, D, F8), lambda i_, j, r, li=li: (j % NDEV, li, j // NDEV, 0, 0)),
                 BS((None, None, tm, F8), lambda i_, j, r: (j // NDEV, j % NDEV, i_, 0)),
                 (2, NDEV, S, F8), CD, (S // tm, 2 * NDEV, 1))
        act = _swiglu_fwd(gu)
        Wd, li = wd[i]
        x2 = _mm("ffn_down", NN, act, Wd, BS((2, tm, F8), lambda i_, j, r: (r, i_, 0)),
                 BS((2, None, F8, tn), lambda i_, j, r, li=li: (r, li, 0, j)),
                 BS((tm, tn), lambda i_, j, r: (i_, j)), (S, D), F32, (S // tm, D // tn, NDEV // 2), add=x1,
                 pairs=lambda a_ref, b_ref: [(a_ref[s], b_ref[s]) for s in range(2)])
        saved.append((xs, h1, mix, x1, h2, gu, act))
        xs = x2
        if i + 1 < depth:
            k, lk = i + 1, (i + 1) // 2
            got = _exchange_wait(f"gather_layer{k}_wait", gathers[k], x2)
            wgu[k], wd[k] = (got[-2], 0), (got[-1], 0)
            if k % 2 == 0:
                wqkv[lk], wo[lk] = (got[0], 0), (got[1], 0)
            else:
                win[lk], wout[lk] = (got[0], 0), (got[1], 0)
                wai[lk] = jnp.transpose(got[2], (1, 2, 3, 4, 0, 5, 6)).reshape(2, 2, NB, BW, BW)
                if lk == 0:
                    small = jnp.transpose(got[3], (1, 2, 0, 3)).reshape(n_rnn, SMALL_ROWS, C)
                    conv_w_f, conv_b_f, b_a_f, b_i_f, lam_f = unpack_small(small)

    dx, dxb, sq_err = _loss_head(xs, loss_target[0])
    loss = lax.psum(0.5 * sq_err[0, 0] / D, ("x", "y", "c"))

    g_norm_mix, g_norm_ffn = [None] * depth, [None] * depth
    g_qkv, g_wo, g_qg, g_kg = [None] * n_attn, [None] * n_attn, [None] * n_attn, [None] * n_attn
    g_win, g_wout, g_ai, g_small = [None] * n_rnn, [None] * n_rnn, [None] * n_rnn, [None] * n_rnn
    g_gu, g_down = [None] * depth, [None] * depth
    scat, sent = {}, None
    for i in reversed(range(depth)):
        l = i // 2
        xs, h1, mix, x1, h2, gu, act = saved[i]
        gain_ffn = norm_ffn[i:i + 1] if sent is None else norm_ffn[i:i + 1] + sent[0, 0]
        Wd, li = wd[i]
        dact = _mm("ffn_down_dx", NT, dxb, Wd, BS((tm, D), lambda i_, j, r: (i_, 0)),
                   BS((None, None, F8, D), lambda i_, j, r, li=li: (j, li, 0, 0)),
                   BS((None, tm, F8), lambda i_, j, r: (j, i_, 0)), (NDEV, S, F8), CD, (S // tm, NDEV, 1))
        g_down[i] = _mm("ffn_down_dw", TN, act, dxb, BS((None, tk, F8), lambda i_, j, r: (i_, r, 0)),
                        BS((tk, tn), lambda i_, j, r: (r, j)),
                        BS((None, F8, tn), lambda i_, j, r: (i_, 0, j)), (NDEV, F8, D), CD, (NDEV, D // tn, S // tk))
        dgu = _swiglu_bwd(dact, gu)
        Wgu, li = wgu[i]
        dh2 = _mm("ffn_gate_up_dx", NT, dgu, Wgu, BS((None, 4, tm, F8), lambda i_, j, r: (r // 2, r % 2, i_, 0)),
                  BS((4, None, None, tn, F8), lambda i_, j, r, li=li: (r % 2, li, r // 2, j, 0)),
                  BS((tm, tn), lambda i_, j, r: (i_, j)), (S, D), F32, (S // tm, D // tn, 2 * NDEV // 4),
                  pairs=lambda a_ref, b_ref: [(a_ref[s], b_ref[s]) for s in range(4)])
        g_gu[i] = _mm("ffn_gate_up_dw", TN, h2, dgu, BS((tk, tn), lambda i_, j, r: (r, i_)),
                      BS((None, None, tk, F8), lambda i_, j, r: (j // NDEV, j % NDEV, r, 0)),
                      BS((None, None, tn, F8), lambda i_, j, r: (j % NDEV, j // NDEV, i_, 0)),
                      (NDEV, 2, D, F8), CD, (D // tn, 2 * NDEV, S // tk))
        dx, dxb, g_norm_ffn[i] = _rms_bwd("norm_ffn_bwd", dh2, x1, gain_ffn, dx)
        scat["ffn", i], sent = _exchange_start(f"scatter_ffn{i}_start", [g_gu[i], g_down[i]], scatter=True)
        if i % 2 == 0:
            qkv, qr, kr, o, lse = mix
            do = proj_rows_dx("attn_out_dx", dxb, *wo[l], OS, CD)
            g_wo[l] = proj_rows_dw("attn_out_dw", o, dxb, OS)
            q_gain = attn_q_gain[l:l + 1] + sent[0, 0]
            if i == 0:
                scat["wo", 0], sent = _exchange_start("scatter_wo0_start", [g_wo[0]], scatter=True)
                q_gain = attn_q_gain[l:l + 1] + sent[0, 0]
            delta = _attn_delta(do, o, NQ, NKV)
            dq, dk, dv = _flash_bwd(qr, kr, qkv, do, jnp.transpose(lse[:, :, :G], (0, 2, 1)),
                                    jnp.transpose(delta[:, :, :G], (0, 2, 1)), NQ, NKV)
            dqkv, g_qg[l], g_kg[l] = _qk_prep_bwd(qkv, dq, dk, dv, q_gain, attn_k_gain[l:l + 1], cos_t, sin_t, NQ, NKV)
            g_qkv[l] = proj_cols_dw("attn_qkv_dw", h1, dqkv, QS)
            if i == 0:
                scat["qkv", 0], sent = _exchange_start("scatter_qkv0_start", [g_qkv[0]], scatter=True)
            dh1 = proj_cols_dx("attn_qkv_dx", dqkv, *wqkv[l], QS)
        else:
            xy, w8, xc, hs, avs, ro = mix
            dro = proj_rows_dx("rnn_out_dx", dxb, *wout[l], CS, F32)
            g_wout[l] = proj_rows_dw("rnn_out_dw", ro, dxb, CS)
            dhs, dyb = _rnn_out_bwd(dro, hs[0], hs[1], xy)
            dxc, dwa, dwi, dba, dbi, dlam = None, [], [], [], [], []
            for d in range(2):
                res = _lru_bwd("lru_bwd_rev" if d else "lru_bwd", dhs, avs[d], hs[d], xc, wai[l], d, b_a_f[l, d][None],
                               b_i_f[l, d][None], lam_f[l, d][None], bool(d), dxc)
                dxc = res[0]
                for lst, val in zip((dwa, dwi, dba, dbi, dlam), res[1:]):
                    lst.append(val)
            dxb_branch, dw8 = _conv_bwd(dxc, xy, w8)
            dxy = jnp.concatenate([dxb_branch, dyb], axis=1)
            g_win[l] = proj_cols_dw("rnn_in_dw", h1, dxy, IS)
            dh1 = proj_cols_dx("rnn_in_dx", dxy, *win[l], IS)
            dai = jnp.stack([jnp.stack(dwa), jnp.stack(dwi)])
            g_ai[l] = jnp.transpose(dai.reshape(2, 2, NB, NDEV, BWS, BW), (3, 0, 1, 2, 4, 5))
            sm = rows_at([(0, dw8[0:4]), (8, dw8[4:5]), (16, dba[0]), (17, dba[1]), (24, dbi[0]), (25, dbi[1]),
                          (32, dlam[0]), (33, dlam[1])], SMALL_ROWS)
            g_small[l] = jnp.transpose(sm.reshape(SMALL_ROWS, NDEV, CS), (1, 0, 2))
        dx, dxb, g_norm_mix[i] = _rms_bwd("norm_mix_bwd", dh1, xs, norm_mix[i:i + 1] + sent[0, 0], dx)
        if i % 2 == 1:
            scat["mix", i], sent = _exchange_start(f"scatter_mix{i}_start", [g_win[l], g_wout[l], g_ai[l], g_small[l]], scatter=True)
        elif i > 0:
            scat["mix", i], sent = _exchange_start(f"scatter_mix{i}_start", [g_qkv[l], g_wo[l]], scatter=True)

    L_gu, L_down = [None] * depth, [None] * depth
    L_qkv, L_wo = [None] * n_attn, [None] * n_attn
    L_win, L_wout, L_ai, L_small = [None] * n_rnn, [None] * n_rnn, [None] * n_rnn, [None] * n_rnn
    norm_part = rows_at(list(enumerate(g_norm_mix + g_norm_ffn)), 2 * depth)
    gain_part = rows_at(list(enumerate(g_qg + g_kg)), 8)
    L_norm, L_gain = _exchange("gather_replicated", [norm_part, gain_part], scatter=False)
    (L_wo[0],) = _exchange_wait("scatter_wo0_wait", scat["wo", 0], L_norm)
    (L_qkv[0],) = _exchange_wait("scatter_qkv0_wait", scat["qkv", 0], L_norm)
    for i in reversed(range(depth)):
        l = i // 2
        L_gu[i], L_down[i] = _exchange_wait(f"scatter_ffn{i}_wait", scat["ffn", i], L_norm)
        if i % 2 == 1:
            L_win[l], L_wout[l], L_ai[l], L_small[l] = _exchange_wait(f"scatter_mix{i}_wait", scat["mix", i], L_norm)
        elif i > 0:
            L_qkv[l], L_wo[l] = _exchange_wait(f"scatter_mix{i}_wait", scat["mix", i], L_norm)

    out = {}
    out["attn_w_qkv"] = _adamw("adamw_qkv", L_qkv, attn_w_qkv, m_attn_w_qkv, v_attn_w_qkv)
    out["attn_w_o"] = _adamw("adamw_wo", L_wo, attn_w_o, m_attn_w_o, v_attn_w_o)
    out["rnn_w_in"] = _adamw("adamw_win", L_win, rnn_w_in, m_rnn_w_in, v_rnn_w_in)
    out["rnn_w_out"] = _adamw("adamw_wout", L_wout, rnn_w_out, m_rnn_w_out, v_rnn_w_out)
    out["ffn_w_down"] = _adamw("adamw_down", L_down, ffn_w_down, m_ffn_w_down, v_ffn_w_down)
    out["ffn_w_gate"] = _adamw("adamw_gate", L_gu, ffn_w_gate, m_ffn_w_gate, v_ffn_w_gate, pick=0)
    out["ffn_w_up"] = _adamw("adamw_up", L_gu, ffn_w_up, m_ffn_w_up, v_ffn_w_up, pick=1)
    out["rnn_w_a"] = _adamw("adamw_w_a", L_ai, rnn_w_a, m_rnn_w_a, v_rnn_w_a, pick=0)
    out["rnn_w_i"] = _adamw("adamw_w_i", L_ai, rnn_w_i, m_rnn_w_i, v_rnn_w_i, pick=1)
    small_res = _adamw("adamw_small", L_small, small_sh,
                       pack_small(m_rnn_conv_w, m_rnn_conv_b, m_rnn_b_a, m_rnn_b_i, m_rnn_lambda),
                       pack_small(v_rnn_conv_w, v_rnn_conv_b, v_rnn_b_a, v_rnn_b_i, v_rnn_lambda))
    for name, vals in zip(("rnn_conv_w", "rnn_conv_b", "rnn_b_a", "rnn_b_i", "rnn_lambda"),
                          zip(*[unpack_small(r) for r in small_res])):
        out[name] = list(vals)
    one = lambda t: t[None]
    for name, land, w, m, v in (("norm_mix", L_norm[:, :depth], norm_mix, m_norm_mix, v_norm_mix),
                                ("norm_ffn", L_norm[:, depth:], norm_ffn, m_norm_ffn, v_norm_ffn),
                                ("attn_q_gain", L_gain[:, :n_attn], attn_q_gain, m_attn_q_gain, v_attn_q_gain),
                                ("attn_k_gain", L_gain[:, n_attn:2 * n_attn], attn_k_gain, m_attn_k_gain, v_attn_k_gain)):
        out[name] = [r[0] for r in _adamw("adamw_" + name, [land], one(w), one(m), one(v))]

    order = ["norm_mix", "norm_ffn", "attn_w_qkv", "attn_q_gain", "attn_k_gain", "attn_w_o", "rnn_w_in", "rnn_conv_w",
             "rnn_conv_b", "rnn_w_a", "rnn_b_a", "rnn_w_i", "rnn_b_i", "rnn_lambda", "rnn_w_out", "ffn_w_gate",
             "ffn_w_up", "ffn_w_down"]
    return (loss, dx[None], *[out[k][0] for k in order], *[out[k][1] for k in order], *[out[k][2] for k in order],
            *[out[k][3] for k in order])
```
